```python
import math
import jax
import jax.numpy as jnp
from jax import lax
import numpy as np

D_MODEL = 2048
BATCH = 4
SEQ = 2048
DEPTH = 4
DEC_BATCH = 8
DEC_SEQ = 1
PAST_LEN = 16384
PAGE_SIZE = 128

HEAD_DIM = 128
HG_HEADS = 4
HG_WIDTH = HG_HEADS * HEAD_DIM
ATT_HEADS = 8
ATT_WIDTH = ATT_HEADS * HEAD_DIM
ML_HEADS = 4
ML_WIDTH = ML_HEADS * HEAD_DIM
D_MIX = HG_WIDTH + ATT_WIDTH + ML_WIDTH
D_FF = 5632
MOBA_BLOCK = 256
MOBA_TOPK = 3
Q_SWEEP = 16
N_BUCKETS = 32
MAX_DISTANCE = 4096
HG_CHUNK = 16
ML_CHUNK = 64
CONV_W = 4
EPS = 1e-6
NEG_BIG = -1e30
IN_SIZES = (HG_WIDTH, HG_WIDTH, HG_WIDTH, HG_WIDTH, ATT_WIDTH, ATT_WIDTH, ATT_WIDTH, 2 * ML_WIDTH, ML_WIDTH, ML_WIDTH, 2 * ML_HEADS)
N_IN = 4 * HG_WIDTH + 3 * ATT_WIDTH + 4 * ML_WIDTH + 2 * ML_HEADS

kernel_name = 'hymba_hgrn2_moba_mlstm_decode_step'


def rmsnorm(x, g):
    xf = x.astype(jnp.float32)
    y = xf * lax.rsqrt(jnp.mean(xf * xf, axis=-1, keepdims=True) + EPS)
    return (y * g.astype(jnp.float32)).astype(x.dtype)


def swiglu(x, w_gate, w_up, w_down):
    return (jax.nn.silu(x @ w_gate) * (x @ w_up)) @ w_down


def to_heads(u, n_heads):
    b_, t_, _ = u.shape
    return u.reshape(b_, t_, n_heads, -1).transpose(0, 2, 1, 3)


def from_heads(o):
    b_, n_, t_, d_ = o.shape
    return o.transpose(0, 2, 1, 3).reshape(b_, t_, n_ * d_)


def pad_time(a, n_pad, axis, value=0.0):
    widths = [(0, 0)] * a.ndim
    widths[axis] = (0, n_pad)
    return jnp.pad(a, widths, constant_values=value)


def to_chunks(a, chunk, value=0.0):
    t_ = a.shape[2]
    n_chunks = -(-t_ // chunk)
    a = pad_time(a, n_chunks * chunk - t_, 2, value)
    a = a.reshape(a.shape[:2] + (n_chunks, chunk) + a.shape[3:])
    return jnp.moveaxis(a, 2, 0)


def from_chunks(o, t_):
    o = jnp.moveaxis(o, 0, 2)
    return o.reshape(o.shape[:2] + (-1,) + o.shape[4:])[:, :, :t_]


def hgrn2_chunked(q, k, v, log_f, s0):
    t_ = q.shape[2]
    xs = (to_chunks(q, HG_CHUNK), to_chunks(k, HG_CHUNK), to_chunks(v, HG_CHUNK), to_chunks(log_f, HG_CHUNK))
    causal = jnp.tril(jnp.ones((HG_CHUNK, HG_CHUNK), dtype=bool))[:, :, None]

    def step(s, chunk):
        qc, kc, vc, lf = chunk
        cum = jnp.cumsum(lf, axis=2)
        o_inter = jnp.einsum('bhtd,bhde->bhte', qc * jnp.exp(cum), s)
        diff = cum[:, :, :, None, :] - cum[:, :, None, :, :]
        decay = jnp.where(causal, jnp.exp(jnp.minimum(diff, 0.0)), 0.0)
        scores = jnp.einsum('bhtd,bhsd,bhtsd->bhts', qc, kc, decay)
        o = o_inter + jnp.einsum('bhts,bhse->bhte', scores, vc)
        last = cum[:, :, -1, :]
        s = jnp.exp(last)[..., None] * s + jnp.einsum('bhsd,bhse->bhde', kc * jnp.exp(last[:, :, None, :] - cum), vc)
        return s, o

    s_final, o = lax.scan(step, s0.astype(jnp.float32), xs)
    return from_chunks(o, t_), s_final


def hgrn2_mixer(u_q, u_f, u_i, u_g, lb, out_norm, s0):
    f32 = jnp.float32
    log_f = jnp.logaddexp(jnp.log(lb), jnp.log1p(-lb) + jax.nn.log_sigmoid(u_f.astype(f32)))
    k = -jnp.expm1(log_f)
    o, s_final = hgrn2_chunked(to_heads(u_q.astype(f32), HG_HEADS), to_heads(k, HG_HEADS),
                               to_heads(u_i.astype(f32), HG_HEADS), to_heads(log_f, HG_HEADS), s0)
    o = rmsnorm(o, out_norm.reshape(HG_HEADS, 1, HEAD_DIM))
    return (from_heads(o) * jax.nn.silu(u_g.astype(f32))).astype(u_q.dtype), s_final


def causal_conv(u, buf, w, b):
    full = jnp.concatenate([buf.astype(u.dtype), u], axis=1)
    y = lax.conv_general_dilated(full, w[:, None, :].astype(u.dtype), window_strides=(1,), padding='VALID',
                                 dimension_numbers=('NWC', 'WIO', 'NWC'), feature_group_count=u.shape[-1])
    return jax.nn.silu(y + b.astype(u.dtype)), full[:, -(CONV_W - 1):]


def mlstm_chunked(q, k, v, i_pre, log_f, c0, n0, m0):
    f32 = jnp.float32
    t_ = q.shape[2]
    xs = (to_chunks(q, ML_CHUNK), to_chunks(k, ML_CHUNK), to_chunks(v, ML_CHUNK),
          to_chunks(i_pre, ML_CHUNK, NEG_BIG), to_chunks(log_f, ML_CHUNK))
    causal = jnp.tril(jnp.ones((ML_CHUNK, ML_CHUNK), dtype=bool))

    def step(carry, chunk):
        c, n, m = carry
        qc, kc, vc, ic, fc = chunk
        cum = jnp.cumsum(fc, axis=-1)
        log_d = jnp.where(causal, cum[..., :, None] - cum[..., None, :] + ic[..., None, :], -jnp.inf)
        m_inter = cum + m[..., None]
        m_t = jnp.maximum(m_inter, jnp.max(log_d, axis=-1))
        w_inter = jnp.exp(m_inter - m_t)
        s = jnp.einsum('bhtd,bhsd->bhts', qc, kc) * jnp.exp(log_d - m_t[..., None])
        num = w_inter[..., None] * jnp.einsum('bhtd,bhde->bhte', qc, c) + jnp.einsum('bhts,bhse->bhte', s, vc)
        den = w_inter * jnp.einsum('bhtd,bhd->bht', qc, n) + jnp.sum(s, axis=-1)
        h = num / jnp.maximum(jnp.abs(den), jnp.exp(-m_t))[..., None]
        m_new = m_t[..., -1]
        w_k = jnp.exp(cum[..., -1:] - cum + ic - m_new[..., None])
        decay = jnp.exp(cum[..., -1] + m - m_new)
        c = decay[..., None, None] * c + jnp.einsum('bhs,bhsd,bhse->bhde', w_k, kc, vc)
        n = decay[..., None] * n + jnp.einsum('bhs,bhsd->bhd', w_k, kc)
        return (c, n, m_new), h

    (c, n, m), h = lax.scan(step, (c0.astype(f32), n0.astype(f32), m0.astype(f32)), xs)
    return from_chunks(h, t_), c, n, m


def mlstm_mixer(u_qk, u_v, u_o, u_gates, conv_w, conv_b, gate_bias, out_norm, c0, n0, m0, conv0):
    f32 = jnp.float32
    qk, conv_new = causal_conv(u_qk, conv0, conv_w, conv_b)
    u_q, u_k = jnp.split(qk, 2, axis=-1)
    q = to_heads(u_q.astype(f32), ML_HEADS)
    k = to_heads(u_k.astype(f32), ML_HEADS) * HEAD_DIM ** -0.5
    v = to_heads(u_v.astype(f32), ML_HEADS)
    g = u_gates.astype(f32) + gate_bias.astype(f32)
    i_pre = g[..., :ML_HEADS].transpose(0, 2, 1)
    log_f = jax.nn.log_sigmoid(g[..., ML_HEADS:]).transpose(0, 2, 1)
    h, c, n, m = mlstm_chunked(q, k, v, i_pre, log_f, c0, n0, m0)
    h = rmsnorm(h, out_norm.reshape(ML_HEADS, 1, HEAD_DIM))
    out = from_heads(h) * jax.nn.sigmoid(u_o.astype(f32))
    return out.astype(u_v.dtype), c, n, m, conv_new


def t5_bucket(rel):
    max_exact = N_BUCKETS // 2
    relf = jnp.maximum(rel, 1).astype(jnp.float32)
    large = max_exact + (jnp.log(relf / max_exact) / math.log(MAX_DISTANCE / max_exact) * (N_BUCKETS - max_exact)).astype(jnp.int32)
    return jnp.where(rel < max_exact, rel, jnp.minimum(large, N_BUCKETS - 1))


def moba_attention(q, k, v, q_start, rel_bias):
    f32 = jnp.float32
    b_, tq, h_, d_ = q.shape
    seq_k = k.shape[1]
    n_blocks = -(-seq_k // MOBA_BLOCK)
    qb = math.gcd(tq, Q_SWEEP)
    span = MOBA_BLOCK + qb
    n_pad = n_blocks * MOBA_BLOCK + qb - seq_k
    kp = pad_time(k, n_pad, 1)
    vp = pad_time(v, n_pad, 1)
    k_blocks = kp[:, :n_blocks * MOBA_BLOCK].reshape(b_, n_blocks, MOBA_BLOCK, h_, d_)
    v_blocks = vp[:, :n_blocks * MOBA_BLOCK].reshape(b_, n_blocks, MOBA_BLOCK, h_, d_)
    k_mean = jnp.mean(k_blocks, axis=2, dtype=f32)
    n_sel = min(MOBA_TOPK, n_blocks)
    bias_tab = rel_bias.T.astype(f32)
    scale = HEAD_DIM ** -0.5
    bi = jnp.arange(b_)[:, None, None, None]
    hi = jnp.arange(h_)[None, :, None, None]

    def query_block(i):
        qs = lax.dynamic_slice_in_dim(q, i * qb, qb, axis=1)
        pos = q_start + i * qb + jnp.arange(qb)
        cur = pos // MOBA_BLOCK
        gate = jnp.einsum('bqhd,bnhd->bhqn', qs.astype(f32), k_mean)
        gate = jnp.where(jnp.arange(n_blocks)[None, :] < cur[:, None], gate, -jnp.inf)
        _, sel = lax.top_k(gate, n_sel)
        valid = sel < cur[:, None]
        k_sel = k_blocks[bi, sel, :, hi]
        v_sel = v_blocks[bi, sel, :, hi]
        key_pos = sel[..., None] * MOBA_BLOCK + jnp.arange(MOBA_BLOCK)
        bias_sel = bias_tab[hi[..., None], t5_bucket(pos[:, None, None] - key_pos)]
        logit_sel = jnp.einsum('bqhd,bhqnkd->bhqnk', qs, k_sel, preferred_element_type=f32) * scale + bias_sel
        logit_sel = jnp.where(valid[..., None], logit_sel, -jnp.inf).reshape(b_, h_, qb, n_sel * MOBA_BLOCK)
        own_start = (pos[0] // MOBA_BLOCK) * MOBA_BLOCK
        k_own = lax.dynamic_slice_in_dim(kp, own_start, span, axis=1)
        v_own = lax.dynamic_slice_in_dim(vp, own_start, span, axis=1)
        own_pos = own_start + jnp.arange(span)
        own_mask = (own_pos[None, :] <= pos[:, None]) & (own_pos[None, :] // MOBA_BLOCK == cur[:, None])
        bias_own = bias_tab[:, t5_bucket(jnp.maximum(pos[:, None] - own_pos[None, :], 0))]
        logit_own = jnp.einsum('bqhd,bkhd->bhqk', qs, k_own, preferred_element_type=f32) * scale + bias_own
        logit_own = jnp.where(own_mask, logit_own, -jnp.inf)
        p = jax.nn.softmax(jnp.concatenate([logit_sel, logit_own], axis=-1), axis=-1)
        p_sel = p[..., :n_sel * MOBA_BLOCK].reshape(b_, h_, qb, n_sel, MOBA_BLOCK).astype(v.dtype)
        p_own = p[..., n_sel * MOBA_BLOCK:].astype(v.dtype)
        return jnp.einsum('bhqnk,bhqnkd->bqhd', p_sel, v_sel) + jnp.einsum('bhqk,bkhd->bqhd', p_own, v_own)

    outs = lax.map(query_block, jnp.arange(tq // qb))
    return outs.transpose(1, 0, 2, 3, 4).reshape(b_, tq, h_ * d_)


def mixer_block(h, q_start, w_in, w_out, lb, hg_norm, rel_bias, conv_w, conv_b, gate_bias, ml_norm,
                hg_s0, c0, n0, m0, conv0, past_k, past_v):
    n_seq, n_tok, _ = h.shape
    u = h @ w_in
    split_points = np.cumsum(IN_SIZES)[:-1].tolist()
    hq, hf, hin, hg, aq, ak, av, mqk, mv, mo, mgate = jnp.split(u, split_points, axis=-1)
    o_hg, hg_s = hgrn2_mixer(hq, hf, hin, hg, lb, hg_norm, hg_s0)
    head_shape = (n_seq, n_tok, ATT_HEADS, HEAD_DIM)
    ak = ak.reshape(head_shape)
    av = av.reshape(head_shape)
    if past_k is None:
        k_all, v_all = ak, av
    else:
        k_all = jnp.concatenate([past_k.astype(ak.dtype), ak], axis=1)
        v_all = jnp.concatenate([past_v.astype(av.dtype), av], axis=1)
    o_att = moba_attention(aq.reshape(head_shape), k_all, v_all, q_start, rel_bias)
    o_ml, c, n, m, conv_new = mlstm_mixer(mqk, mv, mo, mgate, conv_w, conv_b, gate_bias, ml_norm, c0, n0, m0, conv0)
    out = jnp.concatenate([o_hg, o_att, o_ml], axis=-1) @ w_out
    return out, (ak, av, hg_s, c, n, m, conv_new)


def run_trunk(x, q_start, hg0, c0, n0, m0, conv0, cache_k, cache_v, page_table,
              ln_ffn1, w_ffn1_gate, w_ffn1_up, w_ffn1_down, ln_mix, w_in, w_out,
              hgrn_lb_logits, hgrn_out_norm, rel_bias, mlstm_conv_w, mlstm_conv_b,
              mlstm_gate_bias, mlstm_out_norm, ln_ffn2, w_ffn2_gate, w_ffn2_up, w_ffn2_down, ln_final):
    lb = jnp.cumsum(jax.nn.softmax(hgrn_lb_logits.astype(jnp.float32), axis=0), axis=0)
    lb = lb - lb[0]
    new_state = ([], [], [], [], [], [], [])
    for l in range(DEPTH):
        x = x + 0.5 * swiglu(rmsnorm(x, ln_ffn1[l]), w_ffn1_gate[l], w_ffn1_up[l], w_ffn1_down[l])
        if cache_k is None:
            past_k, past_v = None, None
        else:
            n_seq, n_pages = page_table.shape
            past_shape = (n_seq, n_pages * PAGE_SIZE, ATT_HEADS, HEAD_DIM)
            past_k = cache_k[l, page_table].reshape(past_shape)
            past_v = cache_v[l, page_table].reshape(past_shape)
        mix, layer_state = mixer_block(rmsnorm(x, ln_mix[l]), q_start, w_in[l], w_out[l], lb[l], hgrn_out_norm[l],
                                       rel_bias, mlstm_conv_w[l], mlstm_conv_b[l], mlstm_gate_bias[l],
                                       mlstm_out_norm[l], hg0[l], c0[l], n0[l], m0[l], conv0[l], past_k, past_v)
        x = x + mix
        x = x + 0.5 * swiglu(rmsnorm(x, ln_ffn2[l]), w_ffn2_gate[l], w_ffn2_up[l], w_ffn2_down[l])
        for store, s in zip(new_state, layer_state):
            store.append(s)
    return rmsnorm(x, ln_final), [jnp.stack(s) for s in new_state]


def setup_inputs(seed: int = 0) -> dict:
    key = jax.random.key(seed)
    ks = jax.random.split(key, 32)
    f32 = jnp.float32
    n_pages = PAST_LEN // PAGE_SIZE
    n_pool = (DEC_BATCH * n_pages * 5) // 4

    def normal(k, shape, scale):
        return scale * jax.random.normal(k, shape, f32)

    def gain(k, shape):
        return 1.0 + normal(k, shape, 0.05)

    cache_shape = (DEPTH, n_pool, PAGE_SIZE, ATT_HEADS, HEAD_DIM)
    page_table = jax.random.permutation(ks[4], n_pool)[:DEC_BATCH * n_pages].reshape(DEC_BATCH, n_pages).astype(jnp.int32)
    forget_bias = jnp.linspace(3.0, 6.0, ML_HEADS, dtype=f32)
    return {
        'x_prompt': normal(ks[0], (BATCH, SEQ, D_MODEL), 1.0),
        'x_sample': normal(ks[1], (DEC_BATCH, DEC_SEQ, D_MODEL), 1.0),
        'cache_k': normal(ks[2], cache_shape, 1.0),
        'cache_v': normal(ks[3], cache_shape, 1.0),
        'page_table': page_table,
        'state_hgrn': normal(ks[5], (DEPTH, DEC_BATCH, HG_HEADS, HEAD_DIM, HEAD_DIM), 0.5),
        'state_mlstm_c': normal(ks[6], (DEPTH, DEC_BATCH, ML_HEADS, HEAD_DIM, HEAD_DIM), 0.1),
        'state_mlstm_n': normal(ks[7], (DEPTH, DEC_BATCH, ML_HEADS, HEAD_DIM), 0.1),
        'state_mlstm_m': jax.random.uniform(ks[8], (DEPTH, DEC_BATCH, ML_HEADS), f32, 0.0, 3.0),
        'state_mlstm_conv': normal(ks[9], (DEPTH, DEC_BATCH, CONV_W - 1, 2 * ML_WIDTH), 1.0),
        'ln_ffn1': gain(ks[10], (DEPTH, D_MODEL)),
        'w_ffn1_gate': normal(ks[11], (DEPTH, D_MODEL, D_FF), D_MODEL ** -0.5),
        'w_ffn1_up': normal(ks[12], (DEPTH, D_MODEL, D_FF), D_MODEL ** -0.5),
        'w_ffn1_down': normal(ks[13], (DEPTH, D_FF, D_MODEL), D_FF ** -0.5),
        'ln_mix': gain(ks[14], (DEPTH, D_MODEL)),
        'w_in': normal(ks[15], (DEPTH, D_MODEL, N_IN), D_MODEL ** -0.5),
        'w_out': normal(ks[16], (DEPTH, D_MIX, D_MODEL), D_MIX ** -0.5),
        'hgrn_lb_logits': normal(ks[17], (DEPTH, HG_WIDTH), 0.5),
        'hgrn_out_norm': gain(ks[18], (DEPTH, HG_WIDTH)),
        'rel_bias': normal(ks[19], (N_BUCKETS, ATT_HEADS), 0.5),
        'mlstm_conv_w': normal(ks[20], (DEPTH, CONV_W, 2 * ML_WIDTH), CONV_W ** -0.5),
        'mlstm_conv_b': normal(ks[21], (DEPTH, 2 * ML_WIDTH), 0.01),
        'mlstm_gate_bias': jnp.concatenate([normal(ks[22], (DEPTH, ML_HEADS), 0.1),
                                            forget_bias + normal(ks[23], (DEPTH, ML_HEADS), 0.1)], axis=-1),
        'mlstm_out_norm': gain(ks[24], (DEPTH, ML_WIDTH)),
        'ln_ffn2': gain(ks[25], (DEPTH, D_MODEL)),
        'w_ffn2_gate': normal(ks[26], (DEPTH, D_MODEL, D_FF), D_MODEL ** -0.5),
        'w_ffn2_up': normal(ks[27], (DEPTH, D_MODEL, D_FF), D_MODEL ** -0.5),
        'w_ffn2_down': normal(ks[28], (DEPTH, D_FF, D_MODEL), D_FF ** -0.5),
        'ln_final': gain(ks[29], (D_MODEL,)),
    }


def reference(x_prompt, x_sample, cache_k, cache_v, page_table, state_hgrn, state_mlstm_c, state_mlstm_n,
              state_mlstm_m, state_mlstm_conv, ln_ffn1, w_ffn1_gate, w_ffn1_up, w_ffn1_down, ln_mix, w_in, w_out,
              hgrn_lb_logits, hgrn_out_norm, rel_bias, mlstm_conv_w, mlstm_conv_b, mlstm_gate_bias,
              mlstm_out_norm, ln_ffn2, w_ffn2_gate, w_ffn2_up, w_ffn2_down, ln_final):
    f32 = jnp.float32
    weights = (ln_ffn1, w_ffn1_gate, w_ffn1_up, w_ffn1_down, ln_mix, w_in, w_out, hgrn_lb_logits, hgrn_out_norm,
               rel_bias, mlstm_conv_w, mlstm_conv_b, mlstm_gate_bias, mlstm_out_norm, ln_ffn2, w_ffn2_gate,
               w_ffn2_up, w_ffn2_down, ln_final)
    bp = x_prompt.shape[0]
    hg0 = jnp.zeros((DEPTH, bp, HG_HEADS, HEAD_DIM, HEAD_DIM), f32)
    c0 = jnp.zeros((DEPTH, bp, ML_HEADS, HEAD_DIM, HEAD_DIM), f32)
    n0 = jnp.zeros((DEPTH, bp, ML_HEADS, HEAD_DIM), f32)
    m0 = jnp.zeros((DEPTH, bp, ML_HEADS), f32)
    conv0 = jnp.zeros((DEPTH, bp, CONV_W - 1, 2 * ML_WIDTH), x_prompt.dtype)
    y_prompt, new_p = run_trunk(x_prompt, 0, hg0, c0, n0, m0, conv0, None, None, None, *weights)
    past_len = page_table.shape[1] * PAGE_SIZE
    y_sample, new_s = run_trunk(x_sample, past_len, state_hgrn, state_mlstm_c, state_mlstm_n, state_mlstm_m,
                                state_mlstm_conv, cache_k, cache_v, page_table, *weights)
    k_p, v_p, hg_p, c_p, n_p, m_p, conv_p = new_p
    k_s, v_s, hg_s, c_s, n_s, m_s, conv_s = new_s
    return (y_prompt, y_sample, k_p, v_p, k_s, v_s, hg_p, hg_s, c_p, c_s, n_p, n_s, m_p, m_s, conv_p, conv_s)
```

```python
import functools
import math

import jax
import jax.numpy as jnp
import numpy as np
from jax import lax
from jax.experimental import pallas as pl
from jax.experimental.pallas import tpu as pltpu

F32 = jnp.float32
BF16 = jnp.bfloat16

HEAD_DIM = 128
HG_HEADS = 4
ATT_HEADS = 8
ML_HEADS = 4
HG_WIDTH = HG_HEADS * HEAD_DIM
ATT_WIDTH = ATT_HEADS * HEAD_DIM
ML_WIDTH = ML_HEADS * HEAD_DIM
N_MAIN = 4 * HG_WIDTH + 3 * ATT_WIDTH + 4 * ML_WIDTH
N_GATE = 2 * ML_HEADS
OFF_HQ, OFF_HF, OFF_HI, OFF_HG = 0, HG_WIDTH, 2 * HG_WIDTH, 3 * HG_WIDTH
OFF_AQ = 4 * HG_WIDTH
OFF_AK = OFF_AQ + ATT_WIDTH
OFF_AV = OFF_AK + ATT_WIDTH
OFF_MQK = OFF_AV + ATT_WIDTH
OFF_MV = OFF_MQK + 2 * ML_WIDTH
OFF_MO = OFF_MV + ML_WIDTH

PAGE_SIZE = 128
MOBA_BLOCK = 256
MOBA_TOPK = 3
N_BUCKETS = 32
MAX_DISTANCE = 4096
HG_CHUNK = 16
CONV_W = 4
EPS = 1e-6
NEG_BIG = -1e30
LANES = 128
SUBLANES = 8
VMEM_LIMIT = 48 * 1024 * 1024
SAMPLE_PAD = 128

_HI = lax.Precision.HIGHEST


def _cparams(sem):
    return pltpu.CompilerParams(dimension_semantics=sem, vmem_limit_bytes=VMEM_LIMIT)


def _rms(x, g):
    return x * lax.rsqrt(jnp.mean(x * x, axis=-1, keepdims=True) + EPS) * g


def _log_sigmoid(x):
    return jnp.minimum(x, 0.0) - jnp.log1p(jnp.exp(-jnp.abs(x)))


def _dot(a, b):
    return jnp.dot(a, b, preferred_element_type=F32)


def _dot_nt(a, b, precision=None):
    return lax.dot_general(a, b, (((1,), (1,)), ((), ())), preferred_element_type=F32, precision=precision)


def _dot_tn(a, b):
    return lax.dot_general(a, b, (((0,), (0,)), ((), ())), preferred_element_type=F32)


def _ffn_kernel(x_ref, g_ref, wg_ref, wu_ref, wd_ref, o_ref, xn_ref, *, nj):
    j = pl.program_id(1)

    @pl.when(j == 0)
    def _():
        xn_ref[...] = _rms(x_ref[...], g_ref[...]).astype(BF16)
        o_ref[...] = jnp.zeros_like(o_ref)

    xn = xn_ref[...]
    a = _dot(xn, wg_ref[...])
    b = _dot(xn, wu_ref[...])
    h = (a * jax.nn.sigmoid(a)) * b
    o_ref[...] += _dot(h.astype(BF16), wd_ref[...])

    @pl.when(j == nj - 1)
    def _():
        o_ref[...] = x_ref[...] + 0.5 * o_ref[...]


def _ffn(x, ln, wg, wu, wd, l, tm):
    m, d = x.shape
    f = wg.shape[-1]
    tf = 512 if f % 512 == 0 else f
    nj = f // tf
    return pl.pallas_call(
        functools.partial(_ffn_kernel, nj=nj),
        out_shape=jax.ShapeDtypeStruct((m, d), F32),
        grid=(m // tm, nj),
        in_specs=[
            pl.BlockSpec((tm, d), lambda i, j: (i, 0)),
            pl.BlockSpec((None, 1, d), lambda i, j: (l, 0, 0)),
            pl.BlockSpec((None, d, tf), lambda i, j: (l, 0, j)),
            pl.BlockSpec((None, d, tf), lambda i, j: (l, 0, j)),
            pl.BlockSpec((None, tf, d), lambda i, j: (l, j, 0)),
        ],
        out_specs=pl.BlockSpec((tm, d), lambda i, j: (i, 0)),
        scratch_shapes=[pltpu.VMEM((tm, d), BF16)],
        compiler_params=_cparams(("parallel", "arbitrary")),
        name="ffn",
    )(x, ln, wg, wu, wd)


def _inproj_kernel(x_ref, g_ref, w_ref, wgate_ref, u_ref, gate_ref, xn_ref):
    @pl.when(pl.program_id(1) == 0)
    def _():
        xn = _rms(x_ref[...], g_ref[...]).astype(BF16)
        xn_ref[...] = xn
        gate_ref[...] = _dot(xn, wgate_ref[...])

    u_ref[...] = _dot(xn_ref[...], w_ref[...])


def _inproj(x, ln, w, wgate, l, tm):
    m, d = x.shape
    tn = 512
    return pl.pallas_call(
        _inproj_kernel,
        out_shape=(jax.ShapeDtypeStruct((m, N_MAIN), F32), jax.ShapeDtypeStruct((m, LANES), F32)),
        grid=(m // tm, N_MAIN // tn),
        in_specs=[
            pl.BlockSpec((tm, d), lambda i, j: (i, 0)),
            pl.BlockSpec((None, 1, d), lambda i, j: (l, 0, 0)),
            pl.BlockSpec((None, d, tn), lambda i, j: (l, 0, j)),
            pl.BlockSpec((None, d, LANES), lambda i, j: (l, 0, 0)),
        ],
        out_specs=(pl.BlockSpec((tm, tn), lambda i, j: (i, j)), pl.BlockSpec((tm, LANES), lambda i, j: (i, 0))),
        scratch_shapes=[pltpu.VMEM((tm, d), BF16)],
        compiler_params=_cparams(("parallel", "arbitrary")),
        name="inproj",
    )(x, ln, w, wgate)


def _outproj_kernel(x_ref, a_ref, b_ref, c_ref, w_ref, o_ref):
    acc = _dot(a_ref[...], w_ref[0:HG_WIDTH, :])
    acc += _dot(b_ref[...], w_ref[HG_WIDTH:HG_WIDTH + ATT_WIDTH, :])
    acc += _dot(c_ref[...], w_ref[HG_WIDTH + ATT_WIDTH:, :])
    o_ref[...] = x_ref[...] + acc


def _outproj(x, o_hg, o_att, o_ml, w, l, tm):
    m, d = x.shape
    dm = w.shape[1]
    return pl.pallas_call(
        _outproj_kernel,
        out_shape=jax.ShapeDtypeStruct((m, d), F32),
        grid=(m // tm,),
        in_specs=[
            pl.BlockSpec((tm, d), lambda i: (i, 0)),
            pl.BlockSpec((tm, HG_WIDTH), lambda i: (i, 0)),
            pl.BlockSpec((tm, ATT_WIDTH), lambda i: (i, 0)),
            pl.BlockSpec((tm, ML_WIDTH), lambda i: (i, 0)),
            pl.BlockSpec((None, dm, d), lambda i: (l, 0, 0)),
        ],
        out_specs=pl.BlockSpec((tm, d), lambda i: (i, 0)),
        compiler_params=_cparams(("parallel",)),
        name="outproj",
    )(x, o_hg, o_att, o_ml, w)


def _final_norm_kernel(x_ref, g_ref, o_ref):
    o_ref[...] = _rms(x_ref[...], g_ref[...])


def _final_norm(x, g, tm):
    m, d = x.shape
    return pl.pallas_call(
        _final_norm_kernel,
        out_shape=jax.ShapeDtypeStruct((m, d), F32),
        grid=(m // tm,),
        in_specs=[pl.BlockSpec((tm, d), lambda i: (i, 0)), pl.BlockSpec((1, d), lambda i: (0, 0))],
        out_specs=pl.BlockSpec((tm, d), lambda i: (i, 0)),
        compiler_params=_cparams(("parallel",)),
        name="final_norm",
    )(x, g)


def _hgrn_kernel(uq_ref, uf_ref, ui_ref, ug_ref, llb_ref, l1m_ref, oml_ref, norm_ref, s0_ref,
                 o_ref, sout_ref, st_ref, oacc_ref, *, tt, t_valid, t_total):
    c = HG_CHUNK
    ti = pl.program_id(1)
    nt = pl.num_programs(1)

    @pl.when(ti == 0)
    def _():
        for h in range(HG_HEADS):
            st_ref[h] = s0_ref[0, h].T

    row = lax.broadcasted_iota(jnp.int32, (c, c), 0)
    col = lax.broadcasted_iota(jnp.int32, (c, c), 1)
    tril = (col <= row).astype(F32)
    srow = lax.broadcasted_iota(jnp.int32, (c, HEAD_DIM), 0)
    sel_r = lax.broadcasted_iota(jnp.int32, (c, c * c), 0)
    sel_c = lax.broadcasted_iota(jnp.int32, (c, c * c), 1)
    sel_bd = jnp.where((sel_c >= sel_r * c) & (sel_c < sel_r * c + c), 1.0, 0.0).astype(BF16)
    ones = jnp.ones((HEAD_DIM, HEAD_DIM), BF16)
    llb = llb_ref[...]
    l1m = l1m_ref[...]
    oml = oml_ref[...]

    def chunk(ci, carry):
        r0 = pl.multiple_of(ci * c, c)
        uf = uf_ref[pl.ds(r0, c), :]
        b = l1m + _log_sigmoid(uf)
        lf = jnp.maximum(llb, b) + jnp.log1p(jnp.exp(-jnp.abs(llb - b)))
        kk = oml * jax.nn.sigmoid(-uf)
        if t_valid < t_total:
            valid = (ti * tt + r0 + lax.broadcasted_iota(jnp.int32, (c, HG_WIDTH), 0)) < t_valid
            lf = jnp.where(valid, lf, 0.0)
            kk = jnp.where(valid, kk, 0.0)
        cum = jnp.dot(tril, lf, preferred_element_type=F32, precision=_HI)
        uq = uq_ref[pl.ds(r0, c), :]
        ui = ui_ref[pl.ds(r0, c), :]
        for h in range(HG_HEADS):
            hs = slice(h * HEAD_DIM, (h + 1) * HEAD_DIM)
            q, k, v, cm = uq[:, hs], kk[:, hs], ui[:, hs], cum[:, hs]
            st = st_ref[h]
            o_inter = _dot_nt((q * jnp.exp(cm)).astype(BF16), st.astype(BF16))
            ws = []
            for t in range(c):
                dec = jnp.exp(jnp.minimum(cm[t:t + 1, :] - cm, 0.0))
                ws.append(jnp.where(srow <= t, (q[t:t + 1, :] * k) * dec, 0.0))
            w = jnp.concatenate(ws, axis=0).astype(BF16)
            r = _dot(w, ones)
            vt = jnp.concatenate([v] * c, axis=0)
            o_intra = _dot(sel_bd, (r * vt).astype(BF16))
            oacc_ref[pl.ds(r0, c), hs] = o_inter + o_intra
            last = cm[c - 1:c, :]
            kd = (k * jnp.exp(last - cm)).astype(BF16)
            st_ref[h] = st * jnp.exp(last) + _dot_tn(v.astype(BF16), kd)
        return carry

    lax.fori_loop(0, tt // c, chunk, 0)

    for h in range(HG_HEADS):
        hs = slice(h * HEAD_DIM, (h + 1) * HEAD_DIM)
        g = ug_ref[:, hs]
        y = _rms(oacc_ref[:, hs], norm_ref[:, hs]) * (g * jax.nn.sigmoid(g))
        o_ref[:, hs] = y.astype(o_ref.dtype)

    @pl.when(ti == nt - 1)
    def _():
        for h in range(HG_HEADS):
            sout_ref[0, h] = st_ref[h].T


def _hgrn(u, llb, l1m, oml, norm, s0, l, n_seq, t_total, t_valid):
    tt = min(t_total, 512)
    nt = t_total // tt
    cb = HG_WIDTH
    ublock = lambda k: pl.BlockSpec((tt, cb), lambda b, t: (b * nt + t, k))
    par = pl.BlockSpec((None, 1, cb), lambda b, t: (l, 0, 0))
    sblock = pl.BlockSpec((1, HG_HEADS, HEAD_DIM, HEAD_DIM), lambda b, t: (b, 0, 0, 0))
    return pl.pallas_call(
        functools.partial(_hgrn_kernel, tt=tt, t_valid=t_valid, t_total=t_total),
        out_shape=(jax.ShapeDtypeStruct((n_seq * t_total, cb), BF16),
                   jax.ShapeDtypeStruct((n_seq, HG_HEADS, HEAD_DIM, HEAD_DIM), F32)),
        grid=(n_seq, nt),
        in_specs=[ublock(OFF_HQ // cb), ublock(OFF_HF // cb), ublock(OFF_HI // cb), ublock(OFF_HG // cb),
                  par, par, par, par, sblock],
        out_specs=(pl.BlockSpec((tt, cb), lambda b, t: (b * nt + t, 0)), sblock),
        scratch_shapes=[pltpu.VMEM((HG_HEADS, HEAD_DIM, HEAD_DIM), F32), pltpu.VMEM((tt, cb), F32)],
        compiler_params=_cparams(("parallel", "arbitrary")),
        name="hgrn",
    )(u, u, u, u, llb, l1m, oml, norm, s0)


def _mlstm_kernel(qk_ref, v_ref, og_ref, gate_ref, cw_ref, cb_ref, gb_ref, norm_ref, c0_ref, n0_ref, m0_ref,
                  conv0_ref, o_ref, cout_ref, nout_ref, mout_ref, c_ref, n_ref, m_ref, carry_ref,
                  *, ll, t_valid, t_total):
    ti = pl.program_id(1)
    nt = pl.num_programs(1)

    @pl.when(ti == 0)
    def _():
        c_ref[...] = c0_ref[0]
        n_ref[...] = n0_ref[0]
        m_ref[...] = m0_ref[0]
        carry_ref[...] = conv0_ref[0]

    x = qk_ref[...]
    xe = jnp.concatenate([carry_ref[...], x], axis=0)
    cw = cw_ref[...]
    y = cb_ref[...] + cw[3:4, :] * x
    for j in range(1, CONV_W):
        y += cw[3 - j:4 - j, :] * xe[SUBLANES - j:SUBLANES - j + ll, :]
    carry_ref[...] = x[ll - SUBLANES:, :]
    qk = y * jax.nn.sigmoid(y)

    g = gate_ref[...] + gb_ref[...]
    lf = _log_sigmoid(g)
    ipre = g
    if t_valid < t_total:
        valid = (ti * ll + lax.broadcasted_iota(jnp.int32, (ll, LANES), 0)) < t_valid
        lf = jnp.where(valid, lf, 0.0)
        ipre = jnp.where(valid, ipre, NEG_BIG)
    row = lax.broadcasted_iota(jnp.int32, (ll, ll), 0)
    col = lax.broadcasted_iota(jnp.int32, (ll, ll), 1)
    causal = col <= row
    cum = jnp.dot(causal.astype(F32), lf, preferred_element_type=F32, precision=_HI)
    lane = lax.broadcasted_iota(jnp.int32, (ll, LANES), 1)
    a_t = jnp.where(lane < ML_HEADS, ipre, cum).T

    for h in range(ML_HEADS):
        hs = slice(h * HEAD_DIM, (h + 1) * HEAD_DIM)
        q = qk[:, hs]
        k = qk[:, ML_WIDTH + h * HEAD_DIM:ML_WIDTH + (h + 1) * HEAD_DIM] * HEAD_DIM ** -0.5
        v = v_ref[:, hs]
        col_cum = cum[:, ML_HEADS + h:ML_HEADS + h + 1]
        col_i = ipre[:, h:h + 1]
        row_cum = a_t[ML_HEADS + h:ML_HEADS + h + 1, :]
        row_i = a_t[h:h + 1, :]
        m_prev = m_ref[h:h + 1, 0:1]
        log_d = jnp.where(causal, col_cum - row_cum + row_i, -jnp.inf)
        m_inter = col_cum + m_prev
        m_t = jnp.maximum(m_inter, jnp.max(log_d, axis=-1, keepdims=True))
        w_inter = jnp.exp(m_inter - m_t)
        qb, kb, vb = q.astype(BF16), k.astype(BF16), v.astype(BF16)
        s = _dot_nt(qb, kb) * jnp.exp(log_d - m_t)
        c_old = c_ref[h]
        n_old = n_ref[h:h + 1, :]
        num = w_inter * _dot(qb, c_old.astype(BF16)) + _dot(s.astype(BF16), vb)
        den = w_inter * jnp.sum(q * n_old, axis=-1, keepdims=True) + jnp.sum(s, axis=-1, keepdims=True)
        hh = num / jnp.maximum(jnp.abs(den), jnp.exp(-m_t))
        m_new = m_t[ll - 1:ll, :]
        cum_last = col_cum[ll - 1:ll, :]
        w_k = jnp.exp(cum_last - col_cum + col_i - m_new)
        decay = jnp.exp(cum_last + m_prev - m_new)
        kw = k * w_k
        c_ref[h] = decay * c_old + _dot_tn(kw.astype(BF16), vb)
        n_ref[h:h + 1, :] = decay * n_old + jnp.sum(kw, axis=0, keepdims=True)
        m_ref[h:h + 1, :] = jnp.broadcast_to(m_new, (1, LANES))
        og = og_ref[:, hs]
        o_ref[:, hs] = (_rms(hh, norm_ref[:, hs]) * jax.nn.sigmoid(og)).astype(o_ref.dtype)

    @pl.when(ti == nt - 1)
    def _():
        cout_ref[0] = c_ref[...]
        nout_ref[0] = n_ref[...]
        mout_ref[0] = m_ref[...]


def _mlstm(u, gates, cw, cb, gb, norm, c0, n0, m0, conv0, l, n_seq, t_total, t_valid, ll):
    nt = t_total // ll
    ublock = lambda w, k: pl.BlockSpec((ll, w), lambda b, t: (b * nt + t, k))
    par = lambda r, w: pl.BlockSpec((None, r, w), lambda b, t: (l, 0, 0))
    cblock = pl.BlockSpec((1, ML_HEADS, HEAD_DIM, HEAD_DIM), lambda b, t: (b, 0, 0, 0))
    vblock = pl.BlockSpec((1, SUBLANES, LANES), lambda b, t: (b, 0, 0))
    return pl.pallas_call(
        functools.partial(_mlstm_kernel, ll=ll, t_valid=t_valid, t_total=t_total),
        out_shape=(jax.ShapeDtypeStruct((n_seq * t_total, ML_WIDTH), BF16),
                   jax.ShapeDtypeStruct((n_seq, ML_HEADS, HEAD_DIM, HEAD_DIM), F32),
                   jax.ShapeDtypeStruct((n_seq, SUBLANES, LANES), F32),
                   jax.ShapeDtypeStruct((n_seq, SUBLANES, LANES), F32)),
        grid=(n_seq, nt),
        in_specs=[ublock(2 * ML_WIDTH, OFF_MQK // (2 * ML_WIDTH)), ublock(ML_WIDTH, OFF_MV // ML_WIDTH),
                  ublock(ML_WIDTH, OFF_MO // ML_WIDTH),
                  pl.BlockSpec((ll, LANES), lambda b, t: (b * nt + t, 0)),
                  par(CONV_W, 2 * ML_WIDTH), par(1, 2 * ML_WIDTH), par(1, LANES), par(1, ML_WIDTH),
                  cblock, vblock, vblock,
                  pl.BlockSpec((1, SUBLANES, 2 * ML_WIDTH), lambda b, t: (b, 0, 0))],
        out_specs=(pl.BlockSpec((ll, ML_WIDTH), lambda b, t: (b * nt + t, 0)), cblock, vblock, vblock),
        scratch_shapes=[pltpu.VMEM((ML_HEADS, HEAD_DIM, HEAD_DIM), F32), pltpu.VMEM((SUBLANES, LANES), F32),
                        pltpu.VMEM((SUBLANES, LANES), F32), pltpu.VMEM((SUBLANES, 2 * ML_WIDTH), F32)],
        compiler_params=_cparams(("parallel", "arbitrary")),
        name="mlstm",
    )(u, u, u, gates, cw, cb, gb, norm, c0, n0, m0, conv0)


def _moba_prefill_kernel(q_ref, k_ref, v_ref, bias_ref, o_ref, kmean_ref, m_ref, l_ref, acc_ref, sel_ref, *, nb):
    blk = MOBA_BLOCK
    i = pl.program_id(2)
    scale = HEAD_DIM ** -0.5

    @pl.when(i == 0)
    def _():
        kmean_ref[...] = jnp.zeros_like(kmean_ref)
        for n in range(nb):
            kmean_ref[n:n + 1, :] = jnp.mean(k_ref[n * blk:(n + 1) * blk, :], axis=0, keepdims=True)

    q = q_ref[...]
    qb = q.astype(BF16)

    gate = _dot_nt(q, kmean_ref[...], precision=_HI)
    lane = lax.broadcasted_iota(jnp.int32, (blk, LANES), 1)
    past = lane < i
    sel = jnp.zeros((blk, LANES), F32)
    for n in range(nb - 1):
        gn = gate[:, n:n + 1]
        ahead = jnp.where(lane < n, jnp.where(gate >= gn, 1.0, 0.0), jnp.where(gate > gn, 1.0, 0.0))
        rank = jnp.sum(jnp.where(past, ahead, 0.0), axis=-1, keepdims=True)
        sel = jnp.where(lane == n, jnp.where(rank < MOBA_TOPK, 1.0, 0.0), sel)
    sel_ref[...] = sel

    r0 = pl.multiple_of(i * blk, blk)
    row = lax.broadcasted_iota(jnp.int32, (blk, blk), 0)
    col = lax.broadcasted_iota(jnp.int32, (blk, blk), 1)
    logit = _dot_nt(qb, k_ref[pl.ds(r0, blk), :].astype(BF16)) * scale + bias_ref[0]
    logit = jnp.where(col <= row, logit, -jnp.inf)
    m0 = jnp.max(logit, axis=-1, keepdims=True)
    p = jnp.exp(logit - m0)
    m_ref[...] = m0
    l_ref[...] = jnp.sum(p, axis=-1, keepdims=True)
    acc_ref[...] = _dot(p.astype(BF16), v_ref[pl.ds(r0, blk), :].astype(BF16))

    for n in range(nb - 1):
        @pl.when(n < i)
        def _(n=n):
            chosen = sel_ref[:, n:n + 1] > 0.0
            lg = _dot_nt(qb, k_ref[n * blk:(n + 1) * blk, :].astype(BF16)) * scale + bias_ref[i - n]
            lg = jnp.where(chosen, lg, -jnp.inf)
            m_old = m_ref[...]
            m_new = jnp.maximum(m_old, jnp.max(lg, axis=-1, keepdims=True))
            alpha = jnp.exp(m_old - m_new)
            pp = jnp.exp(lg - m_new)
            m_ref[...] = m_new
            l_ref[...] = alpha * l_ref[...] + jnp.sum(pp, axis=-1, keepdims=True)
            acc_ref[...] = alpha * acc_ref[...] + _dot(pp.astype(BF16), v_ref[n * blk:(n + 1) * blk, :].astype(BF16))

    o_ref[...] = (acc_ref[...] / l_ref[...]).astype(o_ref.dtype)


def _moba_prefill(u, bias, n_seq, t_total):
    blk = MOBA_BLOCK
    nb = t_total // blk
    hd = HEAD_DIM
    return pl.pallas_call(
        functools.partial(_moba_prefill_kernel, nb=nb),
        out_shape=jax.ShapeDtypeStruct((n_seq * t_total, ATT_WIDTH), BF16),
        grid=(ATT_HEADS, n_seq, nb),
        in_specs=[
            pl.BlockSpec((blk, hd), lambda h, b, i: (b * nb + i, OFF_AQ // hd + h)),
            pl.BlockSpec((t_total, hd), lambda h, b, i: (b, OFF_AK // hd + h)),
            pl.BlockSpec((t_total, hd), lambda h, b, i: (b, OFF_AV // hd + h)),
            pl.BlockSpec((None, nb, blk, blk), lambda h, b, i: (h, 0, 0, 0)),
        ],
        out_specs=pl.BlockSpec((blk, hd), lambda h, b, i: (b * nb + i, h)),
        scratch_shapes=[pltpu.VMEM((LANES, hd), F32), pltpu.VMEM((blk, 1), F32), pltpu.VMEM((blk, 1), F32),
                        pltpu.VMEM((blk, hd), F32), pltpu.VMEM((blk, LANES), F32)],
        compiler_params=_cparams(("parallel", "parallel", "arbitrary")),
        name="moba_prefill",
    )(u, u, u, bias)


def _kmean_kernel(pt_ref, k0_ref, k1_ref, o_ref):
    s = jnp.sum(k0_ref[...], axis=0, keepdims=True) + jnp.sum(k1_ref[...], axis=0, keepdims=True)
    o_ref[0, 0] = s * (1.0 / MOBA_BLOCK)


def _moba_kmean(cache_k, page_table, l):
    n_seq, n_pages = page_table.shape
    nblk = n_pages * PAGE_SIZE // MOBA_BLOCK
    page = lambda j: pl.BlockSpec((None, None, PAGE_SIZE, ATT_WIDTH), lambda b, n, pt: (l, pt[b, 2 * n + j], 0, 0))
    return pl.pallas_call(
        _kmean_kernel,
        out_shape=jax.ShapeDtypeStruct((n_seq, nblk, 1, ATT_WIDTH), F32),
        grid_spec=pltpu.PrefetchScalarGridSpec(
            num_scalar_prefetch=1, grid=(n_seq, nblk), in_specs=[page(0), page(1)],
            out_specs=pl.BlockSpec((1, 1, 1, ATT_WIDTH), lambda b, n, pt: (b, n, 0, 0))),
        compiler_params=_cparams(("parallel", "arbitrary")),
        name="moba_kmean",
    )(page_table, cache_k, cache_k)


def _moba_select_kernel(km_ref, q_ref, o_ref, *, nblk):
    prod = km_ref[0] * q_ref[0]
    r = lax.broadcasted_iota(jnp.int32, (ATT_WIDTH, LANES), 0)
    c = lax.broadcasted_iota(jnp.int32, (ATT_WIDTH, LANES), 1)
    head_of = jnp.where((r >= c * HEAD_DIM) & (r < (c + 1) * HEAD_DIM), 1.0, 0.0)
    g = jnp.dot(prod, head_of, preferred_element_type=F32, precision=_HI)
    idx = lax.broadcasted_iota(jnp.int32, (nblk, LANES), 0).astype(F32)
    o_ref[...] = jnp.zeros_like(o_ref)
    for j in range(MOBA_TOPK):
        mx = jnp.max(g, axis=0, keepdims=True)
        first = jnp.min(jnp.where(g == mx, idx, float(nblk)), axis=0, keepdims=True)
        o_ref[0, j:j + 1, :] = first.astype(jnp.int32)
        g = jnp.where(idx == first, -jnp.inf, g)


def _moba_select(kmean, q3):
    n_seq, nblk, _ = kmean.shape
    return pl.pallas_call(
        functools.partial(_moba_select_kernel, nblk=nblk),
        out_shape=jax.ShapeDtypeStruct((n_seq, SUBLANES, LANES), jnp.int32),
        grid=(n_seq,),
        in_specs=[pl.BlockSpec((1, nblk, ATT_WIDTH), lambda b: (b, 0, 0)),
                  pl.BlockSpec((1, 1, ATT_WIDTH), lambda b: (b, 0, 0))],
        out_specs=pl.BlockSpec((1, SUBLANES, LANES), lambda b: (b, 0, 0)),
        compiler_params=_cparams(("parallel",)),
        name="moba_select",
    )(kmean, q3)


def _moba_decode_kernel(phys_ref, logi_ref, *refs, n_sel_pages):
    k_refs = refs[:n_sel_pages]
    v_refs = refs[n_sel_pages:2 * n_sel_pages]
    q_ref, kn_ref, vn_ref, bias_ref, bias0_ref, o_ref = refs[2 * n_sel_pages:]
    b = pl.program_id(0)
    h = pl.program_id(1)
    scale = HEAD_DIM ** -0.5
    q = q_ref[0]
    q8 = jnp.broadcast_to(q, (SUBLANES, HEAD_DIM)).astype(BF16)
    base = (b * ATT_HEADS + h) * n_sel_pages
    logits = []
    for j in range(n_sel_pages):
        lp = logi_ref[base + j]
        logits.append(_dot_nt(q8, k_refs[j][...].astype(BF16)) * scale + bias_ref[pl.ds(lp, 1), :])
    kn = kn_ref[0]
    self_logit = _dot_nt(q8, jnp.broadcast_to(kn, (SUBLANES, HEAD_DIM)).astype(BF16))[:, 0:1] * scale \
        + bias0_ref[:, 0:1]
    m = self_logit
    for lg in logits:
        m = jnp.maximum(m, jnp.max(lg, axis=-1, keepdims=True))
    p_self = jnp.exp(self_logit - m)
    den = p_self
    vn = vn_ref[0]
    acc = p_self * vn
    for j in range(n_sel_pages):
        p = jnp.exp(logits[j] - m)
        den += jnp.sum(p, axis=-1, keepdims=True)
        acc += _dot(p.astype(BF16), v_refs[j][...].astype(BF16))
    o_ref[0] = (acc / den)[0:1, :].astype(o_ref.dtype)


def _moba_decode(cache_k, cache_v, phys, logi, q3, k3, v3, bias_s, bias0, l):
    n_seq = q3.shape[0]
    n_sel_pages = MOBA_TOPK * (MOBA_BLOCK // PAGE_SIZE)
    hd = HEAD_DIM

    def page(j):
        return pl.BlockSpec((None, None, PAGE_SIZE, hd),
                            lambda b, h, ph, lg: (l, ph[(b * ATT_HEADS + h) * n_sel_pages + j], 0, h))

    tok = pl.BlockSpec((1, 1, hd), lambda b, h, ph, lg: (b, 0, h))
    return pl.pallas_call(
        functools.partial(_moba_decode_kernel, n_sel_pages=n_sel_pages),
        out_shape=jax.ShapeDtypeStruct((n_seq, 1, ATT_WIDTH), BF16),
        grid_spec=pltpu.PrefetchScalarGridSpec(
            num_scalar_prefetch=2, grid=(n_seq, ATT_HEADS),
            in_specs=[page(j) for j in range(n_sel_pages)] + [page(j) for j in range(n_sel_pages)]
            + [tok, tok, tok,
               pl.BlockSpec((None, bias_s.shape[1], LANES), lambda b, h, ph, lg: (h, 0, 0)),
               pl.BlockSpec((None, 1, LANES), lambda b, h, ph, lg: (h, 0, 0))],
            out_specs=tok),
        compiler_params=_cparams(("parallel", "arbitrary")),
        name="moba_decode",
    )(phys, logi, *([cache_k] * n_sel_pages), *([cache_v] * n_sel_pages), q3, k3, v3, bias_s, bias0)


def _t5_bucket_np(rel):
    rel = np.asarray(rel, np.int64)
    max_exact = N_BUCKETS // 2
    relf = np.maximum(rel, 1).astype(np.float64)
    large = max_exact + (np.log(relf / max_exact) / math.log(MAX_DISTANCE / max_exact)
                         * (N_BUCKETS - max_exact)).astype(np.int64)
    return np.where(rel < max_exact, rel, np.minimum(large, N_BUCKETS - 1)).astype(np.int32)


def _pad_rows(a, rows):
    n, _, w = a.shape
    return jnp.pad(a, ((0, 0), (0, rows - 1), (0, 0))).reshape(n * rows, w)


def _vec_state(a):
    if a.ndim == 2:
        a = jnp.broadcast_to(a[:, :, None], a.shape + (LANES,))
    return jnp.pad(a, ((0, 0), (0, SUBLANES - a.shape[1]), (0, 0)))


def kernel(x_prompt, x_sample, cache_k, cache_v, page_table, state_hgrn, state_mlstm_c, state_mlstm_n,
           state_mlstm_m, state_mlstm_conv, ln_ffn1, w_ffn1_gate, w_ffn1_up, w_ffn1_down, ln_mix, w_in, w_out,
           hgrn_lb_logits, hgrn_out_norm, rel_bias, mlstm_conv_w, mlstm_conv_b, mlstm_gate_bias,
           mlstm_out_norm, ln_ffn2, w_ffn2_gate, w_ffn2_up, w_ffn2_down, ln_final):
    depth = w_in.shape[0]
    bp, tp, d = x_prompt.shape
    bs, ts, _ = x_sample.shape
    assert ts == 1 and tp % MOBA_BLOCK == 0
    n_pages = page_table.shape[1]
    past_len = n_pages * PAGE_SIZE
    assert past_len % MOBA_BLOCK == 0 and past_len // MOBA_BLOCK >= MOBA_TOPK
    n_pool = cache_k.shape[1]

    bf = lambda w: w.astype(BF16)
    wg1, wu1, wd1 = bf(w_ffn1_gate), bf(w_ffn1_up), bf(w_ffn1_down)
    wg2, wu2, wd2 = bf(w_ffn2_gate), bf(w_ffn2_up), bf(w_ffn2_down)
    w_in_b = bf(w_in)
    w_gate_b = bf(jnp.pad(w_in[:, :, N_MAIN:], ((0, 0), (0, 0), (0, LANES - N_GATE))))
    w_out_b = bf(w_out)
    row3 = lambda a: a.reshape(depth, 1, -1)
    ln1, lnm, ln2 = row3(ln_ffn1), row3(ln_mix), row3(ln_ffn2)
    lb = jnp.cumsum(jax.nn.softmax(hgrn_lb_logits.astype(F32), axis=0), axis=0)
    lb = lb - lb[0]
    llb, l1m, oml = row3(jnp.log(lb)), row3(jnp.log1p(-lb)), row3(1.0 - lb)
    hnorm, mnorm = row3(hgrn_out_norm), row3(mlstm_out_norm)
    conv_b = row3(mlstm_conv_b)
    gate_b = row3(jnp.pad(mlstm_gate_bias, ((0, 0), (0, LANES - N_GATE))))
    bias_tab = rel_bias.T.astype(F32)
    nbp = tp // MOBA_BLOCK
    ar = np.arange(MOBA_BLOCK)
    rel_p = np.arange(nbp)[:, None, None] * MOBA_BLOCK + ar[None, :, None] - ar[None, None, :]
    bias_p = bias_tab[:, _t5_bucket_np(np.maximum(rel_p, 0))]
    bias_s = bias_tab[:, _t5_bucket_np(past_len - np.arange(past_len))].reshape(ATT_HEADS, n_pages, PAGE_SIZE)
    bias0 = jnp.broadcast_to(bias_tab[:, 0][:, None, None], (ATT_HEADS, 1, LANES))
    cache_k4 = cache_k.reshape(depth, n_pool, PAGE_SIZE, ATT_WIDTH)
    cache_v4 = cache_v.reshape(depth, n_pool, PAGE_SIZE, ATT_WIDTH)

    zeros = lambda *s: jnp.zeros(s, F32)
    tm_p = 512 if (bp * tp) % 512 == 0 else bp * tp
    ll_p = 256

    xp = x_prompt.reshape(bp * tp, d)
    xs = x_sample.reshape(bs, d)
    outs = {k: [] for k in ("kp", "vp", "ks", "vs", "hgp", "hgs", "cp", "cs", "np", "ns", "mp", "ms", "cvp", "cvs")}

    for l in range(depth):
        xp = _ffn(xp, ln1, wg1, wu1, wd1, l, tm_p)
        u, gates = _inproj(xp, lnm, w_in_b, w_gate_b, l, tm_p)
        o_hg, hg_s = _hgrn(u, llb, l1m, oml, hnorm, zeros(bp, HG_HEADS, HEAD_DIM, HEAD_DIM), l, bp, tp, tp)
        o_att = _moba_prefill(u, bias_p, bp, tp)
        o_ml, c_s, n_s, m_s = _mlstm(u, gates, mlstm_conv_w, conv_b, gate_b, mnorm,
                                     zeros(bp, ML_HEADS, HEAD_DIM, HEAD_DIM), zeros(bp, SUBLANES, LANES),
                                     zeros(bp, SUBLANES, LANES), zeros(bp, SUBLANES, 2 * ML_WIDTH),
                                     l, bp, tp, tp, ll_p)
        xp = _outproj(xp, o_hg, o_att, o_ml, w_out_b, l, tm_p)
        xp = _ffn(xp, ln2, wg2, wu2, wd2, l, tm_p)
        u3 = u.reshape(bp, tp, N_MAIN)
        outs["kp"].append(u3[:, :, OFF_AK:OFF_AV].reshape(bp, tp, ATT_HEADS, HEAD_DIM))
        outs["vp"].append(u3[:, :, OFF_AV:OFF_MQK].reshape(bp, tp, ATT_HEADS, HEAD_DIM))
        outs["hgp"].append(hg_s)
        outs["cp"].append(c_s)
        outs["np"].append(n_s[:, :ML_HEADS])
        outs["mp"].append(m_s[:, :ML_HEADS, 0])
        outs["cvp"].append(u3[:, tp - (CONV_W - 1):, OFF_MQK:OFF_MV])

        xs = _ffn(xs, ln1, wg1, wu1, wd1, l, bs)
        us, gs = _inproj(xs, lnm, w_in_b, w_gate_b, l, bs)
        us_pad = _pad_rows(us[:, None, :], SAMPLE_PAD)
        gs_pad = _pad_rows(gs[:, None, :], SAMPLE_PAD)
        o_hg_s, hg_ss = _hgrn(us_pad, llb, l1m, oml, hnorm, state_hgrn[l], l, bs, SAMPLE_PAD, 1)
        conv0 = jnp.pad(state_mlstm_conv[l], ((0, 0), (SUBLANES - (CONV_W - 1), 0), (0, 0)))
        o_ml_s, c_ss, n_ss, m_ss = _mlstm(us_pad, gs_pad, mlstm_conv_w, conv_b, gate_b, mnorm, state_mlstm_c[l],
                                          _vec_state(state_mlstm_n[l]), _vec_state(state_mlstm_m[l]), conv0,
                                          l, bs, SAMPLE_PAD, 1, SAMPLE_PAD)
        q3 = us[:, None, OFF_AQ:OFF_AK]
        k3 = us[:, None, OFF_AK:OFF_AV]
        v3 = us[:, None, OFF_AV:OFF_MQK]
        kmean = _moba_kmean(cache_k4, page_table, l).reshape(bs, -1, ATT_WIDTH)
        sel = _moba_select(kmean, q3)[:, :MOBA_TOPK, :ATT_HEADS]
        sel = jnp.transpose(sel, (0, 2, 1))
        ppb = MOBA_BLOCK // PAGE_SIZE
        logi = (sel[..., None] * ppb + jnp.arange(ppb, dtype=jnp.int32)).reshape(bs, ATT_HEADS, -1)
        phys = jnp.take_along_axis(page_table[:, None, :], logi, axis=2)
        o_att_s = _moba_decode(cache_k4, cache_v4, phys.reshape(-1), logi.reshape(-1), q3, k3, v3,
                               bias_s, bias0, l).reshape(bs, ATT_WIDTH)
        first = lambda a: a.reshape(bs, SAMPLE_PAD, -1)[:, 0]
        xs = _outproj(xs, first(o_hg_s), o_att_s, first(o_ml_s), w_out_b, l, bs)
        xs = _ffn(xs, ln2, wg2, wu2, wd2, l, bs)
        outs["ks"].append(k3.reshape(bs, 1, ATT_HEADS, HEAD_DIM))
        outs["vs"].append(v3.reshape(bs, 1, ATT_HEADS, HEAD_DIM))
        outs["hgs"].append(hg_ss)
        outs["cs"].append(c_ss)
        outs["ns"].append(n_ss[:, :ML_HEADS])
        outs["ms"].append(m_ss[:, :ML_HEADS, 0])
        outs["cvs"].append(jnp.concatenate([state_mlstm_conv[l][:, 1:], us[:, None, OFF_MQK:OFF_MV]], axis=1))

    y_prompt = _final_norm(xp, ln_final.reshape(1, d), tm_p).reshape(bp, tp, d)
    y_sample = _final_norm(xs, ln_final.reshape(1, d), bs).reshape(bs, 1, d)
    st = {k: jnp.stack(v) for k, v in outs.items()}
    return (y_prompt, y_sample, st["kp"], st["vp"], st["ks"], st["vs"], st["hgp"], st["hgs"], st["cp"], st["cs"],
            st["np"], st["ns"], st["mp"], st["ms"], st["cvp"], st["cvs"])
```

```python
import functools
import math

import jax
import jax.numpy as jnp
import numpy as np
from jax import lax
from jax.experimental import pallas as pl
from jax.experimental.pallas import tpu as pltpu

F32 = jnp.float32
BF16 = jnp.bfloat16

HEAD_DIM = 128
HG_HEADS = 4
ATT_HEADS = 8
ML_HEADS = 4
HG_WIDTH = HG_HEADS * HEAD_DIM
ATT_WIDTH = ATT_HEADS * HEAD_DIM
ML_WIDTH = ML_HEADS * HEAD_DIM
N_MAIN = 4 * HG_WIDTH + 3 * ATT_WIDTH + 4 * ML_WIDTH
N_GATE = 2 * ML_HEADS
OFF_HQ, OFF_HF, OFF_HI, OFF_HG = 0, HG_WIDTH, 2 * HG_WIDTH, 3 * HG_WIDTH
OFF_AQ = 4 * HG_WIDTH
OFF_AK = OFF_AQ + ATT_WIDTH
OFF_AV = OFF_AK + ATT_WIDTH
OFF_MQK = OFF_AV + ATT_WIDTH
OFF_MV = OFF_MQK + 2 * ML_WIDTH
OFF_MO = OFF_MV + ML_WIDTH

PAGE_SIZE = 128
MOBA_BLOCK = 256
MOBA_TOPK = 3
PAGES_PER_BLOCK = MOBA_BLOCK // PAGE_SIZE
N_SEL_PAGES = MOBA_TOPK * PAGES_PER_BLOCK
N_BUCKETS = 32
MAX_DISTANCE = 4096
HG_CHUNK = 16
CONV_W = 4
EPS = 1e-6
NEG_BIG = -1e30
LANES = 128
SUBLANES = 8
VMEM_LIMIT = 48 * 1024 * 1024
SAMPLE_PAD = 128

_HI = lax.Precision.HIGHEST


def _cparams(sem):
    return pltpu.CompilerParams(dimension_semantics=sem, vmem_limit_bytes=VMEM_LIMIT)


def _rms(x, g):
    return x * lax.rsqrt(jnp.mean(x * x, axis=-1, keepdims=True) + EPS) * g


def _log_sigmoid(x):
    return jnp.minimum(x, 0.0) - jnp.log1p(jnp.exp(-jnp.abs(x)))


def _dot(a, b):
    return jnp.dot(a, b, preferred_element_type=F32)


def _dot_nt(a, b, precision=None):
    return lax.dot_general(a, b, (((1,), (1,)), ((), ())), preferred_element_type=F32, precision=precision)


def _dot_tn(a, b):
    return lax.dot_general(a, b, (((0,), (0,)), ((), ())), preferred_element_type=F32)


def _ffn_kernel(x_ref, g_ref, wg_ref, wu_ref, wd_ref, o_ref, xn_ref, *, nj):
    j = pl.program_id(1)

    @pl.when(j == 0)
    def _():
        xn_ref[...] = _rms(x_ref[...], g_ref[...]).astype(BF16)
        o_ref[...] = jnp.zeros_like(o_ref)

    xn = xn_ref[...]
    a = _dot(xn, wg_ref[...])
    b = _dot(xn, wu_ref[...])
    h = (a * jax.nn.sigmoid(a)) * b
    o_ref[...] += _dot(h.astype(BF16), wd_ref[...])

    @pl.when(j == nj - 1)
    def _():
        o_ref[...] = x_ref[...] + 0.5 * o_ref[...]


def _ffn(x, ln, wg, wu, wd, l, tm):
    m, d = x.shape
    f = wg.shape[-1]
    tf = 512 if f % 512 == 0 else f
    nj = f // tf
    return pl.pallas_call(
        functools.partial(_ffn_kernel, nj=nj),
        out_shape=jax.ShapeDtypeStruct((m, d), F32),
        grid=(m // tm, nj),
        in_specs=[
            pl.BlockSpec((tm, d), lambda i, j: (i, 0)),
            pl.BlockSpec((None, 1, d), lambda i, j: (l, 0, 0)),
            pl.BlockSpec((None, d, tf), lambda i, j: (l, 0, j)),
            pl.BlockSpec((None, d, tf), lambda i, j: (l, 0, j)),
            pl.BlockSpec((None, tf, d), lambda i, j: (l, j, 0)),
        ],
        out_specs=pl.BlockSpec((tm, d), lambda i, j: (i, 0)),
        scratch_shapes=[pltpu.VMEM((tm, d), BF16)],
        compiler_params=_cparams(("parallel", "arbitrary")),
        name="ffn",
    )(x, ln, wg, wu, wd)


def _inproj_kernel(x_ref, g_ref, w_ref, wgate_ref, u_ref, gate_ref, xn_ref):
    @pl.when(pl.program_id(1) == 0)
    def _():
        xn = _rms(x_ref[...], g_ref[...]).astype(BF16)
        xn_ref[...] = xn
        gate_ref[...] = _dot(xn, wgate_ref[...])

    u_ref[...] = _dot(xn_ref[...], w_ref[...])


def _inproj(x, ln, w, wgate, l, tm):
    m, d = x.shape
    tn = 512
    return pl.pallas_call(
        _inproj_kernel,
        out_shape=(jax.ShapeDtypeStruct((m, N_MAIN), F32), jax.ShapeDtypeStruct((m, LANES), F32)),
        grid=(m // tm, N_MAIN // tn),
        in_specs=[
            pl.BlockSpec((tm, d), lambda i, j: (i, 0)),
            pl.BlockSpec((None, 1, d), lambda i, j: (l, 0, 0)),
            pl.BlockSpec((None, d, tn), lambda i, j: (l, 0, j)),
            pl.BlockSpec((None, d, LANES), lambda i, j: (l, 0, 0)),
        ],
        out_specs=(pl.BlockSpec((tm, tn), lambda i, j: (i, j)), pl.BlockSpec((tm, LANES), lambda i, j: (i, 0))),
        scratch_shapes=[pltpu.VMEM((tm, d), BF16)],
        compiler_params=_cparams(("parallel", "arbitrary")),
        name="inproj",
    )(x, ln, w, wgate)


def _outproj_kernel(x_ref, a_ref, b_ref, c_ref, w_ref, o_ref):
    acc = _dot(a_ref[...], w_ref[0:HG_WIDTH, :])
    acc += _dot(b_ref[...], w_ref[HG_WIDTH:HG_WIDTH + ATT_WIDTH, :])
    acc += _dot(c_ref[...], w_ref[HG_WIDTH + ATT_WIDTH:, :])
    o_ref[...] = x_ref[...] + acc


def _outproj(x, o_hg, o_att, o_ml, w, l, tm):
    m, d = x.shape
    dm = w.shape[1]
    return pl.pallas_call(
        _outproj_kernel,
        out_shape=jax.ShapeDtypeStruct((m, d), F32),
        grid=(m // tm,),
        in_specs=[
            pl.BlockSpec((tm, d), lambda i: (i, 0)),
            pl.BlockSpec((tm, HG_WIDTH), lambda i: (i, 0)),
            pl.BlockSpec((tm, ATT_WIDTH), lambda i: (i, 0)),
            pl.BlockSpec((tm, ML_WIDTH), lambda i: (i, 0)),
            pl.BlockSpec((None, dm, d), lambda i: (l, 0, 0)),
        ],
        out_specs=pl.BlockSpec((tm, d), lambda i: (i, 0)),
        compiler_params=_cparams(("parallel",)),
        name="outproj",
    )(x, o_hg, o_att, o_ml, w)


def _final_norm_kernel(x_ref, g_ref, o_ref):
    o_ref[...] = _rms(x_ref[...], g_ref[...])


def _final_norm(x, g, tm):
    m, d = x.shape
    return pl.pallas_call(
        _final_norm_kernel,
        out_shape=jax.ShapeDtypeStruct((m, d), F32),
        grid=(m // tm,),
        in_specs=[pl.BlockSpec((tm, d), lambda i: (i, 0)), pl.BlockSpec((1, d), lambda i: (0, 0))],
        out_specs=pl.BlockSpec((tm, d), lambda i: (i, 0)),
        compiler_params=_cparams(("parallel",)),
        name="final_norm",
    )(x, g)


def _hgrn_kernel(uq_ref, uf_ref, ui_ref, ug_ref, llb_ref, l1m_ref, oml_ref, norm_ref, s0_ref,
                 o_ref, sout_ref, st_ref, oacc_ref, *, tt, t_valid, t_total):
    c = HG_CHUNK
    ti = pl.program_id(1)
    nt = pl.num_programs(1)

    @pl.when(ti == 0)
    def _():
        for h in range(HG_HEADS):
            st_ref[h] = s0_ref[0, h].T

    row = lax.broadcasted_iota(jnp.int32, (c, c), 0)
    col = lax.broadcasted_iota(jnp.int32, (c, c), 1)
    tril = (col <= row).astype(F32)
    srow = lax.broadcasted_iota(jnp.int32, (c, HEAD_DIM), 0)
    sel_r = lax.broadcasted_iota(jnp.int32, (c, c * c), 0)
    sel_c = lax.broadcasted_iota(jnp.int32, (c, c * c), 1)
    sel_bd = jnp.where((sel_c >= sel_r * c) & (sel_c < sel_r * c + c), 1.0, 0.0).astype(BF16)
    ones = jnp.ones((HEAD_DIM, HEAD_DIM), BF16)
    llb = llb_ref[...]
    l1m = l1m_ref[...]
    oml = oml_ref[...]

    def chunk(ci, carry):
        r0 = pl.multiple_of(ci * c, c)
        uf = uf_ref[pl.ds(r0, c), :]
        b = l1m + _log_sigmoid(uf)
        lf = jnp.maximum(llb, b) + jnp.log1p(jnp.exp(-jnp.abs(llb - b)))
        kk = oml * jax.nn.sigmoid(-uf)
        if t_valid < t_total:
            valid = (ti * tt + r0 + lax.broadcasted_iota(jnp.int32, (c, HG_WIDTH), 0)) < t_valid
            lf = jnp.where(valid, lf, 0.0)
            kk = jnp.where(valid, kk, 0.0)
        cum = jnp.dot(tril, lf, preferred_element_type=F32, precision=_HI)
        uq = uq_ref[pl.ds(r0, c), :]
        ui = ui_ref[pl.ds(r0, c), :]
        for h in range(HG_HEADS):
            hs = slice(h * HEAD_DIM, (h + 1) * HEAD_DIM)
            q, k, v, cm = uq[:, hs], kk[:, hs], ui[:, hs], cum[:, hs]
            st = st_ref[h]
            o_inter = _dot_nt((q * jnp.exp(cm)).astype(BF16), st.astype(BF16))
            ws = []
            for t in range(c):
                dec = jnp.exp(jnp.minimum(cm[t:t + 1, :] - cm, 0.0))
                ws.append(jnp.where(srow <= t, (q[t:t + 1, :] * k) * dec, 0.0))
            w = jnp.concatenate(ws, axis=0).astype(BF16)
            r = _dot(w, ones)
            vt = jnp.concatenate([v] * c, axis=0)
            o_intra = _dot(sel_bd, (r * vt).astype(BF16))
            oacc_ref[pl.ds(r0, c), hs] = o_inter + o_intra
            last = cm[c - 1:c, :]
            kd = (k * jnp.exp(last - cm)).astype(BF16)
            st_ref[h] = st * jnp.exp(last) + _dot_tn(v.astype(BF16), kd)
        return carry

    lax.fori_loop(0, tt // c, chunk, 0)

    for h in range(HG_HEADS):
        hs = slice(h * HEAD_DIM, (h + 1) * HEAD_DIM)
        g = ug_ref[:, hs]
        y = _rms(oacc_ref[:, hs], norm_ref[:, hs]) * (g * jax.nn.sigmoid(g))
        o_ref[:, hs] = y.astype(o_ref.dtype)

    @pl.when(ti == nt - 1)
    def _():
        for h in range(HG_HEADS):
            sout_ref[0, h] = st_ref[h].T


def _hgrn(u, llb, l1m, oml, norm, s0, l, n_seq, t_total, t_valid):
    tt = min(t_total, 512)
    nt = t_total // tt
    cb = HG_WIDTH
    ublock = lambda k: pl.BlockSpec((tt, cb), lambda b, t: (b * nt + t, k))
    par = pl.BlockSpec((None, 1, cb), lambda b, t: (l, 0, 0))
    sblock = pl.BlockSpec((1, HG_HEADS, HEAD_DIM, HEAD_DIM), lambda b, t: (b, 0, 0, 0))
    return pl.pallas_call(
        functools.partial(_hgrn_kernel, tt=tt, t_valid=t_valid, t_total=t_total),
        out_shape=(jax.ShapeDtypeStruct((n_seq * t_total, cb), BF16),
                   jax.ShapeDtypeStruct((n_seq, HG_HEADS, HEAD_DIM, HEAD_DIM), F32)),
        grid=(n_seq, nt),
        in_specs=[ublock(OFF_HQ // cb), ublock(OFF_HF // cb), ublock(OFF_HI // cb), ublock(OFF_HG // cb),
                  par, par, par, par, sblock],
        out_specs=(pl.BlockSpec((tt, cb), lambda b, t: (b * nt + t, 0)), sblock),
        scratch_shapes=[pltpu.VMEM((HG_HEADS, HEAD_DIM, HEAD_DIM), F32), pltpu.VMEM((tt, cb), F32)],
        compiler_params=_cparams(("parallel", "arbitrary")),
        name="hgrn",
    )(u, u, u, u, llb, l1m, oml, norm, s0)


def _mlstm_kernel(qk_ref, v_ref, og_ref, gate_ref, cw_ref, cb_ref, gb_ref, norm_ref, c0_ref, n0_ref, m0_ref,
                  conv0_ref, o_ref, cout_ref, nout_ref, mout_ref, c_ref, n_ref, m_ref, carry_ref,
                  *, ll, t_valid, t_total):
    ti = pl.program_id(1)
    nt = pl.num_programs(1)

    @pl.when(ti == 0)
    def _():
        c_ref[...] = c0_ref[0]
        n_ref[...] = n0_ref[0]
        m_ref[...] = m0_ref[0]
        carry_ref[...] = conv0_ref[0]

    x = qk_ref[...]
    xe = jnp.concatenate([carry_ref[...], x], axis=0)
    cw = cw_ref[...]
    y = cb_ref[...] + cw[3:4, :] * x
    for j in range(1, CONV_W):
        y += cw[3 - j:4 - j, :] * xe[SUBLANES - j:SUBLANES - j + ll, :]
    carry_ref[...] = x[ll - SUBLANES:, :]
    qk = y * jax.nn.sigmoid(y)

    g = gate_ref[...] + gb_ref[...]
    lf = _log_sigmoid(g)
    ipre = g
    if t_valid < t_total:
        valid = (ti * ll + lax.broadcasted_iota(jnp.int32, (ll, LANES), 0)) < t_valid
        lf = jnp.where(valid, lf, 0.0)
        ipre = jnp.where(valid, ipre, NEG_BIG)
    row = lax.broadcasted_iota(jnp.int32, (ll, ll), 0)
    col = lax.broadcasted_iota(jnp.int32, (ll, ll), 1)
    causal = col <= row
    cum = jnp.dot(causal.astype(F32), lf, preferred_element_type=F32, precision=_HI)
    lane = lax.broadcasted_iota(jnp.int32, (ll, LANES), 1)
    a_t = jnp.where(lane < ML_HEADS, ipre, cum).T

    for h in range(ML_HEADS):
        hs = slice(h * HEAD_DIM, (h + 1) * HEAD_DIM)
        q = qk[:, hs]
        k = qk[:, ML_WIDTH + h * HEAD_DIM:ML_WIDTH + (h + 1) * HEAD_DIM] * HEAD_DIM ** -0.5
        v = v_ref[:, hs]
        col_cum = cum[:, ML_HEADS + h:ML_HEADS + h + 1]
        col_i = ipre[:, h:h + 1]
        row_cum = a_t[ML_HEADS + h:ML_HEADS + h + 1, :]
        row_i = a_t[h:h + 1, :]
        m_prev = m_ref[h:h + 1, 0:1]
        log_d = jnp.where(causal, col_cum - row_cum + row_i, -jnp.inf)
        m_inter = col_cum + m_prev
        m_t = jnp.maximum(m_inter, jnp.max(log_d, axis=-1, keepdims=True))
        w_inter = jnp.exp(m_inter - m_t)
        qb, kb, vb = q.astype(BF16), k.astype(BF16), v.astype(BF16)
        s = _dot_nt(qb, kb) * jnp.exp(log_d - m_t)
        c_old = c_ref[h]
        n_old = n_ref[h:h + 1, :]
        num = w_inter * _dot(qb, c_old.astype(BF16)) + _dot(s.astype(BF16), vb)
        den = w_inter * jnp.sum(q * n_old, axis=-1, keepdims=True) + jnp.sum(s, axis=-1, keepdims=True)
        hh = num / jnp.maximum(jnp.abs(den), jnp.exp(-m_t))
        m_new = m_t[ll - 1:ll, :]
        cum_last = col_cum[ll - 1:ll, :]
        w_k = jnp.exp(cum_last - col_cum + col_i - m_new)
        decay = jnp.exp(cum_last + m_prev - m_new)
        kw = k * w_k
        c_ref[h] = decay * c_old + _dot_tn(kw.astype(BF16), vb)
        n_ref[h:h + 1, :] = decay * n_old + jnp.sum(kw, axis=0, keepdims=True)
        m_ref[h:h + 1, :] = jnp.broadcast_to(m_new, (1, LANES))
        og = og_ref[:, hs]
        o_ref[:, hs] = (_rms(hh, norm_ref[:, hs]) * jax.nn.sigmoid(og)).astype(o_ref.dtype)

    @pl.when(ti == nt - 1)
    def _():
        cout_ref[0] = c_ref[...]
        nout_ref[0] = n_ref[...]
        mout_ref[0] = m_ref[...]


def _mlstm(u, gates, cw, cb, gb, norm, c0, n0, m0, conv0, l, n_seq, t_total, t_valid, ll):
    nt = t_total // ll
    ublock = lambda w, k: pl.BlockSpec((ll, w), lambda b, t: (b * nt + t, k))
    par = lambda r, w: pl.BlockSpec((None, r, w), lambda b, t: (l, 0, 0))
    cblock = pl.BlockSpec((1, ML_HEADS, HEAD_DIM, HEAD_DIM), lambda b, t: (b, 0, 0, 0))
    vblock = pl.BlockSpec((1, SUBLANES, LANES), lambda b, t: (b, 0, 0))
    return pl.pallas_call(
        functools.partial(_mlstm_kernel, ll=ll, t_valid=t_valid, t_total=t_total),
        out_shape=(jax.ShapeDtypeStruct((n_seq * t_total, ML_WIDTH), BF16),
                   jax.ShapeDtypeStruct((n_seq, ML_HEADS, HEAD_DIM, HEAD_DIM), F32),
                   jax.ShapeDtypeStruct((n_seq, SUBLANES, LANES), F32),
                   jax.ShapeDtypeStruct((n_seq, SUBLANES, LANES), F32)),
        grid=(n_seq, nt),
        in_specs=[ublock(2 * ML_WIDTH, OFF_MQK // (2 * ML_WIDTH)), ublock(ML_WIDTH, OFF_MV // ML_WIDTH),
                  ublock(ML_WIDTH, OFF_MO // ML_WIDTH),
                  pl.BlockSpec((ll, LANES), lambda b, t: (b * nt + t, 0)),
                  par(CONV_W, 2 * ML_WIDTH), par(1, 2 * ML_WIDTH), par(1, LANES), par(1, ML_WIDTH),
                  cblock, vblock, vblock,
                  pl.BlockSpec((1, SUBLANES, 2 * ML_WIDTH), lambda b, t: (b, 0, 0))],
        out_specs=(pl.BlockSpec((ll, ML_WIDTH), lambda b, t: (b * nt + t, 0)), cblock, vblock, vblock),
        scratch_shapes=[pltpu.VMEM((ML_HEADS, HEAD_DIM, HEAD_DIM), F32), pltpu.VMEM((SUBLANES, LANES), F32),
                        pltpu.VMEM((SUBLANES, LANES), F32), pltpu.VMEM((SUBLANES, 2 * ML_WIDTH), F32)],
        compiler_params=_cparams(("parallel", "arbitrary")),
        name="mlstm",
    )(u, u, u, gates, cw, cb, gb, norm, c0, n0, m0, conv0)


def _t5_bucket_np(rel):
    rel = np.asarray(rel, np.int64)
    max_exact = N_BUCKETS // 2
    relf = np.maximum(rel, 1).astype(np.float64)
    large = max_exact + (np.log(relf / max_exact) / math.log(MAX_DISTANCE / max_exact)
                         * (N_BUCKETS - max_exact)).astype(np.int64)
    return np.where(rel < max_exact, rel, np.minimum(large, N_BUCKETS - 1)).astype(np.int32)


def _t5_thresholds(max_rel):
    buckets = _t5_bucket_np(np.arange(max_rel + 1))
    out = []
    for b in range(1, N_BUCKETS):
        hit = np.nonzero(buckets >= b)[0]
        out.append(int(hit[0]) if hit.size else None)
    return out


def _t5_bias_kernel(tab_ref, o_ref, *, rel_fn, thresholds):
    rel = rel_fn(o_ref.shape[1:])
    acc = [jnp.full(o_ref.shape[1:], tab_ref[h, 0], F32) for h in range(ATT_HEADS)]
    for b, th in enumerate(thresholds, start=1):
        if th is None:
            continue
        reached = rel >= th
        for h in range(ATT_HEADS):
            acc[h] = jnp.where(reached, tab_ref[h, b], acc[h])
    for h in range(ATT_HEADS):
        o_ref[h] = acc[h]


def _t5_bias_prompt(bias_tab, t_total):
    blk = MOBA_BLOCK
    nb = t_total // blk

    def rel_fn(shape):
        return (pl.program_id(0) * blk + lax.broadcasted_iota(jnp.int32, shape, 1)
                - lax.broadcasted_iota(jnp.int32, shape, 0))

    return pl.pallas_call(
        functools.partial(_t5_bias_kernel, rel_fn=rel_fn, thresholds=_t5_thresholds(t_total)),
        out_shape=jax.ShapeDtypeStruct((ATT_HEADS, blk, nb * blk), F32),
        grid=(nb,),
        in_specs=[pl.BlockSpec(memory_space=pltpu.SMEM)],
        out_specs=pl.BlockSpec((ATT_HEADS, blk, blk), lambda d: (0, 0, d)),
        compiler_params=_cparams(("parallel",)),
        name="t5_bias_prompt",
    )(bias_tab)


def _t5_bias_sample(bias_tab, past_len):
    n_pages = past_len // PAGE_SIZE

    def rel_fn(shape):
        return past_len - (lax.broadcasted_iota(jnp.int32, shape, 0) * PAGE_SIZE
                           + lax.broadcasted_iota(jnp.int32, shape, 1))

    return pl.pallas_call(
        functools.partial(_t5_bias_kernel, rel_fn=rel_fn, thresholds=_t5_thresholds(past_len)),
        out_shape=jax.ShapeDtypeStruct((ATT_HEADS, n_pages, PAGE_SIZE), F32),
        grid=(1,),
        in_specs=[pl.BlockSpec(memory_space=pltpu.SMEM)],
        out_specs=pl.BlockSpec((ATT_HEADS, n_pages, PAGE_SIZE), lambda i: (0, 0, 0)),
        compiler_params=_cparams(("arbitrary",)),
        name="t5_bias_sample",
    )(bias_tab)


def _moba_prefill_kernel(q_ref, k_ref, v_ref, bias_ref, o_ref, vt_ref, m_ref, l_ref, acc_ref, *, nb):
    blk = MOBA_BLOCK
    t_total = nb * blk
    scale = HEAD_DIM ** -0.5
    q = q_ref[...]
    k = k_ref[...]
    qb = q.astype(BF16)
    kb = k.astype(BF16)
    vt_ref[...] = v_ref[...].T.astype(BF16)

    kmean = jnp.concatenate(
        [jnp.mean(k[n * blk:(n + 1) * blk, :], axis=0, keepdims=True) for n in range(nb)]
        + [jnp.zeros((SUBLANES - nb % SUBLANES, HEAD_DIM), F32)] * (nb % SUBLANES != 0), axis=0)
    nrow = kmean.shape[0]
    gate = _dot_nt(kmean, q, precision=_HI)
    brow = lax.broadcasted_iota(jnp.int32, (nrow, t_total), 0)
    qblk = lax.broadcasted_iota(jnp.int32, (nrow, t_total), 1) // blk
    qblk_row = qblk[0:1, :]
    chosen = []
    for n in range(nb - 1):
        gn = gate[n:n + 1, :]
        ahead = jnp.where(brow < n, jnp.where(gate >= gn, 1.0, 0.0), jnp.where(gate > gn, 1.0, 0.0))
        rank = jnp.sum(jnp.where(brow < qblk, ahead, 0.0), axis=0, keepdims=True)
        chosen.append(jnp.where(qblk_row > n, jnp.where(rank < MOBA_TOPK, 1.0, 0.0), 0.0))

    krow = lax.broadcasted_iota(jnp.int32, (blk, blk), 0)
    qcol = lax.broadcasted_iota(jnp.int32, (blk, blk), 1)
    for i in range(nb):
        rs = slice(i * blk, (i + 1) * blk)
        lg = _dot_nt(kb[rs], qb[rs]) * scale + bias_ref[:, 0:blk]
        lg = jnp.where(krow <= qcol, lg, -jnp.inf)
        m0 = jnp.max(lg, axis=0, keepdims=True)
        p = jnp.exp(lg - m0)
        m_ref[:, rs] = m0
        l_ref[:, rs] = jnp.sum(p, axis=0, keepdims=True)
        acc_ref[:, rs] = _dot(vt_ref[:, rs], p.astype(BF16))

    for n in range(nb - 1):
        ks = slice(n * blk, (n + 1) * blk)
        qs = slice((n + 1) * blk, t_total)
        nq = t_total - (n + 1) * blk
        lg = _dot_nt(kb[ks], qb[qs]) * scale + bias_ref[:, blk:blk + nq]
        lg = jnp.where(chosen[n][:, qs] > 0.0, lg, -jnp.inf)
        m_old = m_ref[:, qs]
        m_new = jnp.maximum(m_old, jnp.max(lg, axis=0, keepdims=True))
        alpha = jnp.exp(m_old - m_new)
        p = jnp.exp(lg - m_new)
        m_ref[:, qs] = m_new
        l_ref[:, qs] = alpha * l_ref[:, qs] + jnp.sum(p, axis=0, keepdims=True)
        acc_ref[:, qs] = alpha * acc_ref[:, qs] + _dot(vt_ref[:, ks], p.astype(BF16))

    o_ref[...] = (acc_ref[...] / l_ref[...]).T.astype(o_ref.dtype)


def _moba_prefill(u, bias, n_seq, t_total):
    blk = MOBA_BLOCK
    nb = t_total // blk
    hd = HEAD_DIM
    tok = lambda off: pl.BlockSpec((t_total, hd), lambda h, b: (b, off // hd + h))
    return pl.pallas_call(
        functools.partial(_moba_prefill_kernel, nb=nb),
        out_shape=jax.ShapeDtypeStruct((n_seq * t_total, ATT_WIDTH), BF16),
        grid=(ATT_HEADS, n_seq),
        in_specs=[tok(OFF_AQ), tok(OFF_AK), tok(OFF_AV),
                  pl.BlockSpec((None, blk, nb * blk), lambda h, b: (h, 0, 0))],
        out_specs=pl.BlockSpec((t_total, hd), lambda h, b: (b, h)),
        scratch_shapes=[pltpu.VMEM((hd, t_total), BF16), pltpu.VMEM((1, t_total), F32),
                        pltpu.VMEM((1, t_total), F32), pltpu.VMEM((hd, t_total), F32)],
        compiler_params=_cparams(("parallel", "parallel")),
        name="moba_prefill",
    )(u, u, u, bias)


KMEAN_GROUP = 4


def _kmean_kernel(pt_ref, *refs):
    k_refs, o_ref = refs[:-1], refs[-1]
    for g in range(KMEAN_GROUP):
        s = jnp.sum(k_refs[2 * g][...], axis=0) + jnp.sum(k_refs[2 * g + 1][...], axis=0)
        o_ref[0, g] = s * (1.0 / MOBA_BLOCK)


def _moba_kmean(cache_k, page_table, l):
    n_seq, n_pages = page_table.shape
    nblk = n_pages // PAGES_PER_BLOCK
    grp = KMEAN_GROUP if nblk % KMEAN_GROUP == 0 else 1
    assert grp == KMEAN_GROUP

    def page(j):
        return pl.BlockSpec((None, None, PAGE_SIZE, ATT_HEADS, HEAD_DIM),
                            lambda b, n, pt: (l, pt[b, n * (grp * PAGES_PER_BLOCK) + j], 0, 0, 0))

    n_in = grp * PAGES_PER_BLOCK
    return pl.pallas_call(
        _kmean_kernel,
        out_shape=jax.ShapeDtypeStruct((n_seq, nblk, ATT_HEADS, HEAD_DIM), F32),
        grid_spec=pltpu.PrefetchScalarGridSpec(
            num_scalar_prefetch=1, grid=(n_seq, nblk // grp), in_specs=[page(j) for j in range(n_in)],
            out_specs=pl.BlockSpec((1, grp, ATT_HEADS, HEAD_DIM), lambda b, n, pt: (b, n, 0, 0))),
        compiler_params=_cparams(("parallel", "arbitrary")),
        name="moba_kmean",
    )(page_table, *([cache_k] * n_in))


def _moba_select_kernel(km_ref, q_ref, o_ref, *, nblk):
    g = jnp.sum(km_ref[0] * q_ref[0][None], axis=-1)
    idx = lax.broadcasted_iota(jnp.int32, (nblk, ATT_HEADS), 0).astype(F32)
    o_ref[...] = jnp.zeros_like(o_ref)
    for j in range(MOBA_TOPK):
        mx = jnp.max(g, axis=0, keepdims=True)
        first = jnp.min(jnp.where(g == mx, idx, float(nblk)), axis=0, keepdims=True)
        o_ref[0, j:j + 1, :] = first.astype(jnp.int32)
        g = jnp.where(idx == first, -jnp.inf, g)


def _moba_select(kmean, q3):
    n_seq, nblk = kmean.shape[:2]
    return pl.pallas_call(
        functools.partial(_moba_select_kernel, nblk=nblk),
        out_shape=jax.ShapeDtypeStruct((n_seq, SUBLANES, ATT_HEADS), jnp.int32),
        grid=(n_seq,),
        in_specs=[pl.BlockSpec((1, nblk, ATT_HEADS, HEAD_DIM), lambda b: (b, 0, 0, 0)),
                  pl.BlockSpec((1, ATT_HEADS, HEAD_DIM), lambda b: (b, 0, 0))],
        out_specs=pl.BlockSpec((1, SUBLANES, ATT_HEADS), lambda b: (b, 0, 0)),
        compiler_params=_cparams(("parallel",)),
        name="moba_select",
    )(kmean, q3)


def _moba_decode_kernel(phys_ref, logi_ref, q_ref, kn_ref, vn_ref, bias_ref, bias0_ref, ck_ref, cv_ref, o_ref,
                        kbuf, vbuf, sem, *, l):
    b = pl.program_id(0)
    scale = HEAD_DIM ** -0.5

    def page_copy(which, h, j):
        slot = h * N_SEL_PAGES + j
        pg = phys_ref[(b * ATT_HEADS + h) * N_SEL_PAGES + j]
        src, dst = ((ck_ref, kbuf), (cv_ref, vbuf))[which]
        return pltpu.make_async_copy(src.at[l, pg, :, h, :], dst.at[slot], sem.at[which, slot])

    for h in range(ATT_HEADS):
        for j in range(N_SEL_PAGES):
            page_copy(0, h, j).start()
            page_copy(1, h, j).start()

    for h in range(ATT_HEADS):
        q = q_ref[0, h:h + 1, :]
        q8 = jnp.broadcast_to(q, (SUBLANES, HEAD_DIM)).astype(BF16)
        logits = []
        for j in range(N_SEL_PAGES):
            page_copy(0, h, j).wait()
            lp = logi_ref[(b * ATT_HEADS + h) * N_SEL_PAGES + j]
            logits.append(_dot_nt(q8, kbuf[h * N_SEL_PAGES + j].astype(BF16)) * scale
                          + bias_ref[h, pl.ds(lp, 1), :])
        self_logit = jnp.sum(q * kn_ref[0, h:h + 1, :], axis=-1, keepdims=True) * scale + bias0_ref[h:h + 1, 0:1]
        m = self_logit
        for lg in logits:
            m = jnp.maximum(m, jnp.max(lg, axis=-1, keepdims=True))
        p_self = jnp.exp(self_logit - m)
        den = p_self
        acc = p_self * vn_ref[0, h:h + 1, :]
        for j in range(N_SEL_PAGES):
            page_copy(1, h, j).wait()
            p = jnp.exp(logits[j] - m)
            den += jnp.sum(p, axis=-1, keepdims=True)
            acc += _dot(p.astype(BF16), vbuf[h * N_SEL_PAGES + j].astype(BF16))
        o_ref[0, h:h + 1, :] = (acc / den)[0:1, :]


def _moba_decode(cache_k, cache_v, phys, logi, q3, k3, v3, bias_s, bias0, l):
    n_seq = q3.shape[0]
    n_slots = ATT_HEADS * N_SEL_PAGES
    tok = pl.BlockSpec((1, ATT_HEADS, HEAD_DIM), lambda b, ph, lg: (b, 0, 0))
    return pl.pallas_call(
        functools.partial(_moba_decode_kernel, l=l),
        out_shape=jax.ShapeDtypeStruct((n_seq, ATT_HEADS, HEAD_DIM), F32),
        grid_spec=pltpu.PrefetchScalarGridSpec(
            num_scalar_prefetch=2, grid=(n_seq,),
            in_specs=[tok, tok, tok,
                      pl.BlockSpec(bias_s.shape, lambda b, ph, lg: (0, 0, 0)),
                      pl.BlockSpec(bias0.shape, lambda b, ph, lg: (0, 0)),
                      pl.BlockSpec(memory_space=pl.ANY), pl.BlockSpec(memory_space=pl.ANY)],
            out_specs=tok,
            scratch_shapes=[pltpu.VMEM((n_slots, PAGE_SIZE, HEAD_DIM), F32),
                            pltpu.VMEM((n_slots, PAGE_SIZE, HEAD_DIM), F32),
                            pltpu.SemaphoreType.DMA((2, n_slots))]),
        compiler_params=_cparams(("arbitrary",)),
        name="moba_decode",
    )(phys, logi, q3, k3, v3, bias_s, bias0, cache_k, cache_v)


def _pad_rows(a, rows):
    n, w = a.shape
    return jnp.pad(a[:, None, :], ((0, 0), (0, rows - 1), (0, 0))).reshape(n * rows, w)


def _vec_state(a):
    if a.ndim == 2:
        a = jnp.broadcast_to(a[:, :, None], a.shape + (LANES,))
    return jnp.pad(a, ((0, 0), (0, SUBLANES - a.shape[1]), (0, 0)))


def kernel(x_prompt, x_sample, cache_k, cache_v, page_table, state_hgrn, state_mlstm_c, state_mlstm_n,
           state_mlstm_m, state_mlstm_conv, ln_ffn1, w_ffn1_gate, w_ffn1_up, w_ffn1_down, ln_mix, w_in, w_out,
           hgrn_lb_logits, hgrn_out_norm, rel_bias, mlstm_conv_w, mlstm_conv_b, mlstm_gate_bias,
           mlstm_out_norm, ln_ffn2, w_ffn2_gate, w_ffn2_up, w_ffn2_down, ln_final):
    depth = w_in.shape[0]
    bp, tp, d = x_prompt.shape
    bs, ts, _ = x_sample.shape
    assert ts == 1 and tp % MOBA_BLOCK == 0
    n_pages = page_table.shape[1]
    past_len = n_pages * PAGE_SIZE
    assert past_len % MOBA_BLOCK == 0 and past_len // MOBA_BLOCK >= MOBA_TOPK

    bf = lambda w: w.astype(BF16)
    wg1, wu1, wd1 = bf(w_ffn1_gate), bf(w_ffn1_up), bf(w_ffn1_down)
    wg2, wu2, wd2 = bf(w_ffn2_gate), bf(w_ffn2_up), bf(w_ffn2_down)
    w_in_b = bf(w_in)
    w_gate_b = bf(jnp.pad(w_in[:, :, N_MAIN:], ((0, 0), (0, 0), (0, LANES - N_GATE))))
    w_out_b = bf(w_out)
    row3 = lambda a: a.reshape(depth, 1, -1)
    ln1, lnm, ln2 = row3(ln_ffn1), row3(ln_mix), row3(ln_ffn2)
    lb = jnp.cumsum(jax.nn.softmax(hgrn_lb_logits.astype(F32), axis=0), axis=0)
    lb = lb - lb[0]
    llb, l1m, oml = row3(jnp.log(lb)), row3(jnp.log1p(-lb)), row3(1.0 - lb)
    hnorm, mnorm = row3(hgrn_out_norm), row3(mlstm_out_norm)
    conv_b = row3(mlstm_conv_b)
    gate_b = row3(jnp.pad(mlstm_gate_bias, ((0, 0), (0, LANES - N_GATE))))
    bias_tab = rel_bias.T.astype(F32)
    bias_p = _t5_bias_prompt(bias_tab, tp)
    bias_s = _t5_bias_sample(bias_tab, past_len)
    bias0 = jnp.broadcast_to(bias_tab[:, 0:1], (ATT_HEADS, LANES))

    zeros = lambda *s: jnp.zeros(s, F32)
    tm_p = 512 if (bp * tp) % 512 == 0 else bp * tp
    ll_p = 256

    xp = x_prompt.reshape(bp * tp, d)
    xs = x_sample.reshape(bs, d)
    outs = {k: [] for k in ("kp", "vp", "ks", "vs", "hgp", "hgs", "cp", "cs", "np", "ns", "mp", "ms", "cvp", "cvs")}

    for l in range(depth):
        xp = _ffn(xp, ln1, wg1, wu1, wd1, l, tm_p)
        u, gates = _inproj(xp, lnm, w_in_b, w_gate_b, l, tm_p)
        o_hg, hg_s = _hgrn(u, llb, l1m, oml, hnorm, zeros(bp, HG_HEADS, HEAD_DIM, HEAD_DIM), l, bp, tp, tp)
        o_att = _moba_prefill(u, bias_p, bp, tp)
        o_ml, c_s, n_s, m_s = _mlstm(u, gates, mlstm_conv_w, conv_b, gate_b, mnorm,
                                     zeros(bp, ML_HEADS, HEAD_DIM, HEAD_DIM), zeros(bp, SUBLANES, LANES),
                                     zeros(bp, SUBLANES, LANES), zeros(bp, SUBLANES, 2 * ML_WIDTH),
                                     l, bp, tp, tp, ll_p)
        xp = _outproj(xp, o_hg, o_att, o_ml, w_out_b, l, tm_p)
        xp = _ffn(xp, ln2, wg2, wu2, wd2, l, tm_p)
        u3 = u.reshape(bp, tp, N_MAIN)
        outs["kp"].append(u3[:, :, OFF_AK:OFF_AV].reshape(bp, tp, ATT_HEADS, HEAD_DIM))
        outs["vp"].append(u3[:, :, OFF_AV:OFF_MQK].reshape(bp, tp, ATT_HEADS, HEAD_DIM))
        outs["hgp"].append(hg_s)
        outs["cp"].append(c_s)
        outs["np"].append(n_s[:, :ML_HEADS])
        outs["mp"].append(m_s[:, :ML_HEADS, 0])
        outs["cvp"].append(u3[:, tp - (CONV_W - 1):, OFF_MQK:OFF_MV])

        xs = _ffn(xs, ln1, wg1, wu1, wd1, l, bs)
        us, gs = _inproj(xs, lnm, w_in_b, w_gate_b, l, bs)
        o_hg_s, hg_ss = _hgrn(_pad_rows(us, HG_CHUNK), llb, l1m, oml, hnorm, state_hgrn[l], l, bs, HG_CHUNK, 1)
        conv0 = jnp.pad(state_mlstm_conv[l], ((0, 0), (SUBLANES - (CONV_W - 1), 0), (0, 0)))
        o_ml_s, c_ss, n_ss, m_ss = _mlstm(_pad_rows(us, SAMPLE_PAD), _pad_rows(gs, SAMPLE_PAD), mlstm_conv_w,
                                          conv_b, gate_b, mnorm, state_mlstm_c[l],
                                          _vec_state(state_mlstm_n[l]), _vec_state(state_mlstm_m[l]), conv0,
                                          l, bs, SAMPLE_PAD, 1, SAMPLE_PAD)
        heads = lambda a: a.reshape(bs, ATT_HEADS, HEAD_DIM)
        q3, k3, v3 = heads(us[:, OFF_AQ:OFF_AK]), heads(us[:, OFF_AK:OFF_AV]), heads(us[:, OFF_AV:OFF_MQK])
        kmean = _moba_kmean(cache_k, page_table, l)
        sel = _moba_select(kmean, q3)[:, :MOBA_TOPK, :]
        sel = jnp.transpose(sel, (0, 2, 1))
        logi = (sel[..., None] * PAGES_PER_BLOCK
                + jnp.arange(PAGES_PER_BLOCK, dtype=jnp.int32)).reshape(bs, ATT_HEADS, N_SEL_PAGES)
        phys = jnp.take_along_axis(page_table[:, None, :], logi, axis=2)
        o_att_s = _moba_decode(cache_k, cache_v, phys.reshape(-1), logi.reshape(-1), q3, k3, v3,
                               bias_s, bias0, l).reshape(bs, ATT_WIDTH).astype(BF16)
        first = lambda a, rows: a.reshape(bs, rows, -1)[:, 0]
        xs = _outproj(xs, first(o_hg_s, HG_CHUNK), o_att_s, first(o_ml_s, SAMPLE_PAD), w_out_b, l, bs)
        xs = _ffn(xs, ln2, wg2, wu2, wd2, l, bs)
        outs["ks"].append(k3.reshape(bs, 1, ATT_HEADS, HEAD_DIM))
        outs["vs"].append(v3.reshape(bs, 1, ATT_HEADS, HEAD_DIM))
        outs["hgs"].append(hg_ss)
        outs["cs"].append(c_ss)
        outs["ns"].append(n_ss[:, :ML_HEADS])
        outs["ms"].append(m_ss[:, :ML_HEADS, 0])
        outs["cvs"].append(jnp.concatenate([state_mlstm_conv[l][:, 1:], us[:, None, OFF_MQK:OFF_MV]], axis=1))

    y_prompt = _final_norm(xp, ln_final.reshape(1, d), tm_p).reshape(bp, tp, d)
    y_sample = _final_norm(xs, ln_final.reshape(1, d), bs).reshape(bs, 1, d)
    st = {k: jnp.stack(v) for k, v in outs.items()}
    return (y_prompt, y_sample, st["kp"], st["vp"], st["ks"], st["vs"], st["hgp"], st["hgs"], st["cp"], st["cs"],
            st["np"], st["ns"], st["mp"], st["ms"], st["cvp"], st["cvs"])
```

```python
import functools
import math

import jax
import jax.numpy as jnp
import numpy as np
from jax import lax
from jax.experimental import pallas as pl
from jax.experimental.pallas import tpu as pltpu

F32 = jnp.float32
BF16 = jnp.bfloat16

HEAD_DIM = 128
HG_HEADS = 4
ATT_HEADS = 8
ML_HEADS = 4
HG_WIDTH = HG_HEADS * HEAD_DIM
ATT_WIDTH = ATT_HEADS * HEAD_DIM
ML_WIDTH = ML_HEADS * HEAD_DIM
N_MAIN = 4 * HG_WIDTH + 3 * ATT_WIDTH + 4 * ML_WIDTH
N_GATE = 2 * ML_HEADS
OFF_HQ, OFF_HF, OFF_HI, OFF_HG = 0, HG_WIDTH, 2 * HG_WIDTH, 3 * HG_WIDTH
OFF_AQ = 4 * HG_WIDTH
OFF_AK = OFF_AQ + ATT_WIDTH
OFF_AV = OFF_AK + ATT_WIDTH
OFF_MQK = OFF_AV + ATT_WIDTH
OFF_MV = OFF_MQK + 2 * ML_WIDTH
OFF_MO = OFF_MV + ML_WIDTH

PAGE_SIZE = 128
MOBA_BLOCK = 256
MOBA_TOPK = 3
PAGES_PER_BLOCK = MOBA_BLOCK // PAGE_SIZE
N_SEL_PAGES = MOBA_TOPK * PAGES_PER_BLOCK
N_BUCKETS = 32
MAX_DISTANCE = 4096
HG_CHUNK = 16
CONV_W = 4
EPS = 1e-6
NEG_BIG = -1e30
LANES = 128
SUBLANES = 8
VMEM_LIMIT = 48 * 1024 * 1024
SAMPLE_PAD = 128

_HI = lax.Precision.HIGHEST


def _cparams(sem):
    return pltpu.CompilerParams(dimension_semantics=sem, vmem_limit_bytes=VMEM_LIMIT)


def _rms(x, g):
    return x * lax.rsqrt(jnp.mean(x * x, axis=-1, keepdims=True) + EPS) * g


def _log_sigmoid(x):
    return jnp.minimum(x, 0.0) - jnp.log1p(jnp.exp(-jnp.abs(x)))


def _dot(a, b):
    return jnp.dot(a, b, preferred_element_type=F32)


def _dot_nt(a, b, precision=None):
    return lax.dot_general(a, b, (((1,), (1,)), ((), ())), preferred_element_type=F32, precision=precision)


def _dot_tn(a, b):
    return lax.dot_general(a, b, (((0,), (0,)), ((), ())), preferred_element_type=F32)


def _ffn_kernel(x_ref, g_ref, wg_ref, wu_ref, wd_ref, o_ref, xn_ref, *, nj):
    j = pl.program_id(1)

    @pl.when(j == 0)
    def _():
        xn_ref[...] = _rms(x_ref[...], g_ref[...]).astype(BF16)
        o_ref[...] = jnp.zeros_like(o_ref)

    xn = xn_ref[...]
    a = _dot(xn, wg_ref[...])
    b = _dot(xn, wu_ref[...])
    h = (a * jax.nn.sigmoid(a)) * b
    o_ref[...] += _dot(h.astype(BF16), wd_ref[...])

    @pl.when(j == nj - 1)
    def _():
        o_ref[...] = x_ref[...] + 0.5 * o_ref[...]


def _ffn(x, ln, wg, wu, wd, l, tm):
    m, d = x.shape
    f = wg.shape[-1]
    tf = 512 if f % 512 == 0 else f
    nj = f // tf
    return pl.pallas_call(
        functools.partial(_ffn_kernel, nj=nj),
        out_shape=jax.ShapeDtypeStruct((m, d), F32),
        grid=(m // tm, nj),
        in_specs=[
            pl.BlockSpec((tm, d), lambda i, j: (i, 0), pipeline_mode=pl.Buffered(1)),
            pl.BlockSpec((None, 1, d), lambda i, j: (l, 0, 0)),
            pl.BlockSpec((None, d, tf), lambda i, j: (l, 0, j)),
            pl.BlockSpec((None, d, tf), lambda i, j: (l, 0, j)),
            pl.BlockSpec((None, tf, d), lambda i, j: (l, j, 0)),
        ],
        out_specs=pl.BlockSpec((tm, d), lambda i, j: (i, 0), pipeline_mode=pl.Buffered(1)),
        scratch_shapes=[pltpu.VMEM((tm, d), BF16)],
        compiler_params=_cparams(("parallel", "arbitrary")),
        name="ffn",
    )(x, ln, wg, wu, wd)


def _inproj_kernel(x_ref, g_ref, w_ref, wgate_ref, u_ref, gate_ref, xn_ref):
    @pl.when(pl.program_id(1) == 0)
    def _():
        xn = _rms(x_ref[...], g_ref[...]).astype(BF16)
        xn_ref[...] = xn
        gate_ref[...] = _dot(xn, wgate_ref[...])

    u_ref[...] = _dot(xn_ref[...], w_ref[...])


def _inproj(x, ln, w, wgate, l, tm):
    m, d = x.shape
    tn = 1024
    return pl.pallas_call(
        _inproj_kernel,
        out_shape=(jax.ShapeDtypeStruct((m, N_MAIN), F32), jax.ShapeDtypeStruct((m, LANES), F32)),
        grid=(m // tm, N_MAIN // tn),
        in_specs=[
            pl.BlockSpec((tm, d), lambda i, j: (i, 0)),
            pl.BlockSpec((None, 1, d), lambda i, j: (l, 0, 0)),
            pl.BlockSpec((None, d, tn), lambda i, j: (l, 0, j)),
            pl.BlockSpec((None, d, LANES), lambda i, j: (l, 0, 0)),
        ],
        out_specs=(pl.BlockSpec((tm, tn), lambda i, j: (i, j)), pl.BlockSpec((tm, LANES), lambda i, j: (i, 0))),
        scratch_shapes=[pltpu.VMEM((tm, d), BF16)],
        compiler_params=_cparams(("parallel", "arbitrary")),
        name="inproj",
    )(x, ln, w, wgate)


def _outproj_kernel(x_ref, a_ref, b_ref, c_ref, w_ref, o_ref):
    acc = _dot(a_ref[...], w_ref[0:HG_WIDTH, :])
    acc += _dot(b_ref[...], w_ref[HG_WIDTH:HG_WIDTH + ATT_WIDTH, :])
    acc += _dot(c_ref[...], w_ref[HG_WIDTH + ATT_WIDTH:, :])
    o_ref[...] = x_ref[...] + acc


def _outproj(x, o_hg, o_att, o_ml, w, l, tm):
    m, d = x.shape
    dm = w.shape[1]
    return pl.pallas_call(
        _outproj_kernel,
        out_shape=jax.ShapeDtypeStruct((m, d), F32),
        grid=(m // tm,),
        in_specs=[
            pl.BlockSpec((tm, d), lambda i: (i, 0)),
            pl.BlockSpec((tm, HG_WIDTH), lambda i: (i, 0)),
            pl.BlockSpec((tm, ATT_WIDTH), lambda i: (i, 0)),
            pl.BlockSpec((tm, ML_WIDTH), lambda i: (i, 0)),
            pl.BlockSpec((None, dm, d), lambda i: (l, 0, 0)),
        ],
        out_specs=pl.BlockSpec((tm, d), lambda i: (i, 0)),
        compiler_params=_cparams(("parallel",)),
        name="outproj",
    )(x, o_hg, o_att, o_ml, w)


def _final_norm_kernel(x_ref, g_ref, o_ref):
    o_ref[...] = _rms(x_ref[...], g_ref[...])


def _final_norm(x, g, tm):
    m, d = x.shape
    return pl.pallas_call(
        _final_norm_kernel,
        out_shape=jax.ShapeDtypeStruct((m, d), F32),
        grid=(m // tm,),
        in_specs=[pl.BlockSpec((tm, d), lambda i: (i, 0)), pl.BlockSpec((1, d), lambda i: (0, 0))],
        out_specs=pl.BlockSpec((tm, d), lambda i: (i, 0)),
        compiler_params=_cparams(("parallel",)),
        name="final_norm",
    )(x, g)


def _hgrn_kernel(uq_ref, uf_ref, ui_ref, ug_ref, llb_ref, l1m_ref, oml_ref, norm_ref, s0_ref,
                 o_ref, sout_ref, st_ref, oacc_ref, qd_ref, kd_ref, dl_ref, *, tt, t_valid, t_total):
    c = HG_CHUNK
    ti = pl.program_id(1)
    nt = pl.num_programs(1)

    @pl.when(ti == 0)
    def _():
        for h in range(HG_HEADS):
            st_ref[h] = s0_ref[0, h].T

    rr = min(tt, LANES)
    nc = rr // c
    row = lax.broadcasted_iota(jnp.int32, (rr, rr), 0)
    col = lax.broadcasted_iota(jnp.int32, (rr, rr), 1)
    same_chunk = (row // c) == (col // c)
    tril_bd = jnp.where(same_chunk, jnp.where(col <= row, 1.0, 0.0), 0.0)
    ones_bd = jnp.where(same_chunk, 1.0, 0.0)
    s_idx = lax.broadcasted_iota(jnp.int32, (nc, c, HEAD_DIM), 1)
    sel_r = lax.broadcasted_iota(jnp.int32, (rr, c * rr), 0)
    sel_c = lax.broadcasted_iota(jnp.int32, (rr, c * rr), 1)
    sel_big = jnp.where((sel_c // rr) == (sel_r % c),
                        jnp.where(((sel_c % rr) // c) == (sel_r // c), 1.0, 0.0), 0.0).astype(BF16)
    ones = jnp.ones((HEAD_DIM, HEAD_DIM), BF16)
    llb = llb_ref[...]
    l1m = l1m_ref[...]
    oml = oml_ref[...]

    def intra(si, carry):
        r0 = pl.multiple_of(si * rr, rr)
        uf = uf_ref[pl.ds(r0, rr), :]
        b = l1m + _log_sigmoid(uf)
        lf = jnp.maximum(llb, b) + jnp.log1p(jnp.exp(-jnp.abs(llb - b)))
        kk = oml * jax.nn.sigmoid(-uf)
        if t_valid < t_total:
            valid = (ti * tt + r0 + lax.broadcasted_iota(jnp.int32, (rr, HG_WIDTH), 0)) < t_valid
            lf = jnp.where(valid, lf, 0.0)
            kk = jnp.where(valid, kk, 0.0)
        cum = jnp.dot(tril_bd, lf, preferred_element_type=F32, precision=_HI)
        tot = jnp.dot(ones_bd, lf, preferred_element_type=F32, precision=_HI)
        uq = uq_ref[pl.ds(r0, rr), :]
        ui = ui_ref[pl.ds(r0, rr), :]
        qd_ref[pl.ds(r0, rr), :] = (uq * jnp.exp(cum)).astype(BF16)
        kd_ref[pl.ds(r0, rr), :] = (kk * jnp.exp(tot - cum)).astype(BF16)
        dl_ref[pl.ds(r0, rr), :] = jnp.exp(tot)
        for h in range(HG_HEADS):
            hs = slice(h * HEAD_DIM, (h + 1) * HEAD_DIM)
            q3 = uq[:, hs].reshape(nc, c, HEAD_DIM)
            k3 = kk[:, hs].reshape(nc, c, HEAD_DIM)
            cm3 = cum[:, hs].reshape(nc, c, HEAD_DIM)
            ws = []
            for t in range(c):
                dec = jnp.exp(jnp.broadcast_to(cm3[:, t:t + 1, :], cm3.shape) - cm3)
                w = (jnp.broadcast_to(q3[:, t:t + 1, :], q3.shape) * k3) * dec
                ws.append(jnp.where(s_idx <= t, w, 0.0).reshape(rr, HEAD_DIM))
            w_all = jnp.concatenate(ws, axis=0).astype(BF16)
            r = _dot(w_all, ones)
            vt = jnp.concatenate([ui[:, hs]] * c, axis=0)
            oacc_ref[pl.ds(r0, rr), hs] = _dot(sel_big, (r * vt).astype(BF16))
        return carry

    lax.fori_loop(0, tt // rr, intra, 0)

    def inter(ci, carry):
        r0 = pl.multiple_of(ci * c, c)
        for h in range(HG_HEADS):
            hs = slice(h * HEAD_DIM, (h + 1) * HEAD_DIM)
            st = st_ref[h]
            oacc_ref[pl.ds(r0, c), hs] += _dot_nt(qd_ref[pl.ds(r0, c), hs], st.astype(BF16))
            st_ref[h] = st * dl_ref[pl.ds(r0, 1), hs] + _dot_tn(ui_ref[pl.ds(r0, c), hs].astype(BF16),
                                                                 kd_ref[pl.ds(r0, c), hs])
        return carry

    lax.fori_loop(0, tt // c, inter, 0, unroll=min(4, tt // c))

    for h in range(HG_HEADS):
        hs = slice(h * HEAD_DIM, (h + 1) * HEAD_DIM)
        g = ug_ref[:, hs]
        y = _rms(oacc_ref[:, hs], norm_ref[:, hs]) * (g * jax.nn.sigmoid(g))
        o_ref[:, hs] = y.astype(o_ref.dtype)

    @pl.when(ti == nt - 1)
    def _():
        for h in range(HG_HEADS):
            sout_ref[0, h] = st_ref[h].T


def _hgrn(u, llb, l1m, oml, norm, s0, l, n_seq, t_total, t_valid):
    tt = min(t_total, 512)
    nt = t_total // tt
    cb = HG_WIDTH
    ublock = lambda k: pl.BlockSpec((tt, cb), lambda b, t: (b * nt + t, k))
    par = pl.BlockSpec((None, 1, cb), lambda b, t: (l, 0, 0))
    sblock = pl.BlockSpec((1, HG_HEADS, HEAD_DIM, HEAD_DIM), lambda b, t: (b, 0, 0, 0))
    return pl.pallas_call(
        functools.partial(_hgrn_kernel, tt=tt, t_valid=t_valid, t_total=t_total),
        out_shape=(jax.ShapeDtypeStruct((n_seq * t_total, cb), BF16),
                   jax.ShapeDtypeStruct((n_seq, HG_HEADS, HEAD_DIM, HEAD_DIM), F32)),
        grid=(n_seq, nt),
        in_specs=[ublock(OFF_HQ // cb), ublock(OFF_HF // cb), ublock(OFF_HI // cb), ublock(OFF_HG // cb),
                  par, par, par, par, sblock],
        out_specs=(pl.BlockSpec((tt, cb), lambda b, t: (b * nt + t, 0)), sblock),
        scratch_shapes=[pltpu.VMEM((HG_HEADS, HEAD_DIM, HEAD_DIM), F32), pltpu.VMEM((tt, cb), F32),
                        pltpu.VMEM((tt, cb), BF16), pltpu.VMEM((tt, cb), BF16), pltpu.VMEM((tt, cb), F32)],
        compiler_params=_cparams(("parallel", "arbitrary")),
        name="hgrn",
    )(u, u, u, u, llb, l1m, oml, norm, s0)


def _mlstm_kernel(qk_ref, v_ref, og_ref, gate_ref, cw_ref, cb_ref, gb_ref, norm_ref, c0_ref, n0_ref, m0_ref,
                  conv0_ref, o_ref, cout_ref, nout_ref, mout_ref, c_ref, n_ref, m_ref, carry_ref,
                  *, ll, t_valid, t_total):
    ti = pl.program_id(1)
    nt = pl.num_programs(1)

    @pl.when(ti == 0)
    def _():
        c_ref[...] = c0_ref[0]
        n_ref[...] = n0_ref[0]
        m_ref[...] = m0_ref[0]
        carry_ref[...] = conv0_ref[0]

    x = qk_ref[...]
    xe = jnp.concatenate([carry_ref[...], x], axis=0)
    cw = cw_ref[...]
    y = cb_ref[...] + cw[3:4, :] * x
    for j in range(1, CONV_W):
        y += cw[3 - j:4 - j, :] * xe[SUBLANES - j:SUBLANES - j + ll, :]
    carry_ref[...] = x[ll - SUBLANES:, :]
    qk = y * jax.nn.sigmoid(y)

    g = gate_ref[...] + gb_ref[...]
    lf = _log_sigmoid(g)
    ipre = g
    if t_valid < t_total:
        valid = (ti * ll + lax.broadcasted_iota(jnp.int32, (ll, LANES), 0)) < t_valid
        lf = jnp.where(valid, lf, 0.0)
        ipre = jnp.where(valid, ipre, NEG_BIG)
    row = lax.broadcasted_iota(jnp.int32, (ll, ll), 0)
    col = lax.broadcasted_iota(jnp.int32, (ll, ll), 1)
    causal = col <= row
    cum = jnp.dot(causal.astype(F32), lf, preferred_element_type=F32, precision=_HI)
    lane = lax.broadcasted_iota(jnp.int32, (ll, LANES), 1)
    a_t = jnp.where(lane < ML_HEADS, ipre, cum).T

    for h in range(ML_HEADS):
        hs = slice(h * HEAD_DIM, (h + 1) * HEAD_DIM)
        q = qk[:, hs]
        k = qk[:, ML_WIDTH + h * HEAD_DIM:ML_WIDTH + (h + 1) * HEAD_DIM] * HEAD_DIM ** -0.5
        v = v_ref[:, hs]
        col_cum = cum[:, ML_HEADS + h:ML_HEADS + h + 1]
        col_i = ipre[:, h:h + 1]
        row_cum = a_t[ML_HEADS + h:ML_HEADS + h + 1, :]
        row_i = a_t[h:h + 1, :]
        m_prev = m_ref[h:h + 1, 0:1]
        log_d = jnp.where(causal, col_cum - row_cum + row_i, -jnp.inf)
        m_inter = col_cum + m_prev
        m_t = jnp.maximum(m_inter, jnp.max(log_d, axis=-1, keepdims=True))
        w_inter = jnp.exp(m_inter - m_t)
        qb, kb, vb = q.astype(BF16), k.astype(BF16), v.astype(BF16)
        s = _dot_nt(qb, kb) * jnp.exp(log_d - m_t)
        c_old = c_ref[h]
        n_old = n_ref[h:h + 1, :]
        num = w_inter * _dot(qb, c_old.astype(BF16)) + _dot(s.astype(BF16), vb)
        den = w_inter * jnp.sum(q * n_old, axis=-1, keepdims=True) + jnp.sum(s, axis=-1, keepdims=True)
        hh = num / jnp.maximum(jnp.abs(den), jnp.exp(-m_t))
        m_new = m_t[ll - 1:ll, :]
        cum_last = col_cum[ll - 1:ll, :]
        w_k = jnp.exp(cum_last - col_cum + col_i - m_new)
        decay = jnp.exp(cum_last + m_prev - m_new)
        kw = k * w_k
        c_ref[h] = decay * c_old + _dot_tn(kw.astype(BF16), vb)
        n_ref[h:h + 1, :] = decay * n_old + jnp.sum(kw, axis=0, keepdims=True)
        m_ref[h:h + 1, :] = jnp.broadcast_to(m_new, (1, LANES))
        og = og_ref[:, hs]
        o_ref[:, hs] = (_rms(hh, norm_ref[:, hs]) * jax.nn.sigmoid(og)).astype(o_ref.dtype)

    @pl.when(ti == nt - 1)
    def _():
        cout_ref[0] = c_ref[...]
        nout_ref[0] = n_ref[...]
        mout_ref[0] = m_ref[...]


def _mlstm(u, gates, cw, cb, gb, norm, c0, n0, m0, conv0, l, n_seq, t_total, t_valid, ll):
    nt = t_total // ll
    ublock = lambda w, k: pl.BlockSpec((ll, w), lambda b, t: (b * nt + t, k))
    par = lambda r, w: pl.BlockSpec((None, r, w), lambda b, t: (l, 0, 0))
    cblock = pl.BlockSpec((1, ML_HEADS, HEAD_DIM, HEAD_DIM), lambda b, t: (b, 0, 0, 0))
    vblock = pl.BlockSpec((1, SUBLANES, LANES), lambda b, t: (b, 0, 0))
    return pl.pallas_call(
        functools.partial(_mlstm_kernel, ll=ll, t_valid=t_valid, t_total=t_total),
        out_shape=(jax.ShapeDtypeStruct((n_seq * t_total, ML_WIDTH), BF16),
                   jax.ShapeDtypeStruct((n_seq, ML_HEADS, HEAD_DIM, HEAD_DIM), F32),
                   jax.ShapeDtypeStruct((n_seq, SUBLANES, LANES), F32),
                   jax.ShapeDtypeStruct((n_seq, SUBLANES, LANES), F32)),
        grid=(n_seq, nt),
        in_specs=[ublock(2 * ML_WIDTH, OFF_MQK // (2 * ML_WIDTH)), ublock(ML_WIDTH, OFF_MV // ML_WIDTH),
                  ublock(ML_WIDTH, OFF_MO // ML_WIDTH),
                  pl.BlockSpec((ll, LANES), lambda b, t: (b * nt + t, 0)),
                  par(CONV_W, 2 * ML_WIDTH), par(1, 2 * ML_WIDTH), par(1, LANES), par(1, ML_WIDTH),
                  cblock, vblock, vblock,
                  pl.BlockSpec((1, SUBLANES, 2 * ML_WIDTH), lambda b, t: (b, 0, 0))],
        out_specs=(pl.BlockSpec((ll, ML_WIDTH), lambda b, t: (b * nt + t, 0)), cblock, vblock, vblock),
        scratch_shapes=[pltpu.VMEM((ML_HEADS, HEAD_DIM, HEAD_DIM), F32), pltpu.VMEM((SUBLANES, LANES), F32),
                        pltpu.VMEM((SUBLANES, LANES), F32), pltpu.VMEM((SUBLANES, 2 * ML_WIDTH), F32)],
        compiler_params=_cparams(("parallel", "arbitrary")),
        name="mlstm",
    )(u, u, u, gates, cw, cb, gb, norm, c0, n0, m0, conv0)


def _t5_bucket_np(rel):
    rel = np.asarray(rel, np.int64)
    max_exact = N_BUCKETS // 2
    relf = np.maximum(rel, 1).astype(np.float64)
    large = max_exact + (np.log(relf / max_exact) / math.log(MAX_DISTANCE / max_exact)
                         * (N_BUCKETS - max_exact)).astype(np.int64)
    return np.where(rel < max_exact, rel, np.minimum(large, N_BUCKETS - 1)).astype(np.int32)


def _t5_thresholds(max_rel):
    buckets = _t5_bucket_np(np.arange(max_rel + 1))
    out = []
    for b in range(1, N_BUCKETS):
        hit = np.nonzero(buckets >= b)[0]
        out.append(int(hit[0]) if hit.size else None)
    return out


def _t5_bias_kernel(tab_ref, o_ref, *, rel_fn, thresholds):
    rel = rel_fn(o_ref.shape[1:])
    acc = [jnp.full(o_ref.shape[1:], tab_ref[h, 0], F32) for h in range(ATT_HEADS)]
    for b, th in enumerate(thresholds, start=1):
        if th is None:
            continue
        reached = rel >= th
        for h in range(ATT_HEADS):
            acc[h] = jnp.where(reached, tab_ref[h, b], acc[h])
    for h in range(ATT_HEADS):
        o_ref[h] = acc[h]


def _t5_bias_prompt(bias_tab, t_total):
    blk = MOBA_BLOCK
    nb = t_total // blk

    def rel_fn(shape):
        return (pl.program_id(0) * blk + lax.broadcasted_iota(jnp.int32, shape, 1)
                - lax.broadcasted_iota(jnp.int32, shape, 0))

    return pl.pallas_call(
        functools.partial(_t5_bias_kernel, rel_fn=rel_fn, thresholds=_t5_thresholds(t_total)),
        out_shape=jax.ShapeDtypeStruct((ATT_HEADS, blk, nb * blk), F32),
        grid=(nb,),
        in_specs=[pl.BlockSpec(memory_space=pltpu.SMEM)],
        out_specs=pl.BlockSpec((ATT_HEADS, blk, blk), lambda d: (0, 0, d)),
        compiler_params=_cparams(("parallel",)),
        name="t5_bias_prompt",
    )(bias_tab)


def _t5_bias_sample(bias_tab, past_len):
    n_pages = past_len // PAGE_SIZE

    def rel_fn(shape):
        return past_len - (lax.broadcasted_iota(jnp.int32, shape, 0) * PAGE_SIZE
                           + lax.broadcasted_iota(jnp.int32, shape, 1))

    return pl.pallas_call(
        functools.partial(_t5_bias_kernel, rel_fn=rel_fn, thresholds=_t5_thresholds(past_len)),
        out_shape=jax.ShapeDtypeStruct((ATT_HEADS, n_pages, PAGE_SIZE), F32),
        grid=(1,),
        in_specs=[pl.BlockSpec(memory_space=pltpu.SMEM)],
        out_specs=pl.BlockSpec((ATT_HEADS, n_pages, PAGE_SIZE), lambda i: (0, 0, 0)),
        compiler_params=_cparams(("arbitrary",)),
        name="t5_bias_sample",
    )(bias_tab)


def _moba_prefill_kernel(q_ref, k_ref, v_ref, bias_ref, o_ref, vt_ref, m_ref, l_ref, acc_ref, *, nb):
    blk = MOBA_BLOCK
    t_total = nb * blk
    scale = HEAD_DIM ** -0.5
    q = q_ref[...]
    k = k_ref[...]
    qb = q.astype(BF16)
    kb = k.astype(BF16)
    vt_ref[...] = v_ref[...].T.astype(BF16)

    kmean = jnp.concatenate(
        [jnp.mean(k[n * blk:(n + 1) * blk, :], axis=0, keepdims=True) for n in range(nb)]
        + [jnp.zeros((SUBLANES - nb % SUBLANES, HEAD_DIM), F32)] * (nb % SUBLANES != 0), axis=0)
    nrow = kmean.shape[0]
    gate = _dot_nt(kmean, q, precision=_HI)
    brow = lax.broadcasted_iota(jnp.int32, (nrow, t_total), 0)
    qblk = lax.broadcasted_iota(jnp.int32, (nrow, t_total), 1) // blk
    qblk_row = qblk[0:1, :]
    chosen = []
    for n in range(nb - 1):
        gn = gate[n:n + 1, :]
        ahead = jnp.where(brow < n, jnp.where(gate >= gn, 1.0, 0.0), jnp.where(gate > gn, 1.0, 0.0))
        rank = jnp.sum(jnp.where(brow < qblk, ahead, 0.0), axis=0, keepdims=True)
        chosen.append(jnp.where(qblk_row > n, jnp.where(rank < MOBA_TOPK, 1.0, 0.0), 0.0))

    krow = lax.broadcasted_iota(jnp.int32, (blk, blk), 0)
    qcol = lax.broadcasted_iota(jnp.int32, (blk, blk), 1)
    for i in range(nb):
        rs = slice(i * blk, (i + 1) * blk)
        lg = _dot_nt(kb[rs], qb[rs]) * scale + bias_ref[:, 0:blk]
        lg = jnp.where(krow <= qcol, lg, -jnp.inf)
        m0 = jnp.max(lg, axis=0, keepdims=True)
        p = jnp.exp(lg - m0)
        m_ref[:, rs] = m0
        l_ref[:, rs] = jnp.sum(p, axis=0, keepdims=True)
        acc_ref[:, rs] = _dot(vt_ref[:, rs], p.astype(BF16))

    for n in range(nb - 1):
        ks = slice(n * blk, (n + 1) * blk)
        qs = slice((n + 1) * blk, t_total)
        nq = t_total - (n + 1) * blk
        lg = _dot_nt(kb[ks], qb[qs]) * scale + bias_ref[:, blk:blk + nq]
        lg = jnp.where(chosen[n][:, qs] > 0.0, lg, -jnp.inf)
        m_old = m_ref[:, qs]
        m_new = jnp.maximum(m_old, jnp.max(lg, axis=0, keepdims=True))
        alpha = jnp.exp(m_old - m_new)
        p = jnp.exp(lg - m_new)
        m_ref[:, qs] = m_new
        l_ref[:, qs] = alpha * l_ref[:, qs] + jnp.sum(p, axis=0, keepdims=True)
        acc_ref[:, qs] = alpha * acc_ref[:, qs] + _dot(vt_ref[:, ks], p.astype(BF16))

    o_ref[...] = (acc_ref[...] / l_ref[...]).T.astype(o_ref.dtype)


def _moba_prefill(u, bias, n_seq, t_total):
    blk = MOBA_BLOCK
    nb = t_total // blk
    hd = HEAD_DIM
    tok = lambda off: pl.BlockSpec((t_total, hd), lambda h, b: (b, off // hd + h))
    return pl.pallas_call(
        functools.partial(_moba_prefill_kernel, nb=nb),
        out_shape=jax.ShapeDtypeStruct((n_seq * t_total, ATT_WIDTH), BF16),
        grid=(ATT_HEADS, n_seq),
        in_specs=[tok(OFF_AQ), tok(OFF_AK), tok(OFF_AV),
                  pl.BlockSpec((None, blk, nb * blk), lambda h, b: (h, 0, 0))],
        out_specs=pl.BlockSpec((t_total, hd), lambda h, b: (b, h)),
        scratch_shapes=[pltpu.VMEM((hd, t_total), BF16), pltpu.VMEM((1, t_total), F32),
                        pltpu.VMEM((1, t_total), F32), pltpu.VMEM((hd, t_total), F32)],
        compiler_params=_cparams(("parallel", "parallel")),
        name="moba_prefill",
    )(u, u, u, bias)


KMEAN_GROUP = 4


def _kmean_kernel(pt_ref, *refs):
    k_refs, o_ref = refs[:-1], refs[-1]
    for g in range(KMEAN_GROUP):
        s = jnp.sum(k_refs[2 * g][...], axis=0) + jnp.sum(k_refs[2 * g + 1][...], axis=0)
        o_ref[0, g] = s * (1.0 / MOBA_BLOCK)


def _moba_kmean(cache_k, page_table, l):
    n_seq, n_pages = page_table.shape
    nblk = n_pages // PAGES_PER_BLOCK
    grp = KMEAN_GROUP if nblk % KMEAN_GROUP == 0 else 1
    assert grp == KMEAN_GROUP

    def page(j):
        return pl.BlockSpec((None, None, PAGE_SIZE, ATT_HEADS, HEAD_DIM),
                            lambda b, n, pt: (l, pt[b, n * (grp * PAGES_PER_BLOCK) + j], 0, 0, 0))

    n_in = grp * PAGES_PER_BLOCK
    return pl.pallas_call(
        _kmean_kernel,
        out_shape=jax.ShapeDtypeStruct((n_seq, nblk, ATT_HEADS, HEAD_DIM), F32),
        grid_spec=pltpu.PrefetchScalarGridSpec(
            num_scalar_prefetch=1, grid=(n_seq, nblk // grp), in_specs=[page(j) for j in range(n_in)],
            out_specs=pl.BlockSpec((1, grp, ATT_HEADS, HEAD_DIM), lambda b, n, pt: (b, n, 0, 0))),
        compiler_params=_cparams(("parallel", "arbitrary")),
        name="moba_kmean",
    )(page_table, *([cache_k] * n_in))


def _moba_select_kernel(km_ref, q_ref, o_ref, *, nblk):
    g = jnp.sum(km_ref[0] * q_ref[0][None], axis=-1)
    idx = lax.broadcasted_iota(jnp.int32, (nblk, ATT_HEADS), 0).astype(F32)
    o_ref[...] = jnp.zeros_like(o_ref)
    for j in range(MOBA_TOPK):
        mx = jnp.max(g, axis=0, keepdims=True)
        first = jnp.min(jnp.where(g == mx, idx, float(nblk)), axis=0, keepdims=True)
        o_ref[0, j:j + 1, :] = first.astype(jnp.int32)
        g = jnp.where(idx == first, -jnp.inf, g)


def _moba_select(kmean, q3):
    n_seq, nblk = kmean.shape[:2]
    return pl.pallas_call(
        functools.partial(_moba_select_kernel, nblk=nblk),
        out_shape=jax.ShapeDtypeStruct((n_seq, SUBLANES, ATT_HEADS), jnp.int32),
        grid=(n_seq,),
        in_specs=[pl.BlockSpec((1, nblk, ATT_HEADS, HEAD_DIM), lambda b: (b, 0, 0, 0)),
                  pl.BlockSpec((1, ATT_HEADS, HEAD_DIM), lambda b: (b, 0, 0))],
        out_specs=pl.BlockSpec((1, SUBLANES, ATT_HEADS), lambda b: (b, 0, 0)),
        compiler_params=_cparams(("parallel",)),
        name="moba_select",
    )(kmean, q3)


def _moba_decode_kernel(phys_ref, logi_ref, q_ref, kn_ref, vn_ref, bias_ref, bias0_ref, ck_ref, cv_ref, o_ref,
                        kbuf, vbuf, sem, *, l):
    b = pl.program_id(0)
    scale = HEAD_DIM ** -0.5

    def page_copy(which, h, j):
        slot = h * N_SEL_PAGES + j
        pg = phys_ref[(b * ATT_HEADS + h) * N_SEL_PAGES + j]
        src, dst = ((ck_ref, kbuf), (cv_ref, vbuf))[which]
        return pltpu.make_async_copy(src.at[l, pg, :, h, :], dst.at[slot], sem.at[which, slot])

    for h in range(ATT_HEADS):
        for j in range(N_SEL_PAGES):
            page_copy(0, h, j).start()
            page_copy(1, h, j).start()

    for h in range(ATT_HEADS):
        q = q_ref[0, h:h + 1, :]
        q8 = jnp.broadcast_to(q, (SUBLANES, HEAD_DIM)).astype(BF16)
        logits = []
        for j in range(N_SEL_PAGES):
            page_copy(0, h, j).wait()
            lp = logi_ref[(b * ATT_HEADS + h) * N_SEL_PAGES + j]
            logits.append(_dot_nt(q8, kbuf[h * N_SEL_PAGES + j].astype(BF16)) * scale
                          + bias_ref[h, pl.ds(lp, 1), :])
        self_logit = jnp.sum(q * kn_ref[0, h:h + 1, :], axis=-1, keepdims=True) * scale + bias0_ref[h:h + 1, 0:1]
        m = self_logit
        for lg in logits:
            m = jnp.maximum(m, jnp.max(lg, axis=-1, keepdims=True))
        p_self = jnp.exp(self_logit - m)
        den = p_self
        acc = p_self * vn_ref[0, h:h + 1, :]
        for j in range(N_SEL_PAGES):
            page_copy(1, h, j).wait()
            p = jnp.exp(logits[j] - m)
            den += jnp.sum(p, axis=-1, keepdims=True)
            acc += _dot(p.astype(BF16), vbuf[h * N_SEL_PAGES + j].astype(BF16))
        o_ref[0, h:h + 1, :] = (acc / den)[0:1, :]


def _moba_decode(cache_k, cache_v, phys, logi, q3, k3, v3, bias_s, bias0, l):
    n_seq = q3.shape[0]
    n_slots = ATT_HEADS * N_SEL_PAGES
    tok = pl.BlockSpec((1, ATT_HEADS, HEAD_DIM), lambda b, ph, lg: (b, 0, 0))
    return pl.pallas_call(
        functools.partial(_moba_decode_kernel, l=l),
        out_shape=jax.ShapeDtypeStruct((n_seq, ATT_HEADS, HEAD_DIM), F32),
        grid_spec=pltpu.PrefetchScalarGridSpec(
            num_scalar_prefetch=2, grid=(n_seq,),
            in_specs=[tok, tok, tok,
                      pl.BlockSpec(bias_s.shape, lambda b, ph, lg: (0, 0, 0)),
                      pl.BlockSpec(bias0.shape, lambda b, ph, lg: (0, 0)),
                      pl.BlockSpec(memory_space=pl.ANY), pl.BlockSpec(memory_space=pl.ANY)],
            out_specs=tok,
            scratch_shapes=[pltpu.VMEM((n_slots, PAGE_SIZE, HEAD_DIM), F32),
                            pltpu.VMEM((n_slots, PAGE_SIZE, HEAD_DIM), F32),
                            pltpu.SemaphoreType.DMA((2, n_slots))]),
        compiler_params=_cparams(("arbitrary",)),
        name="moba_decode",
    )(phys, logi, q3, k3, v3, bias_s, bias0, cache_k, cache_v)


def _pad_rows(a, rows):
    n, w = a.shape
    return jnp.pad(a[:, None, :], ((0, 0), (0, rows - 1), (0, 0))).reshape(n * rows, w)


def _vec_state(a):
    if a.ndim == 2:
        a = jnp.broadcast_to(a[:, :, None], a.shape + (LANES,))
    return jnp.pad(a, ((0, 0), (0, SUBLANES - a.shape[1]), (0, 0)))


def kernel(x_prompt, x_sample, cache_k, cache_v, page_table, state_hgrn, state_mlstm_c, state_mlstm_n,
           state_mlstm_m, state_mlstm_conv, ln_ffn1, w_ffn1_gate, w_ffn1_up, w_ffn1_down, ln_mix, w_in, w_out,
           hgrn_lb_logits, hgrn_out_norm, rel_bias, mlstm_conv_w, mlstm_conv_b, mlstm_gate_bias,
           mlstm_out_norm, ln_ffn2, w_ffn2_gate, w_ffn2_up, w_ffn2_down, ln_final):
    depth = w_in.shape[0]
    bp, tp, d = x_prompt.shape
    bs, ts, _ = x_sample.shape
    assert ts == 1 and tp % MOBA_BLOCK == 0
    n_pages = page_table.shape[1]
    past_len = n_pages * PAGE_SIZE
    assert past_len % MOBA_BLOCK == 0 and past_len // MOBA_BLOCK >= MOBA_TOPK

    bf = lambda w: w.astype(BF16)
    wg1, wu1, wd1 = bf(w_ffn1_gate), bf(w_ffn1_up), bf(w_ffn1_down)
    wg2, wu2, wd2 = bf(w_ffn2_gate), bf(w_ffn2_up), bf(w_ffn2_down)
    w_in_b = bf(w_in)
    w_gate_b = bf(jnp.pad(w_in[:, :, N_MAIN:], ((0, 0), (0, 0), (0, LANES - N_GATE))))
    w_out_b = bf(w_out)
    row3 = lambda a: a.reshape(depth, 1, -1)
    ln1, lnm, ln2 = row3(ln_ffn1), row3(ln_mix), row3(ln_ffn2)
    lb = jnp.cumsum(jax.nn.softmax(hgrn_lb_logits.astype(F32), axis=0), axis=0)
    lb = lb - lb[0]
    llb, l1m, oml = row3(jnp.log(lb)), row3(jnp.log1p(-lb)), row3(1.0 - lb)
    hnorm, mnorm = row3(hgrn_out_norm), row3(mlstm_out_norm)
    conv_b = row3(mlstm_conv_b)
    gate_b = row3(jnp.pad(mlstm_gate_bias, ((0, 0), (0, LANES - N_GATE))))
    bias_tab = rel_bias.T.astype(F32)
    bias_p = _t5_bias_prompt(bias_tab, tp)
    bias_s = _t5_bias_sample(bias_tab, past_len)
    bias0 = jnp.broadcast_to(bias_tab[:, 0:1], (ATT_HEADS, LANES))

    zeros = lambda *s: jnp.zeros(s, F32)
    tm_p = 1024 if (bp * tp) % 1024 == 0 else 512
    tm_o = 512
    ll_p = 256

    xp = x_prompt.reshape(bp * tp, d)
    xs = x_sample.reshape(bs, d)
    outs = {k: [] for k in ("kp", "vp", "ks", "vs", "hgp", "hgs", "cp", "cs", "np", "ns", "mp", "ms", "cvp", "cvs")}

    for l in range(depth):
        xp = _ffn(xp, ln1, wg1, wu1, wd1, l, tm_p)
        u, gates = _inproj(xp, lnm, w_in_b, w_gate_b, l, tm_p)
        o_hg, hg_s = _hgrn(u, llb, l1m, oml, hnorm, zeros(bp, HG_HEADS, HEAD_DIM, HEAD_DIM), l, bp, tp, tp)
        o_att = _moba_prefill(u, bias_p, bp, tp)
        o_ml, c_s, n_s, m_s = _mlstm(u, gates, mlstm_conv_w, conv_b, gate_b, mnorm,
                                     zeros(bp, ML_HEADS, HEAD_DIM, HEAD_DIM), zeros(bp, SUBLANES, LANES),
                                     zeros(bp, SUBLANES, LANES), zeros(bp, SUBLANES, 2 * ML_WIDTH),
                                     l, bp, tp, tp, ll_p)
        xp = _outproj(xp, o_hg, o_att, o_ml, w_out_b, l, tm_o)
        xp = _ffn(xp, ln2, wg2, wu2, wd2, l, tm_p)
        u3 = u.reshape(bp, tp, N_MAIN)
        outs["kp"].append(u3[:, :, OFF_AK:OFF_AV].reshape(bp, tp, ATT_HEADS, HEAD_DIM))
        outs["vp"].append(u3[:, :, OFF_AV:OFF_MQK].reshape(bp, tp, ATT_HEADS, HEAD_DIM))
        outs["hgp"].append(hg_s)
        outs["cp"].append(c_s)
        outs["np"].append(n_s[:, :ML_HEADS])
        outs["mp"].append(m_s[:, :ML_HEADS, 0])
        outs["cvp"].append(u3[:, tp - (CONV_W - 1):, OFF_MQK:OFF_MV])

        xs = _ffn(xs, ln1, wg1, wu1, wd1, l, bs)
        us, gs = _inproj(xs, lnm, w_in_b, w_gate_b, l, bs)
        o_hg_s, hg_ss = _hgrn(_pad_rows(us, HG_CHUNK), llb, l1m, oml, hnorm, state_hgrn[l], l, bs, HG_CHUNK, 1)
        conv0 = jnp.pad(state_mlstm_conv[l], ((0, 0), (SUBLANES - (CONV_W - 1), 0), (0, 0)))
        o_ml_s, c_ss, n_ss, m_ss = _mlstm(_pad_rows(us, SAMPLE_PAD), _pad_rows(gs, SAMPLE_PAD), mlstm_conv_w,
                                          conv_b, gate_b, mnorm, state_mlstm_c[l],
                                          _vec_state(state_mlstm_n[l]), _vec_state(state_mlstm_m[l]), conv0,
                                          l, bs, SAMPLE_PAD, 1, SAMPLE_PAD)
        heads = lambda a: a.reshape(bs, ATT_HEADS, HEAD_DIM)
        q3, k3, v3 = heads(us[:, OFF_AQ:OFF_AK]), heads(us[:, OFF_AK:OFF_AV]), heads(us[:, OFF_AV:OFF_MQK])
        kmean = _moba_kmean(cache_k, page_table, l)
        sel = _moba_select(kmean, q3)[:, :MOBA_TOPK, :]
        sel = jnp.transpose(sel, (0, 2, 1))
        logi = (sel[..., None] * PAGES_PER_BLOCK
                + jnp.arange(PAGES_PER_BLOCK, dtype=jnp.int32)).reshape(bs, ATT_HEADS, N_SEL_PAGES)
        phys = jnp.take_along_axis(page_table[:, None, :], logi, axis=2)
        o_att_s = _moba_decode(cache_k, cache_v, phys.reshape(-1), logi.reshape(-1), q3, k3, v3,
                               bias_s, bias0, l).reshape(bs, ATT_WIDTH).astype(BF16)
        first = lambda a, rows: a.reshape(bs, rows, -1)[:, 0]
        xs = _outproj(xs, first(o_hg_s, HG_CHUNK), o_att_s, first(o_ml_s, SAMPLE_PAD), w_out_b, l, bs)
        xs = _ffn(xs, ln2, wg2, wu2, wd2, l, bs)
        outs["ks"].append(k3.reshape(bs, 1, ATT_HEADS, HEAD_DIM))
        outs["vs"].append(v3.reshape(bs, 1, ATT_HEADS, HEAD_DIM))
        outs["hgs"].append(hg_ss)
        outs["cs"].append(c_ss)
        outs["ns"].append(n_ss[:, :ML_HEADS])
        outs["ms"].append(m_ss[:, :ML_HEADS, 0])
        outs["cvs"].append(jnp.concatenate([state_mlstm_conv[l][:, 1:], us[:, None, OFF_MQK:OFF_MV]], axis=1))

    y_prompt = _final_norm(xp, ln_final.reshape(1, d), tm_o).reshape(bp, tp, d)
    y_sample = _final_norm(xs, ln_final.reshape(1, d), bs).reshape(bs, 1, d)
    st = {k: jnp.stack(v) for k, v in outs.items()}
    return (y_prompt, y_sample, st["kp"], st["vp"], st["ks"], st["vs"], st["hgp"], st["hgs"], st["cp"], st["cs"],
            st["np"], st["ns"], st["mp"], st["ms"], st["cvp"], st["cvs"])
```

```python
import functools
import math

import jax
import jax.numpy as jnp
import numpy as np
from jax import lax
from jax.experimental import pallas as pl
from jax.experimental.pallas import tpu as pltpu

F32 = jnp.float32
BF16 = jnp.bfloat16

HEAD_DIM = 128
HG_HEADS = 4
ATT_HEADS = 8
ML_HEADS = 4
HG_WIDTH = HG_HEADS * HEAD_DIM
ATT_WIDTH = ATT_HEADS * HEAD_DIM
ML_WIDTH = ML_HEADS * HEAD_DIM
N_MAIN = 4 * HG_WIDTH + 3 * ATT_WIDTH + 4 * ML_WIDTH
N_GATE = 2 * ML_HEADS
OFF_HQ, OFF_HF, OFF_HI, OFF_HG = 0, HG_WIDTH, 2 * HG_WIDTH, 3 * HG_WIDTH
OFF_AQ = 4 * HG_WIDTH
OFF_AK = OFF_AQ + ATT_WIDTH
OFF_AV = OFF_AK + ATT_WIDTH
OFF_MQK = OFF_AV + ATT_WIDTH
OFF_MV = OFF_MQK + 2 * ML_WIDTH
OFF_MO = OFF_MV + ML_WIDTH

PAGE_SIZE = 128
MOBA_BLOCK = 256
MOBA_TOPK = 3
PAGES_PER_BLOCK = MOBA_BLOCK // PAGE_SIZE
N_SEL_PAGES = MOBA_TOPK * PAGES_PER_BLOCK
N_BUCKETS = 32
MAX_DISTANCE = 4096
HG_CHUNK = 16
CONV_W = 4
EPS = 1e-6
NEG_BIG = -1e30
LANES = 128
SUBLANES = 8
VMEM_LIMIT = 48 * 1024 * 1024
SAMPLE_PAD = 128

_HI = lax.Precision.HIGHEST


def _cparams(sem):
    return pltpu.CompilerParams(dimension_semantics=sem, vmem_limit_bytes=VMEM_LIMIT)


def _rms(x, g):
    return x * lax.rsqrt(jnp.mean(x * x, axis=-1, keepdims=True) + EPS) * g


def _log_sigmoid(x):
    return jnp.minimum(x, 0.0) - jnp.log1p(jnp.exp(-jnp.abs(x)))


def _dot(a, b):
    return jnp.dot(a, b, preferred_element_type=F32)


def _dot_nt(a, b, precision=None):
    return lax.dot_general(a, b, (((1,), (1,)), ((), ())), preferred_element_type=F32, precision=precision)


def _dot_tn(a, b):
    return lax.dot_general(a, b, (((0,), (0,)), ((), ())), preferred_element_type=F32)


def _ffn_body(x_ref, g_ref, wg_ref, wu_ref, wd_ref, o_ref, xn_ref, nj, side_job=None):
    j = pl.program_id(1)

    @pl.when(j == 0)
    def _():
        xn_ref[...] = _rms(x_ref[...], g_ref[...]).astype(BF16)
        o_ref[...] = jnp.zeros_like(o_ref)

    if side_job is not None:
        side_job()
    xn = xn_ref[...]
    a = _dot(xn, wg_ref[...])
    b = _dot(xn, wu_ref[...])
    h = (a * jax.nn.sigmoid(a)) * b
    o_ref[...] += _dot(h.astype(BF16), wd_ref[...])

    @pl.when(j == nj - 1)
    def _():
        o_ref[...] = x_ref[...] + 0.5 * o_ref[...]


def _ffn_kernel(x_ref, g_ref, wg_ref, wu_ref, wd_ref, o_ref, xn_ref, *, nj):
    _ffn_body(x_ref, g_ref, wg_ref, wu_ref, wd_ref, o_ref, xn_ref, nj)


def _ffn_kmean_kernel(pages_ref, x_ref, g_ref, wg_ref, wu_ref, wd_ref, ck_ref, o_ref, km_ref, xn_ref, pbuf, sem,
                      *, nj, l, group, n_blocks):
    s = pl.program_id(0) * nj + pl.program_id(1)
    n_steps = pl.num_programs(0) * nj

    def block_of(step, g):
        return jnp.minimum(step * group + g, n_blocks - 1)

    def page_copy(step, g, p):
        slot = step % 2
        page = pages_ref[block_of(step, g) * PAGES_PER_BLOCK + p]
        k = g * PAGES_PER_BLOCK + p
        return pltpu.make_async_copy(ck_ref.at[l, page], pbuf.at[slot, k], sem.at[slot, k])

    def start_all(step):
        for g in range(group):
            for p in range(PAGES_PER_BLOCK):
                page_copy(step, g, p).start()

    @pl.when(s == 0)
    def _():
        start_all(s)

    @pl.when(s + 1 < n_steps)
    def _():
        start_all(s + 1)

    def block_means():
        slot = s % 2
        for g in range(group):
            for p in range(PAGES_PER_BLOCK):
                page_copy(s, g, p).wait()
        for g in range(group):
            tot = jnp.zeros((ATT_HEADS, HEAD_DIM), F32)
            for p in range(PAGES_PER_BLOCK):
                tot += jnp.sum(pbuf[slot, g * PAGES_PER_BLOCK + p], axis=0)
            km_ref[block_of(s, g)] = tot * (1.0 / MOBA_BLOCK)

    _ffn_body(x_ref, g_ref, wg_ref, wu_ref, wd_ref, o_ref, xn_ref, nj, side_job=block_means)


def _ffn(x, ln, wg, wu, wd, l, tm, kmean_of=None):
    m, d = x.shape
    f = wg.shape[-1]
    tf = 512 if f % 512 == 0 else f
    nj = f // tf
    grid = (m // tm, nj)
    in_specs = [
        pl.BlockSpec((tm, d), lambda i, j, *_: (i, 0)),
        pl.BlockSpec((None, 1, d), lambda i, j, *_: (l, 0, 0)),
        pl.BlockSpec((None, d, tf), lambda i, j, *_: (l, 0, j)),
        pl.BlockSpec((None, d, tf), lambda i, j, *_: (l, 0, j)),
        pl.BlockSpec((None, tf, d), lambda i, j, *_: (l, j, 0)),
    ]
    x_out = pl.BlockSpec((tm, d), lambda i, j, *_: (i, 0))
    if kmean_of is None:
        return pl.pallas_call(
            functools.partial(_ffn_kernel, nj=nj),
            out_shape=jax.ShapeDtypeStruct((m, d), F32),
            grid=grid, in_specs=in_specs, out_specs=x_out,
            scratch_shapes=[pltpu.VMEM((tm, d), BF16)],
            compiler_params=_cparams(("parallel", "arbitrary")),
            name="ffn",
        )(x, ln, wg, wu, wd)
    cache_k, page_table = kmean_of
    n_blocks = page_table.size // PAGES_PER_BLOCK
    group = -(-n_blocks // (grid[0] * grid[1]))
    n_buf = group * PAGES_PER_BLOCK
    return pl.pallas_call(
        functools.partial(_ffn_kmean_kernel, nj=nj, l=l, group=group, n_blocks=n_blocks),
        out_shape=(jax.ShapeDtypeStruct((m, d), F32),
                   jax.ShapeDtypeStruct((n_blocks, ATT_HEADS, HEAD_DIM), F32)),
        grid_spec=pltpu.PrefetchScalarGridSpec(
            num_scalar_prefetch=1, grid=grid,
            in_specs=in_specs + [pl.BlockSpec(memory_space=pl.ANY)],
            out_specs=(x_out, pl.BlockSpec((n_blocks, ATT_HEADS, HEAD_DIM), lambda i, j, pg: (0, 0, 0))),
            scratch_shapes=[pltpu.VMEM((tm, d), BF16),
                            pltpu.VMEM((2, n_buf, PAGE_SIZE, ATT_HEADS, HEAD_DIM), F32),
                            pltpu.SemaphoreType.DMA((2, n_buf))]),
        compiler_params=_cparams(("arbitrary", "arbitrary")),
        name="ffn_kmean",
    )(page_table.reshape(-1), x, ln, wg, wu, wd, cache_k)


def _inproj_kernel(x_ref, g_ref, w_ref, wgate_ref, u_ref, gate_ref, xn_ref):
    @pl.when(pl.program_id(1) == 0)
    def _():
        xn = _rms(x_ref[...], g_ref[...]).astype(BF16)
        xn_ref[...] = xn
        gate_ref[...] = _dot(xn, wgate_ref[...])

    u_ref[...] = _dot(xn_ref[...], w_ref[...])


def _inproj(x, ln, w, wgate, l, tm):
    m, d = x.shape
    tn = 1024
    return pl.pallas_call(
        _inproj_kernel,
        out_shape=(jax.ShapeDtypeStruct((m, N_MAIN), F32), jax.ShapeDtypeStruct((m, LANES), F32)),
        grid=(m // tm, N_MAIN // tn),
        in_specs=[
            pl.BlockSpec((tm, d), lambda i, j: (i, 0)),
            pl.BlockSpec((None, 1, d), lambda i, j: (l, 0, 0)),
            pl.BlockSpec((None, d, tn), lambda i, j: (l, 0, j)),
            pl.BlockSpec((None, d, LANES), lambda i, j: (l, 0, 0)),
        ],
        out_specs=(pl.BlockSpec((tm, tn), lambda i, j: (i, j)), pl.BlockSpec((tm, LANES), lambda i, j: (i, 0))),
        scratch_shapes=[pltpu.VMEM((tm, d), BF16)],
        compiler_params=_cparams(("parallel", "arbitrary")),
        name="inproj",
    )(x, ln, w, wgate)


def _outproj_kernel(x_ref, a_ref, b_ref, c_ref, w_ref, o_ref):
    acc = _dot(a_ref[...], w_ref[0:HG_WIDTH, :])
    acc += _dot(b_ref[...], w_ref[HG_WIDTH:HG_WIDTH + ATT_WIDTH, :])
    acc += _dot(c_ref[...], w_ref[HG_WIDTH + ATT_WIDTH:, :])
    o_ref[...] = x_ref[...] + acc


def _outproj(x, o_hg, o_att, o_ml, w, l, tm):
    m, d = x.shape
    dm = w.shape[1]
    return pl.pallas_call(
        _outproj_kernel,
        out_shape=jax.ShapeDtypeStruct((m, d), F32),
        grid=(m // tm,),
        in_specs=[
            pl.BlockSpec((tm, d), lambda i: (i, 0)),
            pl.BlockSpec((tm, HG_WIDTH), lambda i: (i, 0)),
            pl.BlockSpec((tm, ATT_WIDTH), lambda i: (i, 0)),
            pl.BlockSpec((tm, ML_WIDTH), lambda i: (i, 0)),
            pl.BlockSpec((None, dm, d), lambda i: (l, 0, 0)),
        ],
        out_specs=pl.BlockSpec((tm, d), lambda i: (i, 0)),
        compiler_params=_cparams(("parallel",)),
        name="outproj",
    )(x, o_hg, o_att, o_ml, w)


def _final_norm_kernel(x_ref, g_ref, o_ref):
    o_ref[...] = _rms(x_ref[...], g_ref[...])


def _final_norm(x, g, tm):
    m, d = x.shape
    return pl.pallas_call(
        _final_norm_kernel,
        out_shape=jax.ShapeDtypeStruct((m, d), F32),
        grid=(m // tm,),
        in_specs=[pl.BlockSpec((tm, d), lambda i: (i, 0)), pl.BlockSpec((1, d), lambda i: (0, 0))],
        out_specs=pl.BlockSpec((tm, d), lambda i: (i, 0)),
        compiler_params=_cparams(("parallel",)),
        name="final_norm",
    )(x, g)


def _hgrn_kernel(uq_ref, uf_ref, ui_ref, ug_ref, llb_ref, l1m_ref, oml_ref, norm_ref, s0_ref,
                 o_ref, sout_ref, st_ref, oacc_ref, qd_ref, kd_ref, dl_ref, *, tt, t_valid, t_total):
    c = HG_CHUNK
    ti = pl.program_id(1)
    nt = pl.num_programs(1)

    @pl.when(ti == 0)
    def _():
        for h in range(HG_HEADS):
            st_ref[h] = s0_ref[0, h].T

    rr = min(tt, LANES)
    nc = rr // c
    row = lax.broadcasted_iota(jnp.int32, (rr, rr), 0)
    col = lax.broadcasted_iota(jnp.int32, (rr, rr), 1)
    same_chunk = (row // c) == (col // c)
    tril_bd = jnp.where(same_chunk, jnp.where(col <= row, 1.0, 0.0), 0.0)
    ones_bd = jnp.where(same_chunk, 1.0, 0.0)
    s_idx = lax.broadcasted_iota(jnp.int32, (nc, c, HEAD_DIM), 1)
    sel_r = lax.broadcasted_iota(jnp.int32, (rr, c * rr), 0)
    sel_c = lax.broadcasted_iota(jnp.int32, (rr, c * rr), 1)
    sel_big = jnp.where((sel_c // rr) == (sel_r % c),
                        jnp.where(((sel_c % rr) // c) == (sel_r // c), 1.0, 0.0), 0.0).astype(BF16)
    ones = jnp.ones((HEAD_DIM, HEAD_DIM), BF16)
    llb = llb_ref[...]
    l1m = l1m_ref[...]
    oml = oml_ref[...]

    def intra(si, carry):
        r0 = pl.multiple_of(si * rr, rr)
        uf = uf_ref[pl.ds(r0, rr), :]
        b = l1m + _log_sigmoid(uf)
        lf = jnp.maximum(llb, b) + jnp.log1p(jnp.exp(-jnp.abs(llb - b)))
        kk = oml * jax.nn.sigmoid(-uf)
        if t_valid < t_total:
            valid = (ti * tt + r0 + lax.broadcasted_iota(jnp.int32, (rr, HG_WIDTH), 0)) < t_valid
            lf = jnp.where(valid, lf, 0.0)
            kk = jnp.where(valid, kk, 0.0)
        cum = jnp.dot(tril_bd, lf, preferred_element_type=F32, precision=_HI)
        tot = jnp.dot(ones_bd, lf, preferred_element_type=F32, precision=_HI)
        uq = uq_ref[pl.ds(r0, rr), :]
        ui = ui_ref[pl.ds(r0, rr), :]
        qd_ref[pl.ds(r0, rr), :] = (uq * jnp.exp(cum)).astype(BF16)
        kd_ref[pl.ds(r0, rr), :] = (kk * jnp.exp(tot - cum)).astype(BF16)
        dl_ref[pl.ds(r0, rr), :] = jnp.exp(tot)
        for h in range(HG_HEADS):
            hs = slice(h * HEAD_DIM, (h + 1) * HEAD_DIM)
            q3 = uq[:, hs].reshape(nc, c, HEAD_DIM)
            k3 = kk[:, hs].reshape(nc, c, HEAD_DIM)
            cm3 = cum[:, hs].reshape(nc, c, HEAD_DIM)
            ws = []
            for t in range(c):
                dec = jnp.exp(jnp.broadcast_to(cm3[:, t:t + 1, :], cm3.shape) - cm3)
                w = (jnp.broadcast_to(q3[:, t:t + 1, :], q3.shape) * k3) * dec
                ws.append(jnp.where(s_idx <= t, w, 0.0).reshape(rr, HEAD_DIM))
            w_all = jnp.concatenate(ws, axis=0).astype(BF16)
            r = _dot(w_all, ones)
            vt = jnp.concatenate([ui[:, hs]] * c, axis=0)
            oacc_ref[pl.ds(r0, rr), hs] = _dot(sel_big, (r * vt).astype(BF16))
        return carry

    lax.fori_loop(0, tt // rr, intra, 0)

    def inter(ci, carry):
        r0 = pl.multiple_of(ci * c, c)
        for h in range(HG_HEADS):
            hs = slice(h * HEAD_DIM, (h + 1) * HEAD_DIM)
            st = st_ref[h]
            oacc_ref[pl.ds(r0, c), hs] += _dot_nt(qd_ref[pl.ds(r0, c), hs], st.astype(BF16))
            st_ref[h] = st * dl_ref[pl.ds(r0, 1), hs] + _dot_tn(ui_ref[pl.ds(r0, c), hs].astype(BF16),
                                                                 kd_ref[pl.ds(r0, c), hs])
        return carry

    lax.fori_loop(0, tt // c, inter, 0, unroll=min(4, tt // c))

    for h in range(HG_HEADS):
        hs = slice(h * HEAD_DIM, (h + 1) * HEAD_DIM)
        g = ug_ref[:, hs]
        y = _rms(oacc_ref[:, hs], norm_ref[:, hs]) * (g * jax.nn.sigmoid(g))
        o_ref[:, hs] = y.astype(o_ref.dtype)

    @pl.when(ti == nt - 1)
    def _():
        for h in range(HG_HEADS):
            sout_ref[0, h] = st_ref[h].T


def _hgrn(u, llb, l1m, oml, norm, s0, l, n_seq, t_total, t_valid):
    tt = min(t_total, 512)
    nt = t_total // tt
    cb = HG_WIDTH
    ublock = lambda k: pl.BlockSpec((tt, cb), lambda b, t: (b * nt + t, k))
    par = pl.BlockSpec((None, 1, cb), lambda b, t: (l, 0, 0))
    sblock = pl.BlockSpec((1, HG_HEADS, HEAD_DIM, HEAD_DIM), lambda b, t: (b, 0, 0, 0))
    return pl.pallas_call(
        functools.partial(_hgrn_kernel, tt=tt, t_valid=t_valid, t_total=t_total),
        out_shape=(jax.ShapeDtypeStruct((n_seq * t_total, cb), BF16),
                   jax.ShapeDtypeStruct((n_seq, HG_HEADS, HEAD_DIM, HEAD_DIM), F32)),
        grid=(n_seq, nt),
        in_specs=[ublock(OFF_HQ // cb), ublock(OFF_HF // cb), ublock(OFF_HI // cb), ublock(OFF_HG // cb),
                  par, par, par, par, sblock],
        out_specs=(pl.BlockSpec((tt, cb), lambda b, t: (b * nt + t, 0)), sblock),
        scratch_shapes=[pltpu.VMEM((HG_HEADS, HEAD_DIM, HEAD_DIM), F32), pltpu.VMEM((tt, cb), F32),
                        pltpu.VMEM((tt, cb), BF16), pltpu.VMEM((tt, cb), BF16), pltpu.VMEM((tt, cb), F32)],
        compiler_params=_cparams(("parallel", "arbitrary")),
        name="hgrn",
    )(u, u, u, u, llb, l1m, oml, norm, s0)


def _mlstm_kernel(qk_ref, v_ref, og_ref, gate_ref, cw_ref, cb_ref, gb_ref, norm_ref, c0_ref, n0_ref, m0_ref,
                  conv0_ref, o_ref, cout_ref, nout_ref, mout_ref, c_ref, n_ref, m_ref, carry_ref,
                  *, ll, t_valid, t_total):
    ti = pl.program_id(1)
    nt = pl.num_programs(1)

    @pl.when(ti == 0)
    def _():
        c_ref[...] = c0_ref[0]
        n_ref[...] = n0_ref[0]
        m_ref[...] = m0_ref[0]
        carry_ref[...] = conv0_ref[0]

    x = qk_ref[...]
    xe = jnp.concatenate([carry_ref[...], x], axis=0)
    cw = cw_ref[...]
    y = cb_ref[...] + cw[3:4, :] * x
    for j in range(1, CONV_W):
        y += cw[3 - j:4 - j, :] * xe[SUBLANES - j:SUBLANES - j + ll, :]
    carry_ref[...] = x[ll - SUBLANES:, :]
    qk = y * jax.nn.sigmoid(y)

    g = gate_ref[...] + gb_ref[...]
    lf = _log_sigmoid(g)
    ipre = g
    if t_valid < t_total:
        valid = (ti * ll + lax.broadcasted_iota(jnp.int32, (ll, LANES), 0)) < t_valid
        lf = jnp.where(valid, lf, 0.0)
        ipre = jnp.where(valid, ipre, NEG_BIG)
    row = lax.broadcasted_iota(jnp.int32, (ll, ll), 0)
    col = lax.broadcasted_iota(jnp.int32, (ll, ll), 1)
    causal = col <= row
    cum = jnp.dot(causal.astype(F32), lf, preferred_element_type=F32, precision=_HI)
    lane = lax.broadcasted_iota(jnp.int32, (ll, LANES), 1)
    a_t = jnp.where(lane < ML_HEADS, ipre, cum).T

    for h in range(ML_HEADS):
        hs = slice(h * HEAD_DIM, (h + 1) * HEAD_DIM)
        q = qk[:, hs]
        k = qk[:, ML_WIDTH + h * HEAD_DIM:ML_WIDTH + (h + 1) * HEAD_DIM] * HEAD_DIM ** -0.5
        v = v_ref[:, hs]
        col_cum = cum[:, ML_HEADS + h:ML_HEADS + h + 1]
        col_i = ipre[:, h:h + 1]
        row_cum = a_t[ML_HEADS + h:ML_HEADS + h + 1, :]
        row_i = a_t[h:h + 1, :]
        m_prev = m_ref[h:h + 1, 0:1]
        log_d = jnp.where(causal, col_cum - row_cum + row_i, -jnp.inf)
        m_inter = col_cum + m_prev
        m_t = jnp.maximum(m_inter, jnp.max(log_d, axis=-1, keepdims=True))
        w_inter = jnp.exp(m_inter - m_t)
        qb, kb, vb = q.astype(BF16), k.astype(BF16), v.astype(BF16)
        s = _dot_nt(qb, kb) * jnp.exp(log_d - m_t)
        c_old = c_ref[h]
        n_old = n_ref[h:h + 1, :]
        num = w_inter * _dot(qb, c_old.astype(BF16)) + _dot(s.astype(BF16), vb)
        den = w_inter * jnp.sum(q * n_old, axis=-1, keepdims=True) + jnp.sum(s, axis=-1, keepdims=True)
        hh = num / jnp.maximum(jnp.abs(den), jnp.exp(-m_t))
        m_new = m_t[ll - 1:ll, :]
        cum_last = col_cum[ll - 1:ll, :]
        w_k = jnp.exp(cum_last - col_cum + col_i - m_new)
        decay = jnp.exp(cum_last + m_prev - m_new)
        kw = k * w_k
        c_ref[h] = decay * c_old + _dot_tn(kw.astype(BF16), vb)
        n_ref[h:h + 1, :] = decay * n_old + jnp.sum(kw, axis=0, keepdims=True)
        m_ref[h:h + 1, :] = jnp.broadcast_to(m_new, (1, LANES))
        og = og_ref[:, hs]
        o_ref[:, hs] = (_rms(hh, norm_ref[:, hs]) * jax.nn.sigmoid(og)).astype(o_ref.dtype)

    @pl.when(ti == nt - 1)
    def _():
        cout_ref[0] = c_ref[...]
        nout_ref[0] = n_ref[...]
        mout_ref[0] = m_ref[...]


def _mlstm(u, gates, cw, cb, gb, norm, c0, n0, m0, conv0, l, n_seq, t_total, t_valid, ll):
    nt = t_total // ll
    ublock = lambda w, k: pl.BlockSpec((ll, w), lambda b, t: (b * nt + t, k))
    par = lambda r, w: pl.BlockSpec((None, r, w), lambda b, t: (l, 0, 0))
    cblock = pl.BlockSpec((1, ML_HEADS, HEAD_DIM, HEAD_DIM), lambda b, t: (b, 0, 0, 0))
    vblock = pl.BlockSpec((1, SUBLANES, LANES), lambda b, t: (b, 0, 0))
    return pl.pallas_call(
        functools.partial(_mlstm_kernel, ll=ll, t_valid=t_valid, t_total=t_total),
        out_shape=(jax.ShapeDtypeStruct((n_seq * t_total, ML_WIDTH), BF16),
                   jax.ShapeDtypeStruct((n_seq, ML_HEADS, HEAD_DIM, HEAD_DIM), F32),
                   jax.ShapeDtypeStruct((n_seq, SUBLANES, LANES), F32),
                   jax.ShapeDtypeStruct((n_seq, SUBLANES, LANES), F32)),
        grid=(n_seq, nt),
        in_specs=[ublock(2 * ML_WIDTH, OFF_MQK // (2 * ML_WIDTH)), ublock(ML_WIDTH, OFF_MV // ML_WIDTH),
                  ublock(ML_WIDTH, OFF_MO // ML_WIDTH),
                  pl.BlockSpec((ll, LANES), lambda b, t: (b * nt + t, 0)),
                  par(CONV_W, 2 * ML_WIDTH), par(1, 2 * ML_WIDTH), par(1, LANES), par(1, ML_WIDTH),
                  cblock, vblock, vblock,
                  pl.BlockSpec((1, SUBLANES, 2 * ML_WIDTH), lambda b, t: (b, 0, 0))],
        out_specs=(pl.BlockSpec((ll, ML_WIDTH), lambda b, t: (b * nt + t, 0)), cblock, vblock, vblock),
        scratch_shapes=[pltpu.VMEM((ML_HEADS, HEAD_DIM, HEAD_DIM), F32), pltpu.VMEM((SUBLANES, LANES), F32),
                        pltpu.VMEM((SUBLANES, LANES), F32), pltpu.VMEM((SUBLANES, 2 * ML_WIDTH), F32)],
        compiler_params=_cparams(("parallel", "arbitrary")),
        name="mlstm",
    )(u, u, u, gates, cw, cb, gb, norm, c0, n0, m0, conv0)


def _t5_bucket_np(rel):
    rel = np.asarray(rel, np.int64)
    max_exact = N_BUCKETS // 2
    relf = np.maximum(rel, 1).astype(np.float64)
    large = max_exact + (np.log(relf / max_exact) / math.log(MAX_DISTANCE / max_exact)
                         * (N_BUCKETS - max_exact)).astype(np.int64)
    return np.where(rel < max_exact, rel, np.minimum(large, N_BUCKETS - 1)).astype(np.int32)


def _t5_thresholds(max_rel):
    buckets = _t5_bucket_np(np.arange(max_rel + 1))
    out = []
    for b in range(1, N_BUCKETS):
        hit = np.nonzero(buckets >= b)[0]
        out.append(int(hit[0]) if hit.size else None)
    return out


def _t5_bias_kernel(tab_ref, o_ref, *, rel_fn, thresholds):
    rel = rel_fn(o_ref.shape[1:])
    acc = [jnp.full(o_ref.shape[1:], tab_ref[h, 0], F32) for h in range(ATT_HEADS)]
    for b, th in enumerate(thresholds, start=1):
        if th is None:
            continue
        reached = rel >= th
        for h in range(ATT_HEADS):
            acc[h] = jnp.where(reached, tab_ref[h, b], acc[h])
    for h in range(ATT_HEADS):
        o_ref[h] = acc[h]


def _t5_bias_prompt(bias_tab, t_total):
    blk = MOBA_BLOCK
    nb = t_total // blk

    def rel_fn(shape):
        return (pl.program_id(0) * blk + lax.broadcasted_iota(jnp.int32, shape, 1)
                - lax.broadcasted_iota(jnp.int32, shape, 0))

    return pl.pallas_call(
        functools.partial(_t5_bias_kernel, rel_fn=rel_fn, thresholds=_t5_thresholds(t_total)),
        out_shape=jax.ShapeDtypeStruct((ATT_HEADS, blk, nb * blk), F32),
        grid=(nb,),
        in_specs=[pl.BlockSpec(memory_space=pltpu.SMEM)],
        out_specs=pl.BlockSpec((ATT_HEADS, blk, blk), lambda d: (0, 0, d)),
        compiler_params=_cparams(("parallel",)),
        name="t5_bias_prompt",
    )(bias_tab)


def _t5_bias_sample(bias_tab, past_len):
    n_pages = past_len // PAGE_SIZE

    def rel_fn(shape):
        return past_len - (lax.broadcasted_iota(jnp.int32, shape, 0) * PAGE_SIZE
                           + lax.broadcasted_iota(jnp.int32, shape, 1))

    return pl.pallas_call(
        functools.partial(_t5_bias_kernel, rel_fn=rel_fn, thresholds=_t5_thresholds(past_len)),
        out_shape=jax.ShapeDtypeStruct((ATT_HEADS, n_pages, PAGE_SIZE), F32),
        grid=(1,),
        in_specs=[pl.BlockSpec(memory_space=pltpu.SMEM)],
        out_specs=pl.BlockSpec((ATT_HEADS, n_pages, PAGE_SIZE), lambda i: (0, 0, 0)),
        compiler_params=_cparams(("arbitrary",)),
        name="t5_bias_sample",
    )(bias_tab)


def _moba_prefill_kernel(q_ref, k_ref, v_ref, bias_ref, o_ref, vt_ref, m_ref, l_ref, acc_ref, *, nb):
    blk = MOBA_BLOCK
    t_total = nb * blk
    scale = HEAD_DIM ** -0.5
    q = q_ref[...]
    k = k_ref[...]
    qb = q.astype(BF16)
    kb = k.astype(BF16)
    vt_ref[...] = v_ref[...].T.astype(BF16)

    kmean = jnp.concatenate(
        [jnp.mean(k[n * blk:(n + 1) * blk, :], axis=0, keepdims=True) for n in range(nb)]
        + [jnp.zeros((SUBLANES - nb % SUBLANES, HEAD_DIM), F32)] * (nb % SUBLANES != 0), axis=0)
    nrow = kmean.shape[0]
    gate = _dot_nt(kmean, q, precision=_HI)
    brow = lax.broadcasted_iota(jnp.int32, (nrow, t_total), 0)
    qblk = lax.broadcasted_iota(jnp.int32, (nrow, t_total), 1) // blk
    qblk_row = qblk[0:1, :]
    chosen = []
    for n in range(nb - 1):
        gn = gate[n:n + 1, :]
        ahead = jnp.where(brow < n, jnp.where(gate >= gn, 1.0, 0.0), jnp.where(gate > gn, 1.0, 0.0))
        rank = jnp.sum(jnp.where(brow < qblk, ahead, 0.0), axis=0, keepdims=True)
        chosen.append(jnp.where(qblk_row > n, jnp.where(rank < MOBA_TOPK, 1.0, 0.0), 0.0))

    krow = lax.broadcasted_iota(jnp.int32, (blk, blk), 0)
    qcol = lax.broadcasted_iota(jnp.int32, (blk, blk), 1)
    for i in range(nb):
        rs = slice(i * blk, (i + 1) * blk)
        lg = _dot_nt(kb[rs], qb[rs]) * scale + bias_ref[:, 0:blk]
        lg = jnp.where(krow <= qcol, lg, -jnp.inf)
        m0 = jnp.max(lg, axis=0, keepdims=True)
        p = jnp.exp(lg - m0)
        m_ref[:, rs] = m0
        l_ref[:, rs] = jnp.sum(p, axis=0, keepdims=True)
        acc_ref[:, rs] = _dot(vt_ref[:, rs], p.astype(BF16))

    for n in range(nb - 1):
        ks = slice(n * blk, (n + 1) * blk)
        qs = slice((n + 1) * blk, t_total)
        nq = t_total - (n + 1) * blk
        lg = _dot_nt(kb[ks], qb[qs]) * scale + bias_ref[:, blk:blk + nq]
        lg = jnp.where(chosen[n][:, qs] > 0.0, lg, -jnp.inf)
        m_old = m_ref[:, qs]
        m_new = jnp.maximum(m_old, jnp.max(lg, axis=0, keepdims=True))
        alpha = jnp.exp(m_old - m_new)
        p = jnp.exp(lg - m_new)
        m_ref[:, qs] = m_new
        l_ref[:, qs] = alpha * l_ref[:, qs] + jnp.sum(p, axis=0, keepdims=True)
        acc_ref[:, qs] = alpha * acc_ref[:, qs] + _dot(vt_ref[:, ks], p.astype(BF16))

    o_ref[...] = (acc_ref[...] / l_ref[...]).T.astype(o_ref.dtype)


def _moba_prefill(u, bias, n_seq, t_total):
    blk = MOBA_BLOCK
    nb = t_total // blk
    hd = HEAD_DIM
    tok = lambda off: pl.BlockSpec((t_total, hd), lambda h, b: (b, off // hd + h))
    return pl.pallas_call(
        functools.partial(_moba_prefill_kernel, nb=nb),
        out_shape=jax.ShapeDtypeStruct((n_seq * t_total, ATT_WIDTH), BF16),
        grid=(ATT_HEADS, n_seq),
        in_specs=[tok(OFF_AQ), tok(OFF_AK), tok(OFF_AV),
                  pl.BlockSpec((None, blk, nb * blk), lambda h, b: (h, 0, 0))],
        out_specs=pl.BlockSpec((t_total, hd), lambda h, b: (b, h)),
        scratch_shapes=[pltpu.VMEM((hd, t_total), BF16), pltpu.VMEM((1, t_total), F32),
                        pltpu.VMEM((1, t_total), F32), pltpu.VMEM((hd, t_total), F32)],
        compiler_params=_cparams(("parallel", "parallel")),
        name="moba_prefill",
    )(u, u, u, bias)


def _moba_select_kernel(km_ref, q_ref, o_ref, *, nblk):
    g = jnp.sum(km_ref[0] * q_ref[0][None], axis=-1)
    idx = lax.broadcasted_iota(jnp.int32, (nblk, ATT_HEADS), 0).astype(F32)
    o_ref[...] = jnp.zeros_like(o_ref)
    for j in range(MOBA_TOPK):
        mx = jnp.max(g, axis=0, keepdims=True)
        first = jnp.min(jnp.where(g == mx, idx, float(nblk)), axis=0, keepdims=True)
        o_ref[0, j:j + 1, :] = first.astype(jnp.int32)
        g = jnp.where(idx == first, -jnp.inf, g)


def _moba_select(kmean, q3):
    n_seq, nblk = kmean.shape[:2]
    return pl.pallas_call(
        functools.partial(_moba_select_kernel, nblk=nblk),
        out_shape=jax.ShapeDtypeStruct((n_seq, SUBLANES, ATT_HEADS), jnp.int32),
        grid=(n_seq,),
        in_specs=[pl.BlockSpec((1, nblk, ATT_HEADS, HEAD_DIM), lambda b: (b, 0, 0, 0)),
                  pl.BlockSpec((1, ATT_HEADS, HEAD_DIM), lambda b: (b, 0, 0))],
        out_specs=pl.BlockSpec((1, SUBLANES, ATT_HEADS), lambda b: (b, 0, 0)),
        compiler_params=_cparams(("parallel",)),
        name="moba_select",
    )(kmean, q3)


def _moba_decode_kernel(phys_ref, logi_ref, q_ref, kn_ref, vn_ref, bias_ref, bias0_ref, ck_ref, cv_ref, o_ref,
                        kbuf, vbuf, sem, *, l):
    b = pl.program_id(0)
    scale = HEAD_DIM ** -0.5

    def page_copy(which, h, j):
        slot = h * N_SEL_PAGES + j
        pg = phys_ref[(b * ATT_HEADS + h) * N_SEL_PAGES + j]
        src, dst = ((ck_ref, kbuf), (cv_ref, vbuf))[which]
        return pltpu.make_async_copy(src.at[l, pg, :, h, :], dst.at[slot], sem.at[which, slot])

    for h in range(ATT_HEADS):
        for j in range(N_SEL_PAGES):
            page_copy(0, h, j).start()
            page_copy(1, h, j).start()

    for h in range(ATT_HEADS):
        q = q_ref[0, h:h + 1, :]
        q8 = jnp.broadcast_to(q, (SUBLANES, HEAD_DIM)).astype(BF16)
        logits = []
        for j in range(N_SEL_PAGES):
            page_copy(0, h, j).wait()
            lp = logi_ref[(b * ATT_HEADS + h) * N_SEL_PAGES + j]
            logits.append(_dot_nt(q8, kbuf[h * N_SEL_PAGES + j].astype(BF16)) * scale
                          + bias_ref[h, pl.ds(lp, 1), :])
        self_logit = jnp.sum(q * kn_ref[0, h:h + 1, :], axis=-1, keepdims=True) * scale + bias0_ref[h:h + 1, 0:1]
        m = self_logit
        for lg in logits:
            m = jnp.maximum(m, jnp.max(lg, axis=-1, keepdims=True))
        p_self = jnp.exp(self_logit - m)
        den = p_self
        acc = p_self * vn_ref[0, h:h + 1, :]
        for j in range(N_SEL_PAGES):
            page_copy(1, h, j).wait()
            p = jnp.exp(logits[j] - m)
            den += jnp.sum(p, axis=-1, keepdims=True)
            acc += _dot(p.astype(BF16), vbuf[h * N_SEL_PAGES + j].astype(BF16))
        o_ref[0, h:h + 1, :] = (acc / den)[0:1, :]


def _moba_decode(cache_k, cache_v, phys, logi, q3, k3, v3, bias_s, bias0, l):
    n_seq = q3.shape[0]
    n_slots = ATT_HEADS * N_SEL_PAGES
    tok = pl.BlockSpec((1, ATT_HEADS, HEAD_DIM), lambda b, ph, lg: (b, 0, 0))
    return pl.pallas_call(
        functools.partial(_moba_decode_kernel, l=l),
        out_shape=jax.ShapeDtypeStruct((n_seq, ATT_HEADS, HEAD_DIM), F32),
        grid_spec=pltpu.PrefetchScalarGridSpec(
            num_scalar_prefetch=2, grid=(n_seq,),
            in_specs=[tok, tok, tok,
                      pl.BlockSpec(bias_s.shape, lambda b, ph, lg: (0, 0, 0)),
                      pl.BlockSpec(bias0.shape, lambda b, ph, lg: (0, 0)),
                      pl.BlockSpec(memory_space=pl.ANY), pl.BlockSpec(memory_space=pl.ANY)],
            out_specs=tok,
            scratch_shapes=[pltpu.VMEM((n_slots, PAGE_SIZE, HEAD_DIM), F32),
                            pltpu.VMEM((n_slots, PAGE_SIZE, HEAD_DIM), F32),
                            pltpu.SemaphoreType.DMA((2, n_slots))]),
        compiler_params=_cparams(("arbitrary",)),
        name="moba_decode",
    )(phys, logi, q3, k3, v3, bias_s, bias0, cache_k, cache_v)


def _pad_rows(a, rows):
    n, w = a.shape
    return jnp.pad(a[:, None, :], ((0, 0), (0, rows - 1), (0, 0))).reshape(n * rows, w)


def _vec_state(a):
    if a.ndim == 2:
        a = jnp.broadcast_to(a[:, :, None], a.shape + (LANES,))
    return jnp.pad(a, ((0, 0), (0, SUBLANES - a.shape[1]), (0, 0)))


def kernel(x_prompt, x_sample, cache_k, cache_v, page_table, state_hgrn, state_mlstm_c, state_mlstm_n,
           state_mlstm_m, state_mlstm_conv, ln_ffn1, w_ffn1_gate, w_ffn1_up, w_ffn1_down, ln_mix, w_in, w_out,
           hgrn_lb_logits, hgrn_out_norm, rel_bias, mlstm_conv_w, mlstm_conv_b, mlstm_gate_bias,
           mlstm_out_norm, ln_ffn2, w_ffn2_gate, w_ffn2_up, w_ffn2_down, ln_final):
    depth = w_in.shape[0]
    bp, tp, d = x_prompt.shape
    bs, ts, _ = x_sample.shape
    assert ts == 1 and tp % MOBA_BLOCK == 0
    n_pages = page_table.shape[1]
    past_len = n_pages * PAGE_SIZE
    assert past_len % MOBA_BLOCK == 0 and past_len // MOBA_BLOCK >= MOBA_TOPK

    bf = lambda w: w.astype(BF16)
    wg1, wu1, wd1 = bf(w_ffn1_gate), bf(w_ffn1_up), bf(w_ffn1_down)
    wg2, wu2, wd2 = bf(w_ffn2_gate), bf(w_ffn2_up), bf(w_ffn2_down)
    w_in_b = bf(w_in)
    w_gate_b = bf(jnp.pad(w_in[:, :, N_MAIN:], ((0, 0), (0, 0), (0, LANES - N_GATE))))
    w_out_b = bf(w_out)
    row3 = lambda a: a.reshape(depth, 1, -1)
    ln1, lnm, ln2 = row3(ln_ffn1), row3(ln_mix), row3(ln_ffn2)
    lb = jnp.cumsum(jax.nn.softmax(hgrn_lb_logits.astype(F32), axis=0), axis=0)
    lb = lb - lb[0]
    llb, l1m, oml = row3(jnp.log(lb)), row3(jnp.log1p(-lb)), row3(1.0 - lb)
    hnorm, mnorm = row3(hgrn_out_norm), row3(mlstm_out_norm)
    conv_b = row3(mlstm_conv_b)
    gate_b = row3(jnp.pad(mlstm_gate_bias, ((0, 0), (0, LANES - N_GATE))))
    bias_tab = rel_bias.T.astype(F32)
    bias_p = _t5_bias_prompt(bias_tab, tp)
    bias_s = _t5_bias_sample(bias_tab, past_len)
    bias0 = jnp.broadcast_to(bias_tab[:, 0:1], (ATT_HEADS, LANES))

    zeros = lambda *s: jnp.zeros(s, F32)
    tm_f = 512
    tm_p = 1024 if (bp * tp) % 1024 == 0 else 512
    tm_o = 512
    ll_p = 256

    xp = x_prompt.reshape(bp * tp, d)
    xs = x_sample.reshape(bs, d)
    outs = {k: [] for k in ("ks", "vs", "hgp", "hgs", "cp", "cs", "np", "ns", "mp", "ms", "cvp", "cvs")}
    u_all = []

    for l in range(depth):
        xp, kmean = _ffn(xp, ln1, wg1, wu1, wd1, l, tm_f, kmean_of=(cache_k, page_table))
        u, gates = _inproj(xp, lnm, w_in_b, w_gate_b, l, tm_p)
        o_hg, hg_s = _hgrn(u, llb, l1m, oml, hnorm, zeros(bp, HG_HEADS, HEAD_DIM, HEAD_DIM), l, bp, tp, tp)
        o_att = _moba_prefill(u, bias_p, bp, tp)
        o_ml, c_s, n_s, m_s = _mlstm(u, gates, mlstm_conv_w, conv_b, gate_b, mnorm,
                                     zeros(bp, ML_HEADS, HEAD_DIM, HEAD_DIM), zeros(bp, SUBLANES, LANES),
                                     zeros(bp, SUBLANES, LANES), zeros(bp, SUBLANES, 2 * ML_WIDTH),
                                     l, bp, tp, tp, ll_p)
        xp = _outproj(xp, o_hg, o_att, o_ml, w_out_b, l, tm_o)
        xp = _ffn(xp, ln2, wg2, wu2, wd2, l, tm_f)
        u3 = u.reshape(bp, tp, N_MAIN)
        u_all.append(u3)
        outs["hgp"].append(hg_s)
        outs["cp"].append(c_s)
        outs["np"].append(n_s[:, :ML_HEADS])
        outs["mp"].append(m_s[:, :ML_HEADS, 0])
        outs["cvp"].append(u3[:, tp - (CONV_W - 1):, OFF_MQK:OFF_MV])

        xs = _ffn(xs, ln1, wg1, wu1, wd1, l, bs)
        us, gs = _inproj(xs, lnm, w_in_b, w_gate_b, l, bs)
        o_hg_s, hg_ss = _hgrn(_pad_rows(us, HG_CHUNK), llb, l1m, oml, hnorm, state_hgrn[l], l, bs, HG_CHUNK, 1)
        conv0 = jnp.pad(state_mlstm_conv[l], ((0, 0), (SUBLANES - (CONV_W - 1), 0), (0, 0)))
        o_ml_s, c_ss, n_ss, m_ss = _mlstm(_pad_rows(us, SAMPLE_PAD), _pad_rows(gs, SAMPLE_PAD), mlstm_conv_w,
                                          conv_b, gate_b, mnorm, state_mlstm_c[l],
                                          _vec_state(state_mlstm_n[l]), _vec_state(state_mlstm_m[l]), conv0,
                                          l, bs, SAMPLE_PAD, 1, SAMPLE_PAD)
        heads = lambda a: a.reshape(bs, ATT_HEADS, HEAD_DIM)
        q3, k3, v3 = heads(us[:, OFF_AQ:OFF_AK]), heads(us[:, OFF_AK:OFF_AV]), heads(us[:, OFF_AV:OFF_MQK])
        kmean = kmean.reshape(bs, -1, ATT_HEADS, HEAD_DIM)
        sel = _moba_select(kmean, q3)[:, :MOBA_TOPK, :]
        sel = jnp.transpose(sel, (0, 2, 1))
        logi = (sel[..., None] * PAGES_PER_BLOCK
                + jnp.arange(PAGES_PER_BLOCK, dtype=jnp.int32)).reshape(bs, ATT_HEADS, N_SEL_PAGES)
        phys = jnp.take_along_axis(page_table[:, None, :], logi, axis=2)
        o_att_s = _moba_decode(cache_k, cache_v, phys.reshape(-1), logi.reshape(-1), q3, k3, v3,
                               bias_s, bias0, l).reshape(bs, ATT_WIDTH).astype(BF16)
        first = lambda a, rows: a.reshape(bs, rows, -1)[:, 0]
        xs = _outproj(xs, first(o_hg_s, HG_CHUNK), o_att_s, first(o_ml_s, SAMPLE_PAD), w_out_b, l, bs)
        xs = _ffn(xs, ln2, wg2, wu2, wd2, l, bs)
        outs["ks"].append(k3.reshape(bs, 1, ATT_HEADS, HEAD_DIM))
        outs["vs"].append(v3.reshape(bs, 1, ATT_HEADS, HEAD_DIM))
        outs["hgs"].append(hg_ss)
        outs["cs"].append(c_ss)
        outs["ns"].append(n_ss[:, :ML_HEADS])
        outs["ms"].append(m_ss[:, :ML_HEADS, 0])
        outs["cvs"].append(jnp.concatenate([state_mlstm_conv[l][:, 1:], us[:, None, OFF_MQK:OFF_MV]], axis=1))

    y_prompt = _final_norm(xp, ln_final.reshape(1, d), tm_o).reshape(bp, tp, d)
    y_sample = _final_norm(xs, ln_final.reshape(1, d), bs).reshape(bs, 1, d)
    st = {k: jnp.stack(v) for k, v in outs.items()}
    heads_p = lambda lo, hi: jnp.stack([u3[:, :, lo:hi] for u3 in u_all]).reshape(depth, bp, tp, ATT_HEADS, HEAD_DIM)
    st["kp"], st["vp"] = heads_p(OFF_AK, OFF_AV), heads_p(OFF_AV, OFF_MQK)
    return (y_prompt, y_sample, st["kp"], st["vp"], st["ks"], st["vs"], st["hgp"], st["hgs"], st["cp"], st["cs"],
            st["np"], st["ns"], st["mp"], st["ms"], st["cvp"], st["cvs"])
```

```python
import functools
import math

import jax
import jax.numpy as jnp
import numpy as np
from jax import lax
from jax.experimental import pallas as pl
from jax.experimental.pallas import tpu as pltpu

F32 = jnp.float32
BF16 = jnp.bfloat16

HEAD_DIM = 128
HG_HEADS = 4
ATT_HEADS = 8
ML_HEADS = 4
HG_WIDTH = HG_HEADS * HEAD_DIM
ATT_WIDTH = ATT_HEADS * HEAD_DIM
ML_WIDTH = ML_HEADS * HEAD_DIM
N_MAIN = 4 * HG_WIDTH + 3 * ATT_WIDTH + 4 * ML_WIDTH
N_GATE = 2 * ML_HEADS
OFF_HQ, OFF_HF, OFF_HI, OFF_HG = 0, HG_WIDTH, 2 * HG_WIDTH, 3 * HG_WIDTH
OFF_AQ = 4 * HG_WIDTH
OFF_AK = OFF_AQ + ATT_WIDTH
OFF_AV = OFF_AK + ATT_WIDTH
OFF_MQK = OFF_AV + ATT_WIDTH
OFF_MV = OFF_MQK + 2 * ML_WIDTH
OFF_MO = OFF_MV + ML_WIDTH

PAGE_SIZE = 128
MOBA_BLOCK = 256
MOBA_TOPK = 3
PAGES_PER_BLOCK = MOBA_BLOCK // PAGE_SIZE
N_SEL_PAGES = MOBA_TOPK * PAGES_PER_BLOCK
N_BUCKETS = 32
MAX_DISTANCE = 4096
HG_CHUNK = 16
CONV_W = 4
EPS = 1e-6
NEG_BIG = -1e30
LANES = 128
SUBLANES = 8
VMEM_LIMIT = 48 * 1024 * 1024
SAMPLE_PAD = 128

_HI = lax.Precision.HIGHEST


def _cparams(sem):
    return pltpu.CompilerParams(dimension_semantics=sem, vmem_limit_bytes=VMEM_LIMIT)


def _rms(x, g):
    return x * lax.rsqrt(jnp.mean(x * x, axis=-1, keepdims=True) + EPS) * g


def _log_sigmoid(x):
    return jnp.minimum(x, 0.0) - jnp.log1p(jnp.exp(-jnp.abs(x)))


def _dot(a, b):
    return jnp.dot(a, b, preferred_element_type=F32)


def _dot_nt(a, b, precision=None):
    return lax.dot_general(a, b, (((1,), (1,)), ((), ())), preferred_element_type=F32, precision=precision)


def _dot_tn(a, b):
    return lax.dot_general(a, b, (((0,), (0,)), ((), ())), preferred_element_type=F32)


def _ffn_body(x_ref, g_ref, wg_ref, wu_ref, wd_ref, o_ref, xn_ref, nj, side_job=None, bf16_copies=None):
    j = pl.program_id(1)

    @pl.when(j == 0)
    def _():
        xn_ref[...] = _rms(x_ref[...], g_ref[...]).astype(BF16)
        o_ref[...] = jnp.zeros_like(o_ref)

    if side_job is not None:
        side_job()
    wg, wu, wd = wg_ref[...], wu_ref[...], wd_ref[...]
    if bf16_copies is not None:
        wg, wu, wd = wg.astype(BF16), wu.astype(BF16), wd.astype(BF16)
        for ref, w in zip(bf16_copies, (wg, wu, wd)):
            ref[...] = w
    xn = xn_ref[...]
    a = _dot(xn, wg)
    b = _dot(xn, wu)
    h = (a * jax.nn.sigmoid(a)) * b
    o_ref[...] += _dot(h.astype(BF16), wd)

    @pl.when(j == nj - 1)
    def _():
        o_ref[...] = x_ref[...] + 0.5 * o_ref[...]


def _ffn_cast_kernel(x_ref, g_ref, wg_ref, wu_ref, wd_ref, o_ref, wgb_ref, wub_ref, wdb_ref, xn_ref, *, nj):
    _ffn_body(x_ref, g_ref, wg_ref, wu_ref, wd_ref, o_ref, xn_ref, nj, bf16_copies=(wgb_ref, wub_ref, wdb_ref))


def _ffn_kernel(x_ref, g_ref, wg_ref, wu_ref, wd_ref, o_ref, xn_ref, *, nj):
    _ffn_body(x_ref, g_ref, wg_ref, wu_ref, wd_ref, o_ref, xn_ref, nj)


def _ffn_kmean_kernel(pages_ref, x_ref, g_ref, wg_ref, wu_ref, wd_ref, ck_ref, o_ref, km_ref, xn_ref, pbuf, sem,
                      *, nj, l, group, n_blocks):
    s = pl.program_id(0) * nj + pl.program_id(1)
    n_steps = pl.num_programs(0) * nj

    def block_of(step, g):
        return jnp.minimum(step * group + g, n_blocks - 1)

    def page_copy(step, g, p):
        slot = step % 2
        page = pages_ref[block_of(step, g) * PAGES_PER_BLOCK + p]
        k = g * PAGES_PER_BLOCK + p
        return pltpu.make_async_copy(ck_ref.at[l, page], pbuf.at[slot, k], sem.at[slot, k])

    def start_all(step):
        for g in range(group):
            for p in range(PAGES_PER_BLOCK):
                page_copy(step, g, p).start()

    @pl.when(s == 0)
    def _():
        start_all(s)

    @pl.when(s + 1 < n_steps)
    def _():
        start_all(s + 1)

    def block_means():
        slot = s % 2
        for g in range(group):
            for p in range(PAGES_PER_BLOCK):
                page_copy(s, g, p).wait()
        for g in range(group):
            tot = jnp.zeros((ATT_HEADS, HEAD_DIM), F32)
            for p in range(PAGES_PER_BLOCK):
                tot += jnp.sum(pbuf[slot, g * PAGES_PER_BLOCK + p], axis=0)
            km_ref[block_of(s, g)] = tot * (1.0 / MOBA_BLOCK)

    _ffn_body(x_ref, g_ref, wg_ref, wu_ref, wd_ref, o_ref, xn_ref, nj, side_job=block_means)


def _ffn(x, ln, wg, wu, wd, l, tm, kmean_of=None, cast=False):
    m, d = x.shape
    f = wg.shape[-1]
    tf = 512 if f % 512 == 0 else f
    nj = f // tf
    grid = (m // tm, nj)
    x_in = pl.BlockSpec((tm, d), lambda i, j, *_: (i, 0))
    ln_in = pl.BlockSpec((None, 1, d), lambda i, j, *_: (l, 0, 0))
    w_col = pl.BlockSpec((d, tf), lambda i, j, *_: (0, j))
    w_row = pl.BlockSpec((tf, d), lambda i, j, *_: (j, 0))
    in_specs = [x_in, ln_in, w_col, w_col, w_row]
    x_out = pl.BlockSpec((tm, d), lambda i, j, *_: (i, 0))
    if cast:
        return pl.pallas_call(
            functools.partial(_ffn_cast_kernel, nj=nj),
            out_shape=(jax.ShapeDtypeStruct((m, d), F32), jax.ShapeDtypeStruct((d, f), BF16),
                       jax.ShapeDtypeStruct((d, f), BF16), jax.ShapeDtypeStruct((f, d), BF16)),
            grid=grid,
            in_specs=[x_in, ln_in,
                      pl.BlockSpec((None, d, tf), lambda i, j: (l, 0, j)),
                      pl.BlockSpec((None, d, tf), lambda i, j: (l, 0, j)),
                      pl.BlockSpec((None, tf, d), lambda i, j: (l, j, 0))],
            out_specs=(x_out, w_col, w_col, w_row),
            scratch_shapes=[pltpu.VMEM((tm, d), BF16)],
            compiler_params=_cparams(("arbitrary", "arbitrary")),
            name="ffn_cast",
        )(x, ln, wg, wu, wd)
    if kmean_of is None:
        return pl.pallas_call(
            functools.partial(_ffn_kernel, nj=nj),
            out_shape=jax.ShapeDtypeStruct((m, d), F32),
            grid=grid, in_specs=in_specs, out_specs=x_out,
            scratch_shapes=[pltpu.VMEM((tm, d), BF16)],
            compiler_params=_cparams(("parallel", "arbitrary")),
            name="ffn",
        )(x, ln, wg, wu, wd)
    cache_k, page_table = kmean_of
    n_blocks = page_table.size // PAGES_PER_BLOCK
    group = -(-n_blocks // (grid[0] * grid[1]))
    n_buf = group * PAGES_PER_BLOCK
    return pl.pallas_call(
        functools.partial(_ffn_kmean_kernel, nj=nj, l=l, group=group, n_blocks=n_blocks),
        out_shape=(jax.ShapeDtypeStruct((m, d), F32),
                   jax.ShapeDtypeStruct((n_blocks, ATT_HEADS, HEAD_DIM), F32)),
        grid_spec=pltpu.PrefetchScalarGridSpec(
            num_scalar_prefetch=1, grid=grid,
            in_specs=in_specs + [pl.BlockSpec(memory_space=pl.ANY)],
            out_specs=(x_out, pl.BlockSpec((n_blocks, ATT_HEADS, HEAD_DIM), lambda i, j, pg: (0, 0, 0))),
            scratch_shapes=[pltpu.VMEM((tm, d), BF16),
                            pltpu.VMEM((2, n_buf, PAGE_SIZE, ATT_HEADS, HEAD_DIM), F32),
                            pltpu.SemaphoreType.DMA((2, n_buf))]),
        compiler_params=_cparams(("arbitrary", "arbitrary")),
        name="ffn_kmean",
    )(page_table.reshape(-1), x, ln, wg, wu, wd, cache_k)


def _inproj_kernel(x_ref, g_ref, w_ref, wgate_ref, u_ref, gate_ref, *rest):
    wb_ref, xn_ref = rest if len(rest) == 2 else (None, rest[0])

    @pl.when(pl.program_id(1) == 0)
    def _():
        xn = _rms(x_ref[...], g_ref[...]).astype(BF16)
        xn_ref[...] = xn
        gate_ref[...] = _dot(xn, wgate_ref[...])

    w = w_ref[...]
    if wb_ref is not None:
        w = w.astype(BF16)
        wb_ref[...] = w
    u_ref[...] = _dot(xn_ref[...], w)


def _inproj(x, ln, w, wgate, l, tm, cast=False):
    m, d = x.shape
    tn = 1024
    w_out = pl.BlockSpec((d, tn), lambda i, j: (0, j))
    out_shape = [jax.ShapeDtypeStruct((m, N_MAIN), F32), jax.ShapeDtypeStruct((m, LANES), F32)]
    out_specs = [pl.BlockSpec((tm, tn), lambda i, j: (i, j)), pl.BlockSpec((tm, LANES), lambda i, j: (i, 0))]
    if cast:
        out_shape.append(jax.ShapeDtypeStruct((d, N_MAIN), BF16))
        out_specs.append(w_out)
    return pl.pallas_call(
        _inproj_kernel,
        out_shape=tuple(out_shape),
        grid=(m // tm, N_MAIN // tn),
        in_specs=[
            pl.BlockSpec((tm, d), lambda i, j: (i, 0)),
            pl.BlockSpec((None, 1, d), lambda i, j: (l, 0, 0)),
            pl.BlockSpec((None, d, tn), lambda i, j: (l, 0, j)) if cast else w_out,
            pl.BlockSpec((None, d, LANES), lambda i, j: (l, 0, 0)),
        ],
        out_specs=tuple(out_specs),
        scratch_shapes=[pltpu.VMEM((tm, d), BF16)],
        compiler_params=_cparams(("arbitrary", "arbitrary") if cast else ("parallel", "arbitrary")),
        name="inproj_cast" if cast else "inproj",
    )(x, ln, w, wgate)


def _outproj_kernel(x_ref, a_ref, b_ref, c_ref, w_ref, o_ref):
    acc = _dot(a_ref[...], w_ref[0:HG_WIDTH, :])
    acc += _dot(b_ref[...], w_ref[HG_WIDTH:HG_WIDTH + ATT_WIDTH, :])
    acc += _dot(c_ref[...], w_ref[HG_WIDTH + ATT_WIDTH:, :])
    o_ref[...] = x_ref[...] + acc


def _outproj_cast_kernel(x_ref, a_ref, b_ref, c_ref, w_ref, o_ref, wb_ref):
    w = w_ref[...].astype(BF16)
    wb_ref[...] = w
    acc = _dot(a_ref[...], w[0:HG_WIDTH, :])
    acc += _dot(b_ref[...], w[HG_WIDTH:HG_WIDTH + ATT_WIDTH, :])
    acc += _dot(c_ref[...], w[HG_WIDTH + ATT_WIDTH:, :])
    o_ref[...] = x_ref[...] + acc


def _outproj(x, o_hg, o_att, o_ml, w, l, tm, cast=False):
    m, d = x.shape
    dm = w.shape[-2]
    if cast:
        tn = min(512, d)
        row = lambda width: pl.BlockSpec((tm, width), lambda i, n: (i, 0))
        return pl.pallas_call(
            _outproj_cast_kernel,
            out_shape=(jax.ShapeDtypeStruct((m, d), F32), jax.ShapeDtypeStruct((dm, d), BF16)),
            grid=(m // tm, d // tn),
            in_specs=[pl.BlockSpec((tm, tn), lambda i, n: (i, n)), row(HG_WIDTH), row(ATT_WIDTH), row(ML_WIDTH),
                      pl.BlockSpec((None, dm, tn), lambda i, n: (l, 0, n))],
            out_specs=(pl.BlockSpec((tm, tn), lambda i, n: (i, n)), pl.BlockSpec((dm, tn), lambda i, n: (0, n))),
            compiler_params=_cparams(("arbitrary", "arbitrary")),
            name="outproj_cast",
        )(x, o_hg, o_att, o_ml, w)
    return pl.pallas_call(
        _outproj_kernel,
        out_shape=jax.ShapeDtypeStruct((m, d), F32),
        grid=(m // tm,),
        in_specs=[
            pl.BlockSpec((tm, d), lambda i: (i, 0)),
            pl.BlockSpec((tm, HG_WIDTH), lambda i: (i, 0)),
            pl.BlockSpec((tm, ATT_WIDTH), lambda i: (i, 0)),
            pl.BlockSpec((tm, ML_WIDTH), lambda i: (i, 0)),
            pl.BlockSpec((dm, d), lambda i: (0, 0)),
        ],
        out_specs=pl.BlockSpec((tm, d), lambda i: (i, 0)),
        compiler_params=_cparams(("parallel",)),
        name="outproj",
    )(x, o_hg, o_att, o_ml, w)


def _final_norm_kernel(x_ref, g_ref, o_ref):
    o_ref[...] = _rms(x_ref[...], g_ref[...])


def _final_norm(x, g, tm):
    m, d = x.shape
    return pl.pallas_call(
        _final_norm_kernel,
        out_shape=jax.ShapeDtypeStruct((m, d), F32),
        grid=(m // tm,),
        in_specs=[pl.BlockSpec((tm, d), lambda i: (i, 0)), pl.BlockSpec((1, d), lambda i: (0, 0))],
        out_specs=pl.BlockSpec((tm, d), lambda i: (i, 0)),
        compiler_params=_cparams(("parallel",)),
        name="final_norm",
    )(x, g)


def _hgrn_kernel(uq_ref, uf_ref, ui_ref, ug_ref, llb_ref, l1m_ref, oml_ref, norm_ref, s0_ref,
                 o_ref, sout_ref, st_ref, oacc_ref, qd_ref, kd_ref, dl_ref, *, tt, t_valid, t_total):
    c = HG_CHUNK
    ti = pl.program_id(1)
    nt = pl.num_programs(1)

    @pl.when(ti == 0)
    def _():
        for h in range(HG_HEADS):
            st_ref[h] = s0_ref[0, h].T

    rr = min(tt, LANES)
    nc = rr // c
    row = lax.broadcasted_iota(jnp.int32, (rr, rr), 0)
    col = lax.broadcasted_iota(jnp.int32, (rr, rr), 1)
    same_chunk = (row // c) == (col // c)
    tril_bd = jnp.where(same_chunk, jnp.where(col <= row, 1.0, 0.0), 0.0)
    ones_bd = jnp.where(same_chunk, 1.0, 0.0)
    s_idx = lax.broadcasted_iota(jnp.int32, (nc, c, HEAD_DIM), 1)
    sel_r = lax.broadcasted_iota(jnp.int32, (rr, c * rr), 0)
    sel_c = lax.broadcasted_iota(jnp.int32, (rr, c * rr), 1)
    sel_big = jnp.where((sel_c // rr) == (sel_r % c),
                        jnp.where(((sel_c % rr) // c) == (sel_r // c), 1.0, 0.0), 0.0).astype(BF16)
    ones = jnp.ones((HEAD_DIM, HEAD_DIM), BF16)
    llb = llb_ref[...]
    l1m = l1m_ref[...]
    oml = oml_ref[...]

    def intra(si, carry):
        r0 = pl.multiple_of(si * rr, rr)
        uf = uf_ref[pl.ds(r0, rr), :]
        b = l1m + _log_sigmoid(uf)
        lf = jnp.maximum(llb, b) + jnp.log1p(jnp.exp(-jnp.abs(llb - b)))
        kk = oml * jax.nn.sigmoid(-uf)
        if t_valid < t_total:
            valid = (ti * tt + r0 + lax.broadcasted_iota(jnp.int32, (rr, HG_WIDTH), 0)) < t_valid
            lf = jnp.where(valid, lf, 0.0)
            kk = jnp.where(valid, kk, 0.0)
        cum = jnp.dot(tril_bd, lf, preferred_element_type=F32, precision=_HI)
        tot = jnp.dot(ones_bd, lf, preferred_element_type=F32, precision=_HI)
        uq = uq_ref[pl.ds(r0, rr), :]
        ui = ui_ref[pl.ds(r0, rr), :]
        qd_ref[pl.ds(r0, rr), :] = (uq * jnp.exp(cum)).astype(BF16)
        kd_ref[pl.ds(r0, rr), :] = (kk * jnp.exp(tot - cum)).astype(BF16)
        dl_ref[pl.ds(r0, rr), :] = jnp.exp(tot)
        for h in range(HG_HEADS):
            hs = slice(h * HEAD_DIM, (h + 1) * HEAD_DIM)
            q3 = uq[:, hs].reshape(nc, c, HEAD_DIM)
            k3 = kk[:, hs].reshape(nc, c, HEAD_DIM)
            cm3 = cum[:, hs].reshape(nc, c, HEAD_DIM)
            ws = []
            for t in range(c):
                dec = jnp.exp(jnp.broadcast_to(cm3[:, t:t + 1, :], cm3.shape) - cm3)
                w = (jnp.broadcast_to(q3[:, t:t + 1, :], q3.shape) * k3) * dec
                ws.append(jnp.where(s_idx <= t, w, 0.0).reshape(rr, HEAD_DIM))
            w_all = jnp.concatenate(ws, axis=0).astype(BF16)
            r = _dot(w_all, ones)
            vt = jnp.concatenate([ui[:, hs]] * c, axis=0)
            oacc_ref[pl.ds(r0, rr), hs] = _dot(sel_big, (r * vt).astype(BF16))
        return carry

    lax.fori_loop(0, tt // rr, intra, 0)

    def inter(ci, carry):
        r0 = pl.multiple_of(ci * c, c)
        for h in range(HG_HEADS):
            hs = slice(h * HEAD_DIM, (h + 1) * HEAD_DIM)
            st = st_ref[h]
            oacc_ref[pl.ds(r0, c), hs] += _dot_nt(qd_ref[pl.ds(r0, c), hs], st.astype(BF16))
            st_ref[h] = st * dl_ref[pl.ds(r0, 1), hs] + _dot_tn(ui_ref[pl.ds(r0, c), hs].astype(BF16),
                                                                 kd_ref[pl.ds(r0, c), hs])
        return carry

    lax.fori_loop(0, tt // c, inter, 0, unroll=min(4, tt // c))

    for h in range(HG_HEADS):
        hs = slice(h * HEAD_DIM, (h + 1) * HEAD_DIM)
        g = ug_ref[:, hs]
        y = _rms(oacc_ref[:, hs], norm_ref[:, hs]) * (g * jax.nn.sigmoid(g))
        o_ref[:, hs] = y.astype(o_ref.dtype)

    @pl.when(ti == nt - 1)
    def _():
        for h in range(HG_HEADS):
            sout_ref[0, h] = st_ref[h].T


def _hgrn(u, llb, l1m, oml, norm, s0, l, n_seq, t_total, t_valid):
    tt = min(t_total, 512)
    nt = t_total // tt
    cb = HG_WIDTH
    ublock = lambda k: pl.BlockSpec((tt, cb), lambda b, t: (b * nt + t, k))
    par = pl.BlockSpec((None, 1, cb), lambda b, t: (l, 0, 0))
    sblock = pl.BlockSpec((1, HG_HEADS, HEAD_DIM, HEAD_DIM), lambda b, t: (b, 0, 0, 0))
    return pl.pallas_call(
        functools.partial(_hgrn_kernel, tt=tt, t_valid=t_valid, t_total=t_total),
        out_shape=(jax.ShapeDtypeStruct((n_seq * t_total, cb), BF16),
                   jax.ShapeDtypeStruct((n_seq, HG_HEADS, HEAD_DIM, HEAD_DIM), F32)),
        grid=(n_seq, nt),
        in_specs=[ublock(OFF_HQ // cb), ublock(OFF_HF // cb), ublock(OFF_HI // cb), ublock(OFF_HG // cb),
                  par, par, par, par, sblock],
        out_specs=(pl.BlockSpec((tt, cb), lambda b, t: (b * nt + t, 0)), sblock),
        scratch_shapes=[pltpu.VMEM((HG_HEADS, HEAD_DIM, HEAD_DIM), F32), pltpu.VMEM((tt, cb), F32),
                        pltpu.VMEM((tt, cb), BF16), pltpu.VMEM((tt, cb), BF16), pltpu.VMEM((tt, cb), F32)],
        compiler_params=_cparams(("parallel", "arbitrary")),
        name="hgrn",
    )(u, u, u, u, llb, l1m, oml, norm, s0)


def _mlstm_kernel(qk_ref, v_ref, og_ref, gate_ref, cw_ref, cb_ref, gb_ref, norm_ref, c0_ref, n0_ref, m0_ref,
                  conv0_ref, o_ref, cout_ref, nout_ref, mout_ref, c_ref, n_ref, m_ref, carry_ref,
                  *, ll, t_valid, t_total):
    ti = pl.program_id(1)
    nt = pl.num_programs(1)

    @pl.when(ti == 0)
    def _():
        c_ref[...] = c0_ref[0]
        n_ref[...] = n0_ref[0]
        m_ref[...] = m0_ref[0]
        carry_ref[...] = conv0_ref[0]

    x = qk_ref[...]
    xe = jnp.concatenate([carry_ref[...], x], axis=0)
    cw = cw_ref[...]
    y = cb_ref[...] + cw[3:4, :] * x
    for j in range(1, CONV_W):
        y += cw[3 - j:4 - j, :] * xe[SUBLANES - j:SUBLANES - j + ll, :]
    carry_ref[...] = x[ll - SUBLANES:, :]
    qk = y * jax.nn.sigmoid(y)

    g = gate_ref[...] + gb_ref[...]
    lf = _log_sigmoid(g)
    ipre = g
    if t_valid < t_total:
        valid = (ti * ll + lax.broadcasted_iota(jnp.int32, (ll, LANES), 0)) < t_valid
        lf = jnp.where(valid, lf, 0.0)
        ipre = jnp.where(valid, ipre, NEG_BIG)
    row = lax.broadcasted_iota(jnp.int32, (ll, ll), 0)
    col = lax.broadcasted_iota(jnp.int32, (ll, ll), 1)
    causal = col <= row
    cum = jnp.dot(causal.astype(F32), lf, preferred_element_type=F32, precision=_HI)
    lane = lax.broadcasted_iota(jnp.int32, (ll, LANES), 1)
    a_t = jnp.where(lane < ML_HEADS, ipre, cum).T

    for h in range(ML_HEADS):
        hs = slice(h * HEAD_DIM, (h + 1) * HEAD_DIM)
        q = qk[:, hs]
        k = qk[:, ML_WIDTH + h * HEAD_DIM:ML_WIDTH + (h + 1) * HEAD_DIM] * HEAD_DIM ** -0.5
        v = v_ref[:, hs]
        col_cum = cum[:, ML_HEADS + h:ML_HEADS + h + 1]
        col_i = ipre[:, h:h + 1]
        row_cum = a_t[ML_HEADS + h:ML_HEADS + h + 1, :]
        row_i = a_t[h:h + 1, :]
        m_prev = m_ref[h:h + 1, 0:1]
        log_d = jnp.where(causal, col_cum - row_cum + row_i, -jnp.inf)
        m_inter = col_cum + m_prev
        m_t = jnp.maximum(m_inter, jnp.max(log_d, axis=-1, keepdims=True))
        w_inter = jnp.exp(m_inter - m_t)
        qb, kb, vb = q.astype(BF16), k.astype(BF16), v.astype(BF16)
        s = _dot_nt(qb, kb) * jnp.exp(log_d - m_t)
        c_old = c_ref[h]
        n_old = n_ref[h:h + 1, :]
        num = w_inter * _dot(qb, c_old.astype(BF16)) + _dot(s.astype(BF16), vb)
        den = w_inter * jnp.sum(q * n_old, axis=-1, keepdims=True) + jnp.sum(s, axis=-1, keepdims=True)
        hh = num / jnp.maximum(jnp.abs(den), jnp.exp(-m_t))
        m_new = m_t[ll - 1:ll, :]
        cum_last = col_cum[ll - 1:ll, :]
        w_k = jnp.exp(cum_last - col_cum + col_i - m_new)
        decay = jnp.exp(cum_last + m_prev - m_new)
        kw = k * w_k
        c_ref[h] = decay * c_old + _dot_tn(kw.astype(BF16), vb)
        n_ref[h:h + 1, :] = decay * n_old + jnp.sum(kw, axis=0, keepdims=True)
        m_ref[h:h + 1, :] = jnp.broadcast_to(m_new, (1, LANES))
        og = og_ref[:, hs]
        o_ref[:, hs] = (_rms(hh, norm_ref[:, hs]) * jax.nn.sigmoid(og)).astype(o_ref.dtype)

    @pl.when(ti == nt - 1)
    def _():
        cout_ref[0] = c_ref[...]
        nout_ref[0] = n_ref[...]
        mout_ref[0] = m_ref[...]


def _mlstm(u, gates, cw, cb, gb, norm, c0, n0, m0, conv0, l, n_seq, t_total, t_valid, ll):
    nt = t_total // ll
    ublock = lambda w, k: pl.BlockSpec((ll, w), lambda b, t: (b * nt + t, k))
    par = lambda r, w: pl.BlockSpec((None, r, w), lambda b, t: (l, 0, 0))
    cblock = pl.BlockSpec((1, ML_HEADS, HEAD_DIM, HEAD_DIM), lambda b, t: (b, 0, 0, 0))
    vblock = pl.BlockSpec((1, SUBLANES, LANES), lambda b, t: (b, 0, 0))
    return pl.pallas_call(
        functools.partial(_mlstm_kernel, ll=ll, t_valid=t_valid, t_total=t_total),
        out_shape=(jax.ShapeDtypeStruct((n_seq * t_total, ML_WIDTH), BF16),
                   jax.ShapeDtypeStruct((n_seq, ML_HEADS, HEAD_DIM, HEAD_DIM), F32),
                   jax.ShapeDtypeStruct((n_seq, SUBLANES, LANES), F32),
                   jax.ShapeDtypeStruct((n_seq, SUBLANES, LANES), F32)),
        grid=(n_seq, nt),
        in_specs=[ublock(2 * ML_WIDTH, OFF_MQK // (2 * ML_WIDTH)), ublock(ML_WIDTH, OFF_MV // ML_WIDTH),
                  ublock(ML_WIDTH, OFF_MO // ML_WIDTH),
                  pl.BlockSpec((ll, LANES), lambda b, t: (b * nt + t, 0)),
                  par(CONV_W, 2 * ML_WIDTH), par(1, 2 * ML_WIDTH), par(1, LANES), par(1, ML_WIDTH),
                  cblock, vblock, vblock,
                  pl.BlockSpec((1, SUBLANES, 2 * ML_WIDTH), lambda b, t: (b, 0, 0))],
        out_specs=(pl.BlockSpec((ll, ML_WIDTH), lambda b, t: (b * nt + t, 0)), cblock, vblock, vblock),
        scratch_shapes=[pltpu.VMEM((ML_HEADS, HEAD_DIM, HEAD_DIM), F32), pltpu.VMEM((SUBLANES, LANES), F32),
                        pltpu.VMEM((SUBLANES, LANES), F32), pltpu.VMEM((SUBLANES, 2 * ML_WIDTH), F32)],
        compiler_params=_cparams(("parallel", "arbitrary")),
        name="mlstm",
    )(u, u, u, gates, cw, cb, gb, norm, c0, n0, m0, conv0)


def _t5_bucket_np(rel):
    rel = np.asarray(rel, np.int64)
    max_exact = N_BUCKETS // 2
    relf = np.maximum(rel, 1).astype(np.float64)
    large = max_exact + (np.log(relf / max_exact) / math.log(MAX_DISTANCE / max_exact)
                         * (N_BUCKETS - max_exact)).astype(np.int64)
    return np.where(rel < max_exact, rel, np.minimum(large, N_BUCKETS - 1)).astype(np.int32)


def _t5_thresholds(max_rel):
    buckets = _t5_bucket_np(np.arange(max_rel + 1))
    out = []
    for b in range(1, N_BUCKETS):
        hit = np.nonzero(buckets >= b)[0]
        out.append(int(hit[0]) if hit.size else None)
    return out


def _t5_bias_kernel(tab_ref, o_ref, *, rel_fn, thresholds):
    rel = rel_fn(o_ref.shape[1:])
    acc = [jnp.full(o_ref.shape[1:], tab_ref[h, 0], F32) for h in range(ATT_HEADS)]
    for b, th in enumerate(thresholds, start=1):
        if th is None:
            continue
        reached = rel >= th
        for h in range(ATT_HEADS):
            acc[h] = jnp.where(reached, tab_ref[h, b], acc[h])
    for h in range(ATT_HEADS):
        o_ref[h] = acc[h]


def _t5_bias_prompt(bias_tab, t_total):
    blk = MOBA_BLOCK
    nb = t_total // blk

    def rel_fn(shape):
        return (pl.program_id(0) * blk + lax.broadcasted_iota(jnp.int32, shape, 1)
                - lax.broadcasted_iota(jnp.int32, shape, 0))

    return pl.pallas_call(
        functools.partial(_t5_bias_kernel, rel_fn=rel_fn, thresholds=_t5_thresholds(t_total)),
        out_shape=jax.ShapeDtypeStruct((ATT_HEADS, blk, nb * blk), F32),
        grid=(nb,),
        in_specs=[pl.BlockSpec(memory_space=pltpu.SMEM)],
        out_specs=pl.BlockSpec((ATT_HEADS, blk, blk), lambda d: (0, 0, d)),
        compiler_params=_cparams(("parallel",)),
        name="t5_bias_prompt",
    )(bias_tab)


def _t5_bias_sample(bias_tab, past_len):
    n_pages = past_len // PAGE_SIZE

    def rel_fn(shape):
        return past_len - (lax.broadcasted_iota(jnp.int32, shape, 0) * PAGE_SIZE
                           + lax.broadcasted_iota(jnp.int32, shape, 1))

    return pl.pallas_call(
        functools.partial(_t5_bias_kernel, rel_fn=rel_fn, thresholds=_t5_thresholds(past_len)),
        out_shape=jax.ShapeDtypeStruct((ATT_HEADS, n_pages, PAGE_SIZE), F32),
        grid=(1,),
        in_specs=[pl.BlockSpec(memory_space=pltpu.SMEM)],
        out_specs=pl.BlockSpec((ATT_HEADS, n_pages, PAGE_SIZE), lambda i: (0, 0, 0)),
        compiler_params=_cparams(("arbitrary",)),
        name="t5_bias_sample",
    )(bias_tab)


def _moba_prefill_kernel(q_ref, k_ref, v_ref, bias_ref, o_ref, vt_ref, m_ref, l_ref, acc_ref, *, nb):
    blk = MOBA_BLOCK
    t_total = nb * blk
    scale = HEAD_DIM ** -0.5
    q = q_ref[...]
    k = k_ref[...]
    qb = q.astype(BF16)
    kb = k.astype(BF16)
    vt_ref[...] = v_ref[...].T.astype(BF16)

    kmean = jnp.concatenate(
        [jnp.mean(k[n * blk:(n + 1) * blk, :], axis=0, keepdims=True) for n in range(nb)]
        + [jnp.zeros((SUBLANES - nb % SUBLANES, HEAD_DIM), F32)] * (nb % SUBLANES != 0), axis=0)
    nrow = kmean.shape[0]
    gate = _dot_nt(kmean, q, precision=_HI)
    brow = lax.broadcasted_iota(jnp.int32, (nrow, t_total), 0)
    qblk = lax.broadcasted_iota(jnp.int32, (nrow, t_total), 1) // blk
    qblk_row = qblk[0:1, :]
    chosen = []
    for n in range(nb - 1):
        gn = gate[n:n + 1, :]
        ahead = jnp.where(brow < n, jnp.where(gate >= gn, 1.0, 0.0), jnp.where(gate > gn, 1.0, 0.0))
        rank = jnp.sum(jnp.where(brow < qblk, ahead, 0.0), axis=0, keepdims=True)
        chosen.append(jnp.where(qblk_row > n, jnp.where(rank < MOBA_TOPK, 1.0, 0.0), 0.0))

    krow = lax.broadcasted_iota(jnp.int32, (blk, blk), 0)
    qcol = lax.broadcasted_iota(jnp.int32, (blk, blk), 1)
    for i in range(nb):
        rs = slice(i * blk, (i + 1) * blk)
        lg = _dot_nt(kb[rs], qb[rs]) * scale + bias_ref[:, 0:blk]
        lg = jnp.where(krow <= qcol, lg, -jnp.inf)
        m0 = jnp.max(lg, axis=0, keepdims=True)
        p = jnp.exp(lg - m0)
        m_ref[:, rs] = m0
        l_ref[:, rs] = jnp.sum(p, axis=0, keepdims=True)
        acc_ref[:, rs] = _dot(vt_ref[:, rs], p.astype(BF16))

    for n in range(nb - 1):
        ks = slice(n * blk, (n + 1) * blk)
        qs = slice((n + 1) * blk, t_total)
        nq = t_total - (n + 1) * blk
        lg = _dot_nt(kb[ks], qb[qs]) * scale + bias_ref[:, blk:blk + nq]
        lg = jnp.where(chosen[n][:, qs] > 0.0, lg, -jnp.inf)
        m_old = m_ref[:, qs]
        m_new = jnp.maximum(m_old, jnp.max(lg, axis=0, keepdims=True))
        alpha = jnp.exp(m_old - m_new)
        p = jnp.exp(lg - m_new)
        m_ref[:, qs] = m_new
        l_ref[:, qs] = alpha * l_ref[:, qs] + jnp.sum(p, axis=0, keepdims=True)
        acc_ref[:, qs] = alpha * acc_ref[:, qs] + _dot(vt_ref[:, ks], p.astype(BF16))

    o_ref[...] = (acc_ref[...] / l_ref[...]).T.astype(o_ref.dtype)


def _moba_prefill(u, bias, n_seq, t_total):
    blk = MOBA_BLOCK
    nb = t_total // blk
    hd = HEAD_DIM
    tok = lambda off: pl.BlockSpec((t_total, hd), lambda h, b: (b, off // hd + h))
    return pl.pallas_call(
        functools.partial(_moba_prefill_kernel, nb=nb),
        out_shape=jax.ShapeDtypeStruct((n_seq * t_total, ATT_WIDTH), BF16),
        grid=(ATT_HEADS, n_seq),
        in_specs=[tok(OFF_AQ), tok(OFF_AK), tok(OFF_AV),
                  pl.BlockSpec((None, blk, nb * blk), lambda h, b: (h, 0, 0))],
        out_specs=pl.BlockSpec((t_total, hd), lambda h, b: (b, h)),
        scratch_shapes=[pltpu.VMEM((hd, t_total), BF16), pltpu.VMEM((1, t_total), F32),
                        pltpu.VMEM((1, t_total), F32), pltpu.VMEM((hd, t_total), F32)],
        compiler_params=_cparams(("parallel", "parallel")),
        name="moba_prefill",
    )(u, u, u, bias)


def _moba_select_kernel(km_ref, q_ref, o_ref, *, nblk):
    g = jnp.sum(km_ref[0] * q_ref[0][None], axis=-1)
    idx = lax.broadcasted_iota(jnp.int32, (nblk, ATT_HEADS), 0).astype(F32)
    o_ref[...] = jnp.zeros_like(o_ref)
    for j in range(MOBA_TOPK):
        mx = jnp.max(g, axis=0, keepdims=True)
        first = jnp.min(jnp.where(g == mx, idx, float(nblk)), axis=0, keepdims=True)
        o_ref[0, j:j + 1, :] = first.astype(jnp.int32)
        g = jnp.where(idx == first, -jnp.inf, g)


def _moba_select(kmean, q3):
    n_seq, nblk = kmean.shape[:2]
    return pl.pallas_call(
        functools.partial(_moba_select_kernel, nblk=nblk),
        out_shape=jax.ShapeDtypeStruct((n_seq, SUBLANES, ATT_HEADS), jnp.int32),
        grid=(n_seq,),
        in_specs=[pl.BlockSpec((1, nblk, ATT_HEADS, HEAD_DIM), lambda b: (b, 0, 0, 0)),
                  pl.BlockSpec((1, ATT_HEADS, HEAD_DIM), lambda b: (b, 0, 0))],
        out_specs=pl.BlockSpec((1, SUBLANES, ATT_HEADS), lambda b: (b, 0, 0)),
        compiler_params=_cparams(("parallel",)),
        name="moba_select",
    )(kmean, q3)


def _moba_decode_kernel(phys_ref, logi_ref, q_ref, kn_ref, vn_ref, bias_ref, bias0_ref, ck_ref, cv_ref, o_ref,
                        kbuf, vbuf, sem, *, l):
    b = pl.program_id(0)
    scale = HEAD_DIM ** -0.5

    def page_copy(which, h, j):
        slot = h * N_SEL_PAGES + j
        pg = phys_ref[(b * ATT_HEADS + h) * N_SEL_PAGES + j]
        src, dst = ((ck_ref, kbuf), (cv_ref, vbuf))[which]
        return pltpu.make_async_copy(src.at[l, pg, :, h, :], dst.at[slot], sem.at[which, slot])

    for h in range(ATT_HEADS):
        for j in range(N_SEL_PAGES):
            page_copy(0, h, j).start()
            page_copy(1, h, j).start()

    for h in range(ATT_HEADS):
        q = q_ref[0, h:h + 1, :]
        q8 = jnp.broadcast_to(q, (SUBLANES, HEAD_DIM)).astype(BF16)
        logits = []
        for j in range(N_SEL_PAGES):
            page_copy(0, h, j).wait()
            lp = logi_ref[(b * ATT_HEADS + h) * N_SEL_PAGES + j]
            logits.append(_dot_nt(q8, kbuf[h * N_SEL_PAGES + j].astype(BF16)) * scale
                          + bias_ref[h, pl.ds(lp, 1), :])
        self_logit = jnp.sum(q * kn_ref[0, h:h + 1, :], axis=-1, keepdims=True) * scale + bias0_ref[h:h + 1, 0:1]
        m = self_logit
        for lg in logits:
            m = jnp.maximum(m, jnp.max(lg, axis=-1, keepdims=True))
        p_self = jnp.exp(self_logit - m)
        den = p_self
        acc = p_self * vn_ref[0, h:h + 1, :]
        for j in range(N_SEL_PAGES):
            page_copy(1, h, j).wait()
            p = jnp.exp(logits[j] - m)
            den += jnp.sum(p, axis=-1, keepdims=True)
            acc += _dot(p.astype(BF16), vbuf[h * N_SEL_PAGES + j].astype(BF16))
        o_ref[0, h:h + 1, :] = (acc / den)[0:1, :]


def _moba_decode(cache_k, cache_v, phys, logi, q3, k3, v3, bias_s, bias0, l):
    n_seq = q3.shape[0]
    n_slots = ATT_HEADS * N_SEL_PAGES
    tok = pl.BlockSpec((1, ATT_HEADS, HEAD_DIM), lambda b, ph, lg: (b, 0, 0))
    return pl.pallas_call(
        functools.partial(_moba_decode_kernel, l=l),
        out_shape=jax.ShapeDtypeStruct((n_seq, ATT_HEADS, HEAD_DIM), F32),
        grid_spec=pltpu.PrefetchScalarGridSpec(
            num_scalar_prefetch=2, grid=(n_seq,),
            in_specs=[tok, tok, tok,
                      pl.BlockSpec(bias_s.shape, lambda b, ph, lg: (0, 0, 0)),
                      pl.BlockSpec(bias0.shape, lambda b, ph, lg: (0, 0)),
                      pl.BlockSpec(memory_space=pl.ANY), pl.BlockSpec(memory_space=pl.ANY)],
            out_specs=tok,
            scratch_shapes=[pltpu.VMEM((n_slots, PAGE_SIZE, HEAD_DIM), F32),
                            pltpu.VMEM((n_slots, PAGE_SIZE, HEAD_DIM), F32),
                            pltpu.SemaphoreType.DMA((2, n_slots))]),
        compiler_params=_cparams(("arbitrary",)),
        name="moba_decode",
    )(phys, logi, q3, k3, v3, bias_s, bias0, cache_k, cache_v)


def _pad_rows(a, rows):
    n, w = a.shape
    return jnp.pad(a[:, None, :], ((0, 0), (0, rows - 1), (0, 0))).reshape(n * rows, w)


def _vec_state(a):
    if a.ndim == 2:
        a = jnp.broadcast_to(a[:, :, None], a.shape + (LANES,))
    return jnp.pad(a, ((0, 0), (0, SUBLANES - a.shape[1]), (0, 0)))


def kernel(x_prompt, x_sample, cache_k, cache_v, page_table, state_hgrn, state_mlstm_c, state_mlstm_n,
           state_mlstm_m, state_mlstm_conv, ln_ffn1, w_ffn1_gate, w_ffn1_up, w_ffn1_down, ln_mix, w_in, w_out,
           hgrn_lb_logits, hgrn_out_norm, rel_bias, mlstm_conv_w, mlstm_conv_b, mlstm_gate_bias,
           mlstm_out_norm, ln_ffn2, w_ffn2_gate, w_ffn2_up, w_ffn2_down, ln_final):
    depth = w_in.shape[0]
    bp, tp, d = x_prompt.shape
    bs, ts, _ = x_sample.shape
    assert ts == 1 and tp % MOBA_BLOCK == 0
    n_pages = page_table.shape[1]
    past_len = n_pages * PAGE_SIZE
    assert past_len % MOBA_BLOCK == 0 and past_len // MOBA_BLOCK >= MOBA_TOPK

    w_gate_b = jnp.pad(w_in[:, :, N_MAIN:], ((0, 0), (0, 0), (0, LANES - N_GATE))).astype(BF16)
    row3 = lambda a: a.reshape(depth, 1, -1)
    ln1, lnm, ln2 = row3(ln_ffn1), row3(ln_mix), row3(ln_ffn2)
    lb = jnp.cumsum(jax.nn.softmax(hgrn_lb_logits.astype(F32), axis=0), axis=0)
    lb = lb - lb[0]
    llb, l1m, oml = row3(jnp.log(lb)), row3(jnp.log1p(-lb)), row3(1.0 - lb)
    hnorm, mnorm = row3(hgrn_out_norm), row3(mlstm_out_norm)
    conv_b = row3(mlstm_conv_b)
    gate_b = row3(jnp.pad(mlstm_gate_bias, ((0, 0), (0, LANES - N_GATE))))
    bias_tab = rel_bias.T.astype(F32)
    bias_p = _t5_bias_prompt(bias_tab, tp)
    bias_s = _t5_bias_sample(bias_tab, past_len)
    bias0 = jnp.broadcast_to(bias_tab[:, 0:1], (ATT_HEADS, LANES))

    zeros = lambda *s: jnp.zeros(s, F32)
    tm_f = 512
    tm_p = 1024 if (bp * tp) % 1024 == 0 else 512
    tm_o = 512
    ll_p = 256

    xp = x_prompt.reshape(bp * tp, d)
    xs = x_sample.reshape(bs, d)
    outs = {k: [] for k in ("ks", "vs", "hgp", "hgs", "cp", "cs", "np", "ns", "mp", "ms", "cvp", "cvs")}
    u_all = []

    for l in range(depth):
        xs, wg1, wu1, wd1 = _ffn(xs, ln1, w_ffn1_gate, w_ffn1_up, w_ffn1_down, l, bs, cast=True)
        xp, kmean = _ffn(xp, ln1, wg1, wu1, wd1, l, tm_f, kmean_of=(cache_k, page_table))
        us, gs, w_in_b = _inproj(xs, lnm, w_in, w_gate_b, l, bs, cast=True)
        u, gates = _inproj(xp, lnm, w_in_b, w_gate_b, l, tm_p)

        o_hg, hg_s = _hgrn(u, llb, l1m, oml, hnorm, zeros(bp, HG_HEADS, HEAD_DIM, HEAD_DIM), l, bp, tp, tp)
        o_att = _moba_prefill(u, bias_p, bp, tp)
        o_ml, c_s, n_s, m_s = _mlstm(u, gates, mlstm_conv_w, conv_b, gate_b, mnorm,
                                     zeros(bp, ML_HEADS, HEAD_DIM, HEAD_DIM), zeros(bp, SUBLANES, LANES),
                                     zeros(bp, SUBLANES, LANES), zeros(bp, SUBLANES, 2 * ML_WIDTH),
                                     l, bp, tp, tp, ll_p)
        u3 = u.reshape(bp, tp, N_MAIN)
        u_all.append(u3)
        outs["hgp"].append(hg_s)
        outs["cp"].append(c_s)
        outs["np"].append(n_s[:, :ML_HEADS])
        outs["mp"].append(m_s[:, :ML_HEADS, 0])
        outs["cvp"].append(u3[:, tp - (CONV_W - 1):, OFF_MQK:OFF_MV])

        o_hg_s, hg_ss = _hgrn(_pad_rows(us, HG_CHUNK), llb, l1m, oml, hnorm, state_hgrn[l], l, bs, HG_CHUNK, 1)
        conv0 = jnp.pad(state_mlstm_conv[l], ((0, 0), (SUBLANES - (CONV_W - 1), 0), (0, 0)))
        o_ml_s, c_ss, n_ss, m_ss = _mlstm(_pad_rows(us, SAMPLE_PAD), _pad_rows(gs, SAMPLE_PAD), mlstm_conv_w,
                                          conv_b, gate_b, mnorm, state_mlstm_c[l],
                                          _vec_state(state_mlstm_n[l]), _vec_state(state_mlstm_m[l]), conv0,
                                          l, bs, SAMPLE_PAD, 1, SAMPLE_PAD)
        heads = lambda a: a.reshape(bs, ATT_HEADS, HEAD_DIM)
        q3, k3, v3 = heads(us[:, OFF_AQ:OFF_AK]), heads(us[:, OFF_AK:OFF_AV]), heads(us[:, OFF_AV:OFF_MQK])
        kmean = kmean.reshape(bs, -1, ATT_HEADS, HEAD_DIM)
        sel = _moba_select(kmean, q3)[:, :MOBA_TOPK, :]
        sel = jnp.transpose(sel, (0, 2, 1))
        logi = (sel[..., None] * PAGES_PER_BLOCK
                + jnp.arange(PAGES_PER_BLOCK, dtype=jnp.int32)).reshape(bs, ATT_HEADS, N_SEL_PAGES)
        phys = jnp.take_along_axis(page_table[:, None, :], logi, axis=2)
        o_att_s = _moba_decode(cache_k, cache_v, phys.reshape(-1), logi.reshape(-1), q3, k3, v3,
                               bias_s, bias0, l).reshape(bs, ATT_WIDTH).astype(BF16)
        first = lambda a, rows: a.reshape(bs, rows, -1)[:, 0]

        xs, w_out_b = _outproj(xs, first(o_hg_s, HG_CHUNK), o_att_s, first(o_ml_s, SAMPLE_PAD), w_out, l, bs,
                               cast=True)
        xp = _outproj(xp, o_hg, o_att, o_ml, w_out_b, l, tm_o)
        xs, wg2, wu2, wd2 = _ffn(xs, ln2, w_ffn2_gate, w_ffn2_up, w_ffn2_down, l, bs, cast=True)
        xp = _ffn(xp, ln2, wg2, wu2, wd2, l, tm_f)
        outs["ks"].append(k3.reshape(bs, 1, ATT_HEADS, HEAD_DIM))
        outs["vs"].append(v3.reshape(bs, 1, ATT_HEADS, HEAD_DIM))
        outs["hgs"].append(hg_ss)
        outs["cs"].append(c_ss)
        outs["ns"].append(n_ss[:, :ML_HEADS])
        outs["ms"].append(m_ss[:, :ML_HEADS, 0])
        outs["cvs"].append(jnp.concatenate([state_mlstm_conv[l][:, 1:], us[:, None, OFF_MQK:OFF_MV]], axis=1))

    y_prompt = _final_norm(xp, ln_final.reshape(1, d), tm_o).reshape(bp, tp, d)
    y_sample = _final_norm(xs, ln_final.reshape(1, d), bs).reshape(bs, 1, d)
    st = {k: jnp.stack(v) for k, v in outs.items()}
    heads_p = lambda lo, hi: jnp.stack([u3[:, :, lo:hi] for u3 in u_all]).reshape(depth, bp, tp, ATT_HEADS, HEAD_DIM)
    st["kp"], st["vp"] = heads_p(OFF_AK, OFF_AV), heads_p(OFF_AV, OFF_MQK)
    return (y_prompt, y_sample, st["kp"], st["vp"], st["ks"], st["vs"], st["hgp"], st["hgs"], st["cp"], st["cs"],
            st["np"], st["ns"], st["mp"], st["ms"], st["cvp"], st["cvs"])
```

```python
import functools
import math

import jax
import jax.numpy as jnp
import numpy as np
from jax import lax
from jax.experimental import pallas as pl
from jax.experimental.pallas import tpu as pltpu

F32 = jnp.float32
BF16 = jnp.bfloat16

HEAD_DIM = 128
HG_HEADS = 4
ATT_HEADS = 8
ML_HEADS = 4
HG_WIDTH = HG_HEADS * HEAD_DIM
ATT_WIDTH = ATT_HEADS * HEAD_DIM
ML_WIDTH = ML_HEADS * HEAD_DIM
N_MAIN = 4 * HG_WIDTH + 3 * ATT_WIDTH + 4 * ML_WIDTH
N_GATE = 2 * ML_HEADS
OFF_HQ, OFF_HF, OFF_HI, OFF_HG = 0, HG_WIDTH, 2 * HG_WIDTH, 3 * HG_WIDTH
OFF_AQ = 4 * HG_WIDTH
OFF_AK = OFF_AQ + ATT_WIDTH
OFF_AV = OFF_AK + ATT_WIDTH
OFF_MQK = OFF_AV + ATT_WIDTH
OFF_MV = OFF_MQK + 2 * ML_WIDTH
OFF_MO = OFF_MV + ML_WIDTH

PAGE_SIZE = 128
MOBA_BLOCK = 256
MOBA_TOPK = 3
PAGES_PER_BLOCK = MOBA_BLOCK // PAGE_SIZE
N_SEL_PAGES = MOBA_TOPK * PAGES_PER_BLOCK
N_BUCKETS = 32
MAX_DISTANCE = 4096
HG_CHUNK = 16
CONV_W = 4
EPS = 1e-6
NEG_BIG = -1e30
LANES = 128
SUBLANES = 8
VMEM_LIMIT = 48 * 1024 * 1024
SAMPLE_PAD = 128

_HI = lax.Precision.HIGHEST


def _cparams(sem):
    return pltpu.CompilerParams(dimension_semantics=sem, vmem_limit_bytes=VMEM_LIMIT)


def _rms(x, g):
    return x * lax.rsqrt(jnp.mean(x * x, axis=-1, keepdims=True) + EPS) * g


def _log_sigmoid(x):
    return jnp.minimum(x, 0.0) - jnp.log1p(jnp.exp(-jnp.abs(x)))


def _dot(a, b):
    return jnp.dot(a, b, preferred_element_type=F32)


def _dot_nt(a, b, precision=None):
    return lax.dot_general(a, b, (((1,), (1,)), ((), ())), preferred_element_type=F32, precision=precision)


def _dot_tn(a, b):
    return lax.dot_general(a, b, (((0,), (0,)), ((), ())), preferred_element_type=F32)


def _ffn_body(x_ref, g_ref, wg_ref, wu_ref, wd_ref, o_ref, xn_ref, nj, side_job=None, bf16_copies=None):
    j = pl.program_id(1)

    @pl.when(j == 0)
    def _():
        xn_ref[...] = _rms(x_ref[...], g_ref[...]).astype(BF16)
        o_ref[...] = jnp.zeros_like(o_ref)

    if side_job is not None:
        side_job()
    wg, wu, wd = wg_ref[...], wu_ref[...], wd_ref[...]
    if bf16_copies is not None:
        wg, wu, wd = wg.astype(BF16), wu.astype(BF16), wd.astype(BF16)
        for ref, w in zip(bf16_copies, (wg, wu, wd)):
            ref[...] = w
    xn = xn_ref[...]
    a = _dot(xn, wg)
    b = _dot(xn, wu)
    h = (a * jax.nn.sigmoid(a)) * b
    o_ref[...] += _dot(h.astype(BF16), wd)

    @pl.when(j == nj - 1)
    def _():
        o_ref[...] = x_ref[...] + 0.5 * o_ref[...]


def _ffn_cast_kernel(x_ref, g_ref, wg_ref, wu_ref, wd_ref, o_ref, wgb_ref, wub_ref, wdb_ref, xn_ref, *, nj):
    _ffn_body(x_ref, g_ref, wg_ref, wu_ref, wd_ref, o_ref, xn_ref, nj, bf16_copies=(wgb_ref, wub_ref, wdb_ref))


def _ffn_kernel(x_ref, g_ref, wg_ref, wu_ref, wd_ref, o_ref, xn_ref, *, nj):
    _ffn_body(x_ref, g_ref, wg_ref, wu_ref, wd_ref, o_ref, xn_ref, nj)


def _ffn_kmean_kernel(pages_ref, x_ref, g_ref, wg_ref, wu_ref, wd_ref, ck_ref, o_ref, km_ref, xn_ref, pbuf, sem,
                      *, nj, l, group, n_blocks):
    s = pl.program_id(0) * nj + pl.program_id(1)
    n_steps = pl.num_programs(0) * nj

    def block_of(step, g):
        return jnp.minimum(step * group + g, n_blocks - 1)

    def page_copy(step, g, p):
        slot = step % 2
        page = pages_ref[block_of(step, g) * PAGES_PER_BLOCK + p]
        k = g * PAGES_PER_BLOCK + p
        return pltpu.make_async_copy(ck_ref.at[l, page], pbuf.at[slot, k], sem.at[slot, k])

    def start_all(step):
        for g in range(group):
            for p in range(PAGES_PER_BLOCK):
                page_copy(step, g, p).start()

    @pl.when(s == 0)
    def _():
        start_all(s)

    @pl.when(s + 1 < n_steps)
    def _():
        start_all(s + 1)

    def block_means():
        slot = s % 2
        for g in range(group):
            for p in range(PAGES_PER_BLOCK):
                page_copy(s, g, p).wait()
        for g in range(group):
            tot = jnp.zeros((ATT_HEADS, HEAD_DIM), F32)
            for p in range(PAGES_PER_BLOCK):
                tot += jnp.sum(pbuf[slot, g * PAGES_PER_BLOCK + p], axis=0)
            km_ref[block_of(s, g)] = tot * (1.0 / MOBA_BLOCK)

    _ffn_body(x_ref, g_ref, wg_ref, wu_ref, wd_ref, o_ref, xn_ref, nj, side_job=block_means)


def _ffn(x, ln, wg, wu, wd, l, tm, kmean_of=None, cast=False):
    m, d = x.shape
    f = wg.shape[-1]
    tf = 512 if f % 512 == 0 else f
    nj = f // tf
    grid = (m // tm, nj)
    x_in = pl.BlockSpec((tm, d), lambda i, j, *_: (i, 0))
    ln_in = pl.BlockSpec((None, 1, d), lambda i, j, *_: (l, 0, 0))
    w_col = pl.BlockSpec((d, tf), lambda i, j, *_: (0, j))
    w_row = pl.BlockSpec((tf, d), lambda i, j, *_: (j, 0))
    in_specs = [x_in, ln_in, w_col, w_col, w_row]
    x_out = pl.BlockSpec((tm, d), lambda i, j, *_: (i, 0))
    if cast:
        return pl.pallas_call(
            functools.partial(_ffn_cast_kernel, nj=nj),
            out_shape=(jax.ShapeDtypeStruct((m, d), F32), jax.ShapeDtypeStruct((d, f), BF16),
                       jax.ShapeDtypeStruct((d, f), BF16), jax.ShapeDtypeStruct((f, d), BF16)),
            grid=grid,
            in_specs=[x_in, ln_in,
                      pl.BlockSpec((None, d, tf), lambda i, j: (l, 0, j)),
                      pl.BlockSpec((None, d, tf), lambda i, j: (l, 0, j)),
                      pl.BlockSpec((None, tf, d), lambda i, j: (l, j, 0))],
            out_specs=(x_out, w_col, w_col, w_row),
            scratch_shapes=[pltpu.VMEM((tm, d), BF16)],
            compiler_params=_cparams(("arbitrary", "arbitrary")),
            name="ffn_cast",
        )(x, ln, wg, wu, wd)
    if kmean_of is None:
        return pl.pallas_call(
            functools.partial(_ffn_kernel, nj=nj),
            out_shape=jax.ShapeDtypeStruct((m, d), F32),
            grid=grid, in_specs=in_specs, out_specs=x_out,
            scratch_shapes=[pltpu.VMEM((tm, d), BF16)],
            compiler_params=_cparams(("parallel", "arbitrary")),
            name="ffn",
        )(x, ln, wg, wu, wd)
    cache_k, page_table = kmean_of
    n_blocks = page_table.size // PAGES_PER_BLOCK
    group = -(-n_blocks // (grid[0] * grid[1]))
    n_buf = group * PAGES_PER_BLOCK
    return pl.pallas_call(
        functools.partial(_ffn_kmean_kernel, nj=nj, l=l, group=group, n_blocks=n_blocks),
        out_shape=(jax.ShapeDtypeStruct((m, d), F32),
                   jax.ShapeDtypeStruct((n_blocks, ATT_HEADS, HEAD_DIM), F32)),
        grid_spec=pltpu.PrefetchScalarGridSpec(
            num_scalar_prefetch=1, grid=grid,
            in_specs=in_specs + [pl.BlockSpec(memory_space=pl.ANY)],
            out_specs=(x_out, pl.BlockSpec((n_blocks, ATT_HEADS, HEAD_DIM), lambda i, j, pg: (0, 0, 0))),
            scratch_shapes=[pltpu.VMEM((tm, d), BF16),
                            pltpu.VMEM((2, n_buf, PAGE_SIZE, ATT_HEADS, HEAD_DIM), F32),
                            pltpu.SemaphoreType.DMA((2, n_buf))]),
        compiler_params=_cparams(("arbitrary", "arbitrary")),
        name="ffn_kmean",
    )(page_table.reshape(-1), x, ln, wg, wu, wd, cache_k)


def _inproj_kernel(x_ref, g_ref, w_ref, wgate_ref, u_ref, gate_ref, *rest):
    wb_ref, xn_ref = rest if len(rest) == 2 else (None, rest[0])

    @pl.when(pl.program_id(1) == 0)
    def _():
        xn = _rms(x_ref[...], g_ref[...]).astype(BF16)
        xn_ref[...] = xn
        gate_ref[...] = _dot(xn, wgate_ref[...])

    w = w_ref[...]
    if wb_ref is not None:
        w = w.astype(BF16)
        wb_ref[...] = w
    u_ref[...] = _dot(xn_ref[...], w)


def _inproj(x, ln, w, wgate, l, tm, cast=False):
    m, d = x.shape
    tn = 1024
    w_out = pl.BlockSpec((d, tn), lambda i, j: (0, j))
    out_shape = [jax.ShapeDtypeStruct((m, N_MAIN), F32), jax.ShapeDtypeStruct((m, LANES), F32)]
    out_specs = [pl.BlockSpec((tm, tn), lambda i, j: (i, j)), pl.BlockSpec((tm, LANES), lambda i, j: (i, 0))]
    if cast:
        out_shape.append(jax.ShapeDtypeStruct((d, N_MAIN), BF16))
        out_specs.append(w_out)
    return pl.pallas_call(
        _inproj_kernel,
        out_shape=tuple(out_shape),
        grid=(m // tm, N_MAIN // tn),
        in_specs=[
            pl.BlockSpec((tm, d), lambda i, j: (i, 0)),
            pl.BlockSpec((None, 1, d), lambda i, j: (l, 0, 0)),
            pl.BlockSpec((None, d, tn), lambda i, j: (l, 0, j)) if w.ndim == 3 else w_out,
            pl.BlockSpec((None, d, LANES), lambda i, j: (l, 0, 0)),
        ],
        out_specs=tuple(out_specs),
        scratch_shapes=[pltpu.VMEM((tm, d), BF16)],
        compiler_params=_cparams(("arbitrary", "arbitrary") if cast else ("parallel", "arbitrary")),
        name="inproj_cast" if cast else "inproj",
    )(x, ln, w, wgate)


def _outproj_kernel(x_ref, a_ref, b_ref, c_ref, w_ref, o_ref):
    acc = _dot(a_ref[...], w_ref[0:HG_WIDTH, :])
    acc += _dot(b_ref[...], w_ref[HG_WIDTH:HG_WIDTH + ATT_WIDTH, :])
    acc += _dot(c_ref[...], w_ref[HG_WIDTH + ATT_WIDTH:, :])
    o_ref[...] = x_ref[...] + acc


def _outproj_cast_kernel(x_ref, a_ref, b_ref, c_ref, w_ref, o_ref, wb_ref):
    w = w_ref[...].astype(BF16)
    wb_ref[...] = w
    acc = _dot(a_ref[...], w[0:HG_WIDTH, :])
    acc += _dot(b_ref[...], w[HG_WIDTH:HG_WIDTH + ATT_WIDTH, :])
    acc += _dot(c_ref[...], w[HG_WIDTH + ATT_WIDTH:, :])
    o_ref[...] = x_ref[...] + acc


def _outproj(x, o_hg, o_att, o_ml, w, l, tm, cast=False):
    m, d = x.shape
    dm = w.shape[-2]
    if cast:
        tn = min(512, d)
        row = lambda width: pl.BlockSpec((tm, width), lambda i, n: (i, 0))
        return pl.pallas_call(
            _outproj_cast_kernel,
            out_shape=(jax.ShapeDtypeStruct((m, d), F32), jax.ShapeDtypeStruct((dm, d), BF16)),
            grid=(m // tm, d // tn),
            in_specs=[pl.BlockSpec((tm, tn), lambda i, n: (i, n)), row(HG_WIDTH), row(ATT_WIDTH), row(ML_WIDTH),
                      pl.BlockSpec((None, dm, tn), lambda i, n: (l, 0, n))],
            out_specs=(pl.BlockSpec((tm, tn), lambda i, n: (i, n)), pl.BlockSpec((dm, tn), lambda i, n: (0, n))),
            compiler_params=_cparams(("arbitrary", "arbitrary")),
            name="outproj_cast",
        )(x, o_hg, o_att, o_ml, w)
    return pl.pallas_call(
        _outproj_kernel,
        out_shape=jax.ShapeDtypeStruct((m, d), F32),
        grid=(m // tm,),
        in_specs=[
            pl.BlockSpec((tm, d), lambda i: (i, 0)),
            pl.BlockSpec((tm, HG_WIDTH), lambda i: (i, 0)),
            pl.BlockSpec((tm, ATT_WIDTH), lambda i: (i, 0)),
            pl.BlockSpec((tm, ML_WIDTH), lambda i: (i, 0)),
            pl.BlockSpec((dm, d), lambda i: (0, 0)),
        ],
        out_specs=pl.BlockSpec((tm, d), lambda i: (i, 0)),
        compiler_params=_cparams(("parallel",)),
        name="outproj",
    )(x, o_hg, o_att, o_ml, w)


def _final_norm_kernel(x_ref, g_ref, o_ref):
    o_ref[...] = _rms(x_ref[...], g_ref[...])


def _final_norm(x, g, tm):
    m, d = x.shape
    return pl.pallas_call(
        _final_norm_kernel,
        out_shape=jax.ShapeDtypeStruct((m, d), F32),
        grid=(m // tm,),
        in_specs=[pl.BlockSpec((tm, d), lambda i: (i, 0)), pl.BlockSpec((1, d), lambda i: (0, 0))],
        out_specs=pl.BlockSpec((tm, d), lambda i: (i, 0)),
        compiler_params=_cparams(("parallel",)),
        name="final_norm",
    )(x, g)


def _hgrn_kernel(uq_ref, uf_ref, ui_ref, ug_ref, llb_ref, l1m_ref, oml_ref, norm_ref, s0_ref,
                 o_ref, sout_ref, st_ref, oacc_ref, qd_ref, kd_ref, dl_ref, *, tt, t_valid, t_total):
    c = HG_CHUNK
    ti = pl.program_id(1)
    nt = pl.num_programs(1)

    @pl.when(ti == 0)
    def _():
        for h in range(HG_HEADS):
            st_ref[h] = s0_ref[0, h].T

    rr = min(tt, LANES)
    nc = rr // c
    row = lax.broadcasted_iota(jnp.int32, (rr, rr), 0)
    col = lax.broadcasted_iota(jnp.int32, (rr, rr), 1)
    same_chunk = (row // c) == (col // c)
    tril_bd = jnp.where(same_chunk, jnp.where(col <= row, 1.0, 0.0), 0.0)
    ones_bd = jnp.where(same_chunk, 1.0, 0.0)
    s_idx = lax.broadcasted_iota(jnp.int32, (nc, c, HEAD_DIM), 1)
    sel_r = lax.broadcasted_iota(jnp.int32, (rr, c * rr), 0)
    sel_c = lax.broadcasted_iota(jnp.int32, (rr, c * rr), 1)
    sel_big = jnp.where((sel_c // rr) == (sel_r % c),
                        jnp.where(((sel_c % rr) // c) == (sel_r // c), 1.0, 0.0), 0.0).astype(BF16)
    ones = jnp.ones((HEAD_DIM, HEAD_DIM), BF16)
    llb = llb_ref[...]
    l1m = l1m_ref[...]
    oml = oml_ref[...]

    def intra(si, carry):
        r0 = pl.multiple_of(si * rr, rr)
        uf = uf_ref[pl.ds(r0, rr), :]
        b = l1m + _log_sigmoid(uf)
        lf = jnp.maximum(llb, b) + jnp.log1p(jnp.exp(-jnp.abs(llb - b)))
        kk = oml * jax.nn.sigmoid(-uf)
        if t_valid < t_total:
            valid = (ti * tt + r0 + lax.broadcasted_iota(jnp.int32, (rr, HG_WIDTH), 0)) < t_valid
            lf = jnp.where(valid, lf, 0.0)
            kk = jnp.where(valid, kk, 0.0)
        cum = jnp.dot(tril_bd, lf, preferred_element_type=F32, precision=_HI)
        tot = jnp.dot(ones_bd, lf, preferred_element_type=F32, precision=_HI)
        uq = uq_ref[pl.ds(r0, rr), :]
        ui = ui_ref[pl.ds(r0, rr), :]
        qd_ref[pl.ds(r0, rr), :] = (uq * jnp.exp(cum)).astype(BF16)
        kd_ref[pl.ds(r0, rr), :] = (kk * jnp.exp(tot - cum)).astype(BF16)
        dl_ref[pl.ds(r0, rr), :] = jnp.exp(tot)
        for h in range(HG_HEADS):
            hs = slice(h * HEAD_DIM, (h + 1) * HEAD_DIM)
            q3 = uq[:, hs].reshape(nc, c, HEAD_DIM)
            k3 = kk[:, hs].reshape(nc, c, HEAD_DIM)
            cm3 = cum[:, hs].reshape(nc, c, HEAD_DIM)
            ws = []
            for t in range(c):
                ns = SUBLANES if t < SUBLANES else c
                k3t, cm3t = k3[:, :ns, :], cm3[:, :ns, :]
                dec = jnp.exp(jnp.broadcast_to(cm3[:, t:t + 1, :], cm3t.shape) - cm3t)
                w = (jnp.broadcast_to(q3[:, t:t + 1, :], k3t.shape) * k3t) * dec
                w = jnp.where(s_idx[:, :ns, :] <= t, w, 0.0)
                if ns < c:
                    w = jnp.concatenate([w, jnp.zeros((nc, c - ns, HEAD_DIM), F32)], axis=1)
                ws.append(w.reshape(rr, HEAD_DIM))
            w_all = jnp.concatenate(ws, axis=0).astype(BF16)
            r = _dot(w_all, ones)
            vt = jnp.concatenate([ui[:, hs]] * c, axis=0)
            oacc_ref[pl.ds(r0, rr), hs] = _dot(sel_big, (r * vt).astype(BF16))
        return carry

    lax.fori_loop(0, tt // rr, intra, 0)

    def inter(ci, carry):
        r0 = pl.multiple_of(ci * c, c)
        for h in range(HG_HEADS):
            hs = slice(h * HEAD_DIM, (h + 1) * HEAD_DIM)
            st = st_ref[h]
            oacc_ref[pl.ds(r0, c), hs] += _dot_nt(qd_ref[pl.ds(r0, c), hs], st.astype(BF16))
            st_ref[h] = st * dl_ref[pl.ds(r0, 1), hs] + _dot_tn(ui_ref[pl.ds(r0, c), hs].astype(BF16),
                                                                 kd_ref[pl.ds(r0, c), hs])
        return carry

    lax.fori_loop(0, tt // c, inter, 0, unroll=min(4, tt // c))

    for h in range(HG_HEADS):
        hs = slice(h * HEAD_DIM, (h + 1) * HEAD_DIM)
        g = ug_ref[:, hs]
        y = _rms(oacc_ref[:, hs], norm_ref[:, hs]) * (g * jax.nn.sigmoid(g))
        o_ref[:, hs] = y.astype(o_ref.dtype)

    @pl.when(ti == nt - 1)
    def _():
        for h in range(HG_HEADS):
            sout_ref[0, h] = st_ref[h].T


def _hgrn(u, llb, l1m, oml, norm, s0, l, n_seq, t_total, t_valid):
    tt = min(t_total, 512)
    nt = t_total // tt
    cb = HG_WIDTH
    ublock = lambda k: pl.BlockSpec((tt, cb), lambda b, t: (b * nt + t, k))
    par = pl.BlockSpec((None, 1, cb), lambda b, t: (l, 0, 0))
    sblock = pl.BlockSpec((1, HG_HEADS, HEAD_DIM, HEAD_DIM), lambda b, t: (b, 0, 0, 0))
    return pl.pallas_call(
        functools.partial(_hgrn_kernel, tt=tt, t_valid=t_valid, t_total=t_total),
        out_shape=(jax.ShapeDtypeStruct((n_seq * t_total, cb), BF16),
                   jax.ShapeDtypeStruct((n_seq, HG_HEADS, HEAD_DIM, HEAD_DIM), F32)),
        grid=(n_seq, nt),
        in_specs=[ublock(OFF_HQ // cb), ublock(OFF_HF // cb), ublock(OFF_HI // cb), ublock(OFF_HG // cb),
                  par, par, par, par, sblock],
        out_specs=(pl.BlockSpec((tt, cb), lambda b, t: (b * nt + t, 0)), sblock),
        scratch_shapes=[pltpu.VMEM((HG_HEADS, HEAD_DIM, HEAD_DIM), F32), pltpu.VMEM((tt, cb), F32),
                        pltpu.VMEM((tt, cb), BF16), pltpu.VMEM((tt, cb), BF16), pltpu.VMEM((tt, cb), F32)],
        compiler_params=_cparams(("parallel", "arbitrary")),
        name="hgrn",
    )(u, u, u, u, llb, l1m, oml, norm, s0)


def _mlstm_kernel(qk_ref, v_ref, og_ref, gate_ref, cw_ref, cb_ref, gb_ref, norm_ref, c0_ref, n0_ref, m0_ref,
                  conv0_ref, o_ref, cout_ref, nout_ref, mout_ref, c_ref, n_ref, m_ref, carry_ref,
                  *, ll, t_valid, t_total):
    ti = pl.program_id(1)
    nt = pl.num_programs(1)

    @pl.when(ti == 0)
    def _():
        c_ref[...] = c0_ref[0]
        n_ref[...] = n0_ref[0]
        m_ref[...] = m0_ref[0]
        carry_ref[...] = conv0_ref[0]

    x = qk_ref[...]
    xe = jnp.concatenate([carry_ref[...], x], axis=0)
    cw = cw_ref[...]
    y = cb_ref[...] + cw[3:4, :] * x
    for j in range(1, CONV_W):
        y += cw[3 - j:4 - j, :] * xe[SUBLANES - j:SUBLANES - j + ll, :]
    carry_ref[...] = x[ll - SUBLANES:, :]
    qk = y * jax.nn.sigmoid(y)

    g = gate_ref[...] + gb_ref[...]
    lf = _log_sigmoid(g)
    ipre = g
    if t_valid < t_total:
        valid = (ti * ll + lax.broadcasted_iota(jnp.int32, (ll, LANES), 0)) < t_valid
        lf = jnp.where(valid, lf, 0.0)
        ipre = jnp.where(valid, ipre, NEG_BIG)
    row = lax.broadcasted_iota(jnp.int32, (ll, ll), 0)
    col = lax.broadcasted_iota(jnp.int32, (ll, ll), 1)
    causal = col <= row
    cum = jnp.dot(causal.astype(F32), lf, preferred_element_type=F32, precision=_HI)
    lane = lax.broadcasted_iota(jnp.int32, (ll, LANES), 1)
    a_t = jnp.where(lane < ML_HEADS, ipre, cum).T

    for h in range(ML_HEADS):
        hs = slice(h * HEAD_DIM, (h + 1) * HEAD_DIM)
        q = qk[:, hs]
        k = qk[:, ML_WIDTH + h * HEAD_DIM:ML_WIDTH + (h + 1) * HEAD_DIM] * HEAD_DIM ** -0.5
        v = v_ref[:, hs]
        col_cum = cum[:, ML_HEADS + h:ML_HEADS + h + 1]
        col_i = ipre[:, h:h + 1]
        row_cum = a_t[ML_HEADS + h:ML_HEADS + h + 1, :]
        row_i = a_t[h:h + 1, :]
        m_prev = m_ref[h:h + 1, 0:1]
        log_d = jnp.where(causal, col_cum - row_cum + row_i, -jnp.inf)
        m_inter = col_cum + m_prev
        m_t = jnp.maximum(m_inter, jnp.max(log_d, axis=-1, keepdims=True))
        w_inter = jnp.exp(m_inter - m_t)
        qb, kb, vb = q.astype(BF16), k.astype(BF16), v.astype(BF16)
        s = _dot_nt(qb, kb) * jnp.exp(log_d - m_t)
        c_old = c_ref[h]
        n_old = n_ref[h:h + 1, :]
        num = w_inter * _dot(qb, c_old.astype(BF16)) + _dot(s.astype(BF16), vb)
        den = w_inter * jnp.sum(q * n_old, axis=-1, keepdims=True) + jnp.sum(s, axis=-1, keepdims=True)
        hh = num / jnp.maximum(jnp.abs(den), jnp.exp(-m_t))
        m_new = m_t[ll - 1:ll, :]
        cum_last = col_cum[ll - 1:ll, :]
        w_k = jnp.exp(cum_last - col_cum + col_i - m_new)
        decay = jnp.exp(cum_last + m_prev - m_new)
        kw = k * w_k
        c_ref[h] = decay * c_old + _dot_tn(kw.astype(BF16), vb)
        n_ref[h:h + 1, :] = decay * n_old + jnp.sum(kw, axis=0, keepdims=True)
        m_ref[h:h + 1, :] = jnp.broadcast_to(m_new, (1, LANES))
        og = og_ref[:, hs]
        o_ref[:, hs] = (_rms(hh, norm_ref[:, hs]) * jax.nn.sigmoid(og)).astype(o_ref.dtype)

    @pl.when(ti == nt - 1)
    def _():
        cout_ref[0] = c_ref[...]
        nout_ref[0] = n_ref[...]
        mout_ref[0] = m_ref[...]


def _mlstm(u, gates, cw, cb, gb, norm, c0, n0, m0, conv0, l, n_seq, t_total, t_valid, ll):
    nt = t_total // ll
    ublock = lambda w, k: pl.BlockSpec((ll, w), lambda b, t: (b * nt + t, k))
    par = lambda r, w: pl.BlockSpec((None, r, w), lambda b, t: (l, 0, 0))
    cblock = pl.BlockSpec((1, ML_HEADS, HEAD_DIM, HEAD_DIM), lambda b, t: (b, 0, 0, 0))
    vblock = pl.BlockSpec((1, SUBLANES, LANES), lambda b, t: (b, 0, 0))
    return pl.pallas_call(
        functools.partial(_mlstm_kernel, ll=ll, t_valid=t_valid, t_total=t_total),
        out_shape=(jax.ShapeDtypeStruct((n_seq * t_total, ML_WIDTH), BF16),
                   jax.ShapeDtypeStruct((n_seq, ML_HEADS, HEAD_DIM, HEAD_DIM), F32),
                   jax.ShapeDtypeStruct((n_seq, SUBLANES, LANES), F32),
                   jax.ShapeDtypeStruct((n_seq, SUBLANES, LANES), F32)),
        grid=(n_seq, nt),
        in_specs=[ublock(2 * ML_WIDTH, OFF_MQK // (2 * ML_WIDTH)), ublock(ML_WIDTH, OFF_MV // ML_WIDTH),
                  ublock(ML_WIDTH, OFF_MO // ML_WIDTH),
                  pl.BlockSpec((ll, LANES), lambda b, t: (b * nt + t, 0)),
                  par(CONV_W, 2 * ML_WIDTH), par(1, 2 * ML_WIDTH), par(1, LANES), par(1, ML_WIDTH),
                  cblock, vblock, vblock,
                  pl.BlockSpec((1, SUBLANES, 2 * ML_WIDTH), lambda b, t: (b, 0, 0))],
        out_specs=(pl.BlockSpec((ll, ML_WIDTH), lambda b, t: (b * nt + t, 0)), cblock, vblock, vblock),
        scratch_shapes=[pltpu.VMEM((ML_HEADS, HEAD_DIM, HEAD_DIM), F32), pltpu.VMEM((SUBLANES, LANES), F32),
                        pltpu.VMEM((SUBLANES, LANES), F32), pltpu.VMEM((SUBLANES, 2 * ML_WIDTH), F32)],
        compiler_params=_cparams(("parallel", "arbitrary")),
        name="mlstm",
    )(u, u, u, gates, cw, cb, gb, norm, c0, n0, m0, conv0)


def _t5_bucket_np(rel):
    rel = np.asarray(rel, np.int64)
    max_exact = N_BUCKETS // 2
    relf = np.maximum(rel, 1).astype(np.float64)
    large = max_exact + (np.log(relf / max_exact) / math.log(MAX_DISTANCE / max_exact)
                         * (N_BUCKETS - max_exact)).astype(np.int64)
    return np.where(rel < max_exact, rel, np.minimum(large, N_BUCKETS - 1)).astype(np.int32)


def _t5_thresholds(max_rel):
    buckets = _t5_bucket_np(np.arange(max_rel + 1))
    out = []
    for b in range(1, N_BUCKETS):
        hit = np.nonzero(buckets >= b)[0]
        out.append(int(hit[0]) if hit.size else None)
    return out


def _t5_bias_kernel(tab_ref, o_ref, *, rel_fn, thresholds):
    rel = rel_fn(o_ref.shape[1:])
    acc = [jnp.full(o_ref.shape[1:], tab_ref[h, 0], F32) for h in range(ATT_HEADS)]
    for b, th in enumerate(thresholds, start=1):
        if th is None:
            continue
        reached = rel >= th
        for h in range(ATT_HEADS):
            acc[h] = jnp.where(reached, tab_ref[h, b], acc[h])
    for h in range(ATT_HEADS):
        o_ref[h] = acc[h]


def _t5_bias_prompt(bias_tab, t_total):
    blk = MOBA_BLOCK
    nb = t_total // blk

    def rel_fn(shape):
        return (pl.program_id(0) * blk + lax.broadcasted_iota(jnp.int32, shape, 1)
                - lax.broadcasted_iota(jnp.int32, shape, 0))

    return pl.pallas_call(
        functools.partial(_t5_bias_kernel, rel_fn=rel_fn, thresholds=_t5_thresholds(t_total)),
        out_shape=jax.ShapeDtypeStruct((ATT_HEADS, blk, nb * blk), F32),
        grid=(nb,),
        in_specs=[pl.BlockSpec(memory_space=pltpu.SMEM)],
        out_specs=pl.BlockSpec((ATT_HEADS, blk, blk), lambda d: (0, 0, d)),
        compiler_params=_cparams(("parallel",)),
        name="t5_bias_prompt",
    )(bias_tab)


def _t5_bias_sample(bias_tab, past_len):
    n_pages = past_len // PAGE_SIZE

    def rel_fn(shape):
        return past_len - (lax.broadcasted_iota(jnp.int32, shape, 0) * PAGE_SIZE
                           + lax.broadcasted_iota(jnp.int32, shape, 1))

    return pl.pallas_call(
        functools.partial(_t5_bias_kernel, rel_fn=rel_fn, thresholds=_t5_thresholds(past_len)),
        out_shape=jax.ShapeDtypeStruct((ATT_HEADS, n_pages, PAGE_SIZE), F32),
        grid=(1,),
        in_specs=[pl.BlockSpec(memory_space=pltpu.SMEM)],
        out_specs=pl.BlockSpec((ATT_HEADS, n_pages, PAGE_SIZE), lambda i: (0, 0, 0)),
        compiler_params=_cparams(("arbitrary",)),
        name="t5_bias_sample",
    )(bias_tab)


def _moba_prefill_kernel(phys_ref, q_ref, k_ref, v_ref, bias_ref, ck_ref, cv_ref, o_ref, kg_ref, vg_ref,
                         vt_ref, m_ref, l_ref, acc_ref, sem, *, nb, l, per_step, n_slices):
    blk = MOBA_BLOCK
    t_total = nb * blk
    scale = HEAD_DIM ** -0.5
    step = pl.program_id(0) * pl.num_programs(1) + pl.program_id(1)

    def slice_copy(which, k):
        idx = jnp.minimum(step * per_step + k, n_slices - 1)
        head = (idx // N_SEL_PAGES) % ATT_HEADS
        src, dst = ((ck_ref, kg_ref), (cv_ref, vg_ref))[which]
        return pltpu.make_async_copy(src.at[l, phys_ref[idx], :, head, :], dst.at[k], sem.at[which, k])

    for k in range(per_step):
        slice_copy(0, k).start(priority=k % 2)
        slice_copy(1, k).start(priority=(k + 1) % 2)

    q = q_ref[...]
    k = k_ref[...]
    qb = q.astype(BF16)
    kb = k.astype(BF16)
    vt_ref[...] = v_ref[...].T.astype(BF16)

    kmean = jnp.concatenate(
        [jnp.mean(k[n * blk:(n + 1) * blk, :], axis=0, keepdims=True) for n in range(nb)]
        + [jnp.zeros((SUBLANES - nb % SUBLANES, HEAD_DIM), F32)] * (nb % SUBLANES != 0), axis=0)
    nrow = kmean.shape[0]
    gate = _dot_nt(kmean, q, precision=_HI)
    brow = lax.broadcasted_iota(jnp.int32, (nrow, t_total), 0)
    qblk = lax.broadcasted_iota(jnp.int32, (nrow, t_total), 1) // blk
    qblk_row = qblk[0:1, :]
    chosen = []
    for n in range(nb - 1):
        gn = gate[n:n + 1, :]
        ahead = jnp.where(brow < n, jnp.where(gate >= gn, 1.0, 0.0), jnp.where(gate > gn, 1.0, 0.0))
        rank = jnp.sum(jnp.where(brow < qblk, ahead, 0.0), axis=0, keepdims=True)
        chosen.append(jnp.where(qblk_row > n, jnp.where(rank < MOBA_TOPK, 1.0, 0.0), 0.0))

    krow = lax.broadcasted_iota(jnp.int32, (blk, blk), 0)
    qcol = lax.broadcasted_iota(jnp.int32, (blk, blk), 1)
    for i in range(nb):
        rs = slice(i * blk, (i + 1) * blk)
        lg = _dot_nt(kb[rs], qb[rs]) * scale + bias_ref[:, 0:blk]
        lg = jnp.where(krow <= qcol, lg, -jnp.inf)
        m0 = jnp.max(lg, axis=0, keepdims=True)
        p = jnp.exp(lg - m0)
        m_ref[:, rs] = m0
        l_ref[:, rs] = jnp.sum(p, axis=0, keepdims=True)
        acc_ref[:, rs] = _dot(vt_ref[:, rs], p.astype(BF16))

    for n in range(nb - 1):
        ks = slice(n * blk, (n + 1) * blk)
        qs = slice((n + 1) * blk, t_total)
        nq = t_total - (n + 1) * blk
        lg = _dot_nt(kb[ks], qb[qs]) * scale + bias_ref[:, blk:blk + nq]
        lg = jnp.where(chosen[n][:, qs] > 0.0, lg, -jnp.inf)
        m_old = m_ref[:, qs]
        m_new = jnp.maximum(m_old, jnp.max(lg, axis=0, keepdims=True))
        alpha = jnp.exp(m_old - m_new)
        p = jnp.exp(lg - m_new)
        m_ref[:, qs] = m_new
        l_ref[:, qs] = alpha * l_ref[:, qs] + jnp.sum(p, axis=0, keepdims=True)
        acc_ref[:, qs] = alpha * acc_ref[:, qs] + _dot(vt_ref[:, ks], p.astype(BF16))

    o_ref[...] = (acc_ref[...] / l_ref[...]).T.astype(o_ref.dtype)

    for k in range(per_step):
        slice_copy(0, k).wait()
        slice_copy(1, k).wait()


def _moba_prefill(u, bias, n_seq, t_total, cache_k, cache_v, phys, l):
    blk = MOBA_BLOCK
    nb = t_total // blk
    hd = HEAD_DIM
    n_slices = phys.size
    n_steps = ATT_HEADS * n_seq
    per_step = -(-n_slices // n_steps)
    tok = lambda off: pl.BlockSpec((t_total, hd), lambda h, b, ph: (b, off // hd + h))
    gathered = pl.BlockSpec((per_step, PAGE_SIZE, hd), lambda h, b, ph: (h * n_seq + b, 0, 0))
    return pl.pallas_call(
        functools.partial(_moba_prefill_kernel, nb=nb, l=l, per_step=per_step, n_slices=n_slices),
        out_shape=(jax.ShapeDtypeStruct((n_seq * t_total, ATT_WIDTH), BF16),
                   jax.ShapeDtypeStruct((n_steps * per_step, PAGE_SIZE, hd), F32),
                   jax.ShapeDtypeStruct((n_steps * per_step, PAGE_SIZE, hd), F32)),
        grid_spec=pltpu.PrefetchScalarGridSpec(
            num_scalar_prefetch=1, grid=(ATT_HEADS, n_seq),
            in_specs=[tok(OFF_AQ), tok(OFF_AK), tok(OFF_AV),
                      pl.BlockSpec((None, blk, nb * blk), lambda h, b, ph: (h, 0, 0)),
                      pl.BlockSpec(memory_space=pl.ANY), pl.BlockSpec(memory_space=pl.ANY)],
            out_specs=(pl.BlockSpec((t_total, hd), lambda h, b, ph: (b, h)), gathered, gathered),
            scratch_shapes=[pltpu.VMEM((hd, t_total), BF16), pltpu.VMEM((1, t_total), F32),
                            pltpu.VMEM((1, t_total), F32), pltpu.VMEM((hd, t_total), F32),
                            pltpu.SemaphoreType.DMA((2, per_step))]),
        compiler_params=_cparams(("arbitrary", "arbitrary")),
        name="moba_prefill",
    )(phys, u, u, u, bias, cache_k, cache_v)


def _moba_select_kernel(km_ref, q_ref, o_ref, *, nblk):
    g = jnp.sum(km_ref[0] * q_ref[0][None], axis=-1)
    idx = lax.broadcasted_iota(jnp.int32, (nblk, ATT_HEADS), 0).astype(F32)
    o_ref[...] = jnp.zeros_like(o_ref)
    for j in range(MOBA_TOPK):
        mx = jnp.max(g, axis=0, keepdims=True)
        first = jnp.min(jnp.where(g == mx, idx, float(nblk)), axis=0, keepdims=True)
        o_ref[0, j:j + 1, :] = first.astype(jnp.int32)
        g = jnp.where(idx == first, -jnp.inf, g)


def _moba_select(kmean, q3):
    n_seq, nblk = kmean.shape[:2]
    return pl.pallas_call(
        functools.partial(_moba_select_kernel, nblk=nblk),
        out_shape=jax.ShapeDtypeStruct((n_seq, SUBLANES, ATT_HEADS), jnp.int32),
        grid=(n_seq,),
        in_specs=[pl.BlockSpec((1, nblk, ATT_HEADS, HEAD_DIM), lambda b: (b, 0, 0, 0)),
                  pl.BlockSpec((1, ATT_HEADS, HEAD_DIM), lambda b: (b, 0, 0))],
        out_specs=pl.BlockSpec((1, SUBLANES, ATT_HEADS), lambda b: (b, 0, 0)),
        compiler_params=_cparams(("parallel",)),
        name="moba_select",
    )(kmean, q3)


def _moba_decode_kernel(logi_ref, q_ref, kn_ref, vn_ref, bias_ref, bias0_ref, kg_ref, vg_ref, o_ref):
    b = pl.program_id(0)
    scale = HEAD_DIM ** -0.5

    for h in range(ATT_HEADS):
        q = q_ref[0, h:h + 1, :]
        q8 = jnp.broadcast_to(q, (SUBLANES, HEAD_DIM)).astype(BF16)
        logits = []
        for j in range(N_SEL_PAGES):
            lp = logi_ref[(b * ATT_HEADS + h) * N_SEL_PAGES + j]
            logits.append(_dot_nt(q8, kg_ref[h * N_SEL_PAGES + j].astype(BF16)) * scale
                          + bias_ref[h, pl.ds(lp, 1), :])
        self_logit = jnp.sum(q * kn_ref[0, h:h + 1, :], axis=-1, keepdims=True) * scale + bias0_ref[h:h + 1, 0:1]
        m = self_logit
        for lg in logits:
            m = jnp.maximum(m, jnp.max(lg, axis=-1, keepdims=True))
        p_self = jnp.exp(self_logit - m)
        den = p_self
        acc = p_self * vn_ref[0, h:h + 1, :]
        for j in range(N_SEL_PAGES):
            p = jnp.exp(logits[j] - m)
            den += jnp.sum(p, axis=-1, keepdims=True)
            acc += _dot(p.astype(BF16), vg_ref[h * N_SEL_PAGES + j].astype(BF16))
        o_ref[0, h:h + 1, :] = (acc / den)[0:1, :]


def _moba_decode(kg, vg, logi, q3, k3, v3, bias_s, bias0):
    n_seq = q3.shape[0]
    n_slots = ATT_HEADS * N_SEL_PAGES
    tok = pl.BlockSpec((1, ATT_HEADS, HEAD_DIM), lambda b, lg: (b, 0, 0))
    gathered = pl.BlockSpec((n_slots, PAGE_SIZE, HEAD_DIM), lambda b, lg: (b, 0, 0))
    return pl.pallas_call(
        _moba_decode_kernel,
        out_shape=jax.ShapeDtypeStruct((n_seq, ATT_HEADS, HEAD_DIM), F32),
        grid_spec=pltpu.PrefetchScalarGridSpec(
            num_scalar_prefetch=1, grid=(n_seq,),
            in_specs=[tok, tok, tok,
                      pl.BlockSpec(bias_s.shape, lambda b, lg: (0, 0, 0)),
                      pl.BlockSpec(bias0.shape, lambda b, lg: (0, 0)),
                      gathered, gathered],
            out_specs=tok),
        compiler_params=_cparams(("parallel",)),
        name="moba_decode",
    )(logi, q3, k3, v3, bias_s, bias0, kg, vg)


def _pad_rows(a, rows):
    n, w = a.shape
    return jnp.pad(a[:, None, :], ((0, 0), (0, rows - 1), (0, 0))).reshape(n * rows, w)


def _vec_state(a):
    if a.ndim == 2:
        a = jnp.broadcast_to(a[:, :, None], a.shape + (LANES,))
    return jnp.pad(a, ((0, 0), (0, SUBLANES - a.shape[1]), (0, 0)))


def kernel(x_prompt, x_sample, cache_k, cache_v, page_table, state_hgrn, state_mlstm_c, state_mlstm_n,
           state_mlstm_m, state_mlstm_conv, ln_ffn1, w_ffn1_gate, w_ffn1_up, w_ffn1_down, ln_mix, w_in, w_out,
           hgrn_lb_logits, hgrn_out_norm, rel_bias, mlstm_conv_w, mlstm_conv_b, mlstm_gate_bias,
           mlstm_out_norm, ln_ffn2, w_ffn2_gate, w_ffn2_up, w_ffn2_down, ln_final):
    depth = w_in.shape[0]
    bp, tp, d = x_prompt.shape
    bs, ts, _ = x_sample.shape
    assert ts == 1 and tp % MOBA_BLOCK == 0
    n_pages = page_table.shape[1]
    past_len = n_pages * PAGE_SIZE
    assert past_len % MOBA_BLOCK == 0 and past_len // MOBA_BLOCK >= MOBA_TOPK

    w_in_b = w_in.astype(BF16)
    w_gate_b = jnp.pad(w_in[:, :, N_MAIN:], ((0, 0), (0, 0), (0, LANES - N_GATE))).astype(BF16)
    row3 = lambda a: a.reshape(depth, 1, -1)
    ln1, lnm, ln2 = row3(ln_ffn1), row3(ln_mix), row3(ln_ffn2)
    lb = jnp.cumsum(jax.nn.softmax(hgrn_lb_logits.astype(F32), axis=0), axis=0)
    lb = lb - lb[0]
    llb, l1m, oml = row3(jnp.log(lb)), row3(jnp.log1p(-lb)), row3(1.0 - lb)
    hnorm, mnorm = row3(hgrn_out_norm), row3(mlstm_out_norm)
    conv_b = row3(mlstm_conv_b)
    gate_b = row3(jnp.pad(mlstm_gate_bias, ((0, 0), (0, LANES - N_GATE))))
    bias_tab = rel_bias.T.astype(F32)
    bias_p = _t5_bias_prompt(bias_tab, tp)
    bias_s = _t5_bias_sample(bias_tab, past_len)
    bias0 = jnp.broadcast_to(bias_tab[:, 0:1], (ATT_HEADS, LANES))

    zeros = lambda *s: jnp.zeros(s, F32)
    tm_f = 512
    tm_p = 1024 if (bp * tp) % 1024 == 0 else 512
    tm_o = 512
    ll_p = 256

    xp = x_prompt.reshape(bp * tp, d)
    xs = x_sample.reshape(bs, d)
    outs = {k: [] for k in ("ks", "vs", "hgp", "hgs", "cp", "cs", "np", "ns", "mp", "ms", "cvp", "cvs")}
    u_all = []

    for l in range(depth):
        xs, wg1, wu1, wd1 = _ffn(xs, ln1, w_ffn1_gate, w_ffn1_up, w_ffn1_down, l, bs, cast=True)
        xp, kmean = _ffn(xp, ln1, wg1, wu1, wd1, l, tm_f, kmean_of=(cache_k, page_table))
        us, gs = _inproj(xs, lnm, w_in_b, w_gate_b, l, bs)
        u, gates = _inproj(xp, lnm, w_in_b, w_gate_b, l, tm_p)

        heads = lambda a: a.reshape(bs, ATT_HEADS, HEAD_DIM)
        q3, k3, v3 = heads(us[:, OFF_AQ:OFF_AK]), heads(us[:, OFF_AK:OFF_AV]), heads(us[:, OFF_AV:OFF_MQK])
        sel = _moba_select(kmean.reshape(bs, -1, ATT_HEADS, HEAD_DIM), q3)[:, :MOBA_TOPK, :]
        sel = jnp.transpose(sel, (0, 2, 1))
        logi = (sel[..., None] * PAGES_PER_BLOCK
                + jnp.arange(PAGES_PER_BLOCK, dtype=jnp.int32)).reshape(bs, ATT_HEADS, N_SEL_PAGES)
        phys = jnp.take_along_axis(page_table[:, None, :], logi, axis=2)

        o_hg, hg_s = _hgrn(u, llb, l1m, oml, hnorm, zeros(bp, HG_HEADS, HEAD_DIM, HEAD_DIM), l, bp, tp, tp)
        o_att, kg, vg = _moba_prefill(u, bias_p, bp, tp, cache_k, cache_v, phys.reshape(-1), l)
        o_ml, c_s, n_s, m_s = _mlstm(u, gates, mlstm_conv_w, conv_b, gate_b, mnorm,
                                     zeros(bp, ML_HEADS, HEAD_DIM, HEAD_DIM), zeros(bp, SUBLANES, LANES),
                                     zeros(bp, SUBLANES, LANES), zeros(bp, SUBLANES, 2 * ML_WIDTH),
                                     l, bp, tp, tp, ll_p)
        u3 = u.reshape(bp, tp, N_MAIN)
        u_all.append(u3)
        outs["hgp"].append(hg_s)
        outs["cp"].append(c_s)
        outs["np"].append(n_s[:, :ML_HEADS])
        outs["mp"].append(m_s[:, :ML_HEADS, 0])
        outs["cvp"].append(u3[:, tp - (CONV_W - 1):, OFF_MQK:OFF_MV])

        o_hg_s, hg_ss = _hgrn(_pad_rows(us, HG_CHUNK), llb, l1m, oml, hnorm, state_hgrn[l], l, bs, HG_CHUNK, 1)
        conv0 = jnp.pad(state_mlstm_conv[l], ((0, 0), (SUBLANES - (CONV_W - 1), 0), (0, 0)))
        o_ml_s, c_ss, n_ss, m_ss = _mlstm(_pad_rows(us, SAMPLE_PAD), _pad_rows(gs, SAMPLE_PAD), mlstm_conv_w,
                                          conv_b, gate_b, mnorm, state_mlstm_c[l],
                                          _vec_state(state_mlstm_n[l]), _vec_state(state_mlstm_m[l]), conv0,
                                          l, bs, SAMPLE_PAD, 1, SAMPLE_PAD)
        o_att_s = _moba_decode(kg, vg, logi.reshape(-1), q3, k3, v3, bias_s, bias0).reshape(bs, ATT_WIDTH)
        o_att_s = o_att_s.astype(BF16)
        first = lambda a, rows: a.reshape(bs, rows, -1)[:, 0]

        xs, w_out_b = _outproj(xs, first(o_hg_s, HG_CHUNK), o_att_s, first(o_ml_s, SAMPLE_PAD), w_out, l, bs,
                               cast=True)
        xp = _outproj(xp, o_hg, o_att, o_ml, w_out_b, l, tm_o)
        xs, wg2, wu2, wd2 = _ffn(xs, ln2, w_ffn2_gate, w_ffn2_up, w_ffn2_down, l, bs, cast=True)
        xp = _ffn(xp, ln2, wg2, wu2, wd2, l, tm_f)
        outs["ks"].append(k3.reshape(bs, 1, ATT_HEADS, HEAD_DIM))
        outs["vs"].append(v3.reshape(bs, 1, ATT_HEADS, HEAD_DIM))
        outs["hgs"].append(hg_ss)
        outs["cs"].append(c_ss)
        outs["ns"].append(n_ss[:, :ML_HEADS])
        outs["ms"].append(m_ss[:, :ML_HEADS, 0])
        outs["cvs"].append(jnp.concatenate([state_mlstm_conv[l][:, 1:], us[:, None, OFF_MQK:OFF_MV]], axis=1))

    y_prompt = _final_norm(xp, ln_final.reshape(1, d), tm_o).reshape(bp, tp, d)
    y_sample = _final_norm(xs, ln_final.reshape(1, d), bs).reshape(bs, 1, d)
    st = {k: jnp.stack(v) for k, v in outs.items()}
    heads_p = lambda lo, hi: jnp.stack([u3[:, :, lo:hi] for u3 in u_all]).reshape(depth, bp, tp, ATT_HEADS, HEAD_DIM)
    st["kp"], st["vp"] = heads_p(OFF_AK, OFF_AV), heads_p(OFF_AV, OFF_MQK)
    return (y_prompt, y_sample, st["kp"], st["vp"], st["ks"], st["vs"], st["hgp"], st["hgs"], st["cp"], st["cs"],
            st["np"], st["ns"], st["mp"], st["ms"], st["cvp"], st["cvs"])
```

```python
import functools
import math

import jax
import jax.numpy as jnp
import numpy as np
from jax import lax
from jax.experimental import pallas as pl
from jax.experimental.pallas import tpu as pltpu

F32 = jnp.float32
BF16 = jnp.bfloat16

HEAD_DIM = 128
HG_HEADS = 4
ATT_HEADS = 8
ML_HEADS = 4
HG_WIDTH = HG_HEADS * HEAD_DIM
ATT_WIDTH = ATT_HEADS * HEAD_DIM
ML_WIDTH = ML_HEADS * HEAD_DIM
N_MAIN = 4 * HG_WIDTH + 3 * ATT_WIDTH + 4 * ML_WIDTH
N_GATE = 2 * ML_HEADS
OFF_HQ, OFF_HF, OFF_HI, OFF_HG = 0, HG_WIDTH, 2 * HG_WIDTH, 3 * HG_WIDTH
OFF_AQ = 4 * HG_WIDTH
OFF_AK = OFF_AQ + ATT_WIDTH
OFF_AV = OFF_AK + ATT_WIDTH
OFF_MQK = OFF_AV + ATT_WIDTH
OFF_MV = OFF_MQK + 2 * ML_WIDTH
OFF_MO = OFF_MV + ML_WIDTH

PAGE_SIZE = 128
MOBA_BLOCK = 256
MOBA_TOPK = 3
PAGES_PER_BLOCK = MOBA_BLOCK // PAGE_SIZE
N_SEL_PAGES = MOBA_TOPK * PAGES_PER_BLOCK
N_BUCKETS = 32
MAX_DISTANCE = 4096
HG_CHUNK = 16
CONV_W = 4
EPS = 1e-6
NEG_BIG = -1e30
LANES = 128
SUBLANES = 8
VMEM_LIMIT = 48 * 1024 * 1024
SAMPLE_PAD = 128

_HI = lax.Precision.HIGHEST


def _cparams(sem):
    return pltpu.CompilerParams(dimension_semantics=sem, vmem_limit_bytes=VMEM_LIMIT)


def _rms(x, g):
    return x * lax.rsqrt(jnp.mean(x * x, axis=-1, keepdims=True) + EPS) * g


def _log_sigmoid(x):
    return jnp.minimum(x, 0.0) - jnp.log1p(jnp.exp(-jnp.abs(x)))


def _dot(a, b):
    return jnp.dot(a, b, preferred_element_type=F32)


def _dot_nt(a, b, precision=None):
    return lax.dot_general(a, b, (((1,), (1,)), ((), ())), preferred_element_type=F32, precision=precision)


def _dot_tn(a, b):
    return lax.dot_general(a, b, (((0,), (0,)), ((), ())), preferred_element_type=F32)


def _ffn_body(x_ref, g_ref, wg_ref, wu_ref, wd_ref, o_ref, xn_ref, nj, side_job=None, bf16_copies=None,
              final_ref=None):
    j = pl.program_id(1)

    @pl.when(j == 0)
    def _():
        xn_ref[...] = _rms(x_ref[...], g_ref[...]).astype(BF16)
        o_ref[...] = jnp.zeros_like(o_ref)

    if side_job is not None:
        side_job()
    wg, wu, wd = wg_ref[...], wu_ref[...], wd_ref[...]
    if bf16_copies is not None:
        wg, wu, wd = wg.astype(BF16), wu.astype(BF16), wd.astype(BF16)
        for ref, w in zip(bf16_copies, (wg, wu, wd)):
            ref[...] = w
    xn = xn_ref[...]
    a = _dot(xn, wg)
    b = _dot(xn, wu)
    h = (a * jax.nn.sigmoid(a)) * b
    o_ref[...] += _dot(h.astype(BF16), wd)

    @pl.when(j == nj - 1)
    def _():
        y = x_ref[...] + 0.5 * o_ref[...]
        o_ref[...] = y if final_ref is None else _rms(y, final_ref[...])


def _ffn_cast_kernel(x_ref, g_ref, wg_ref, wu_ref, wd_ref, *rest, nj, final):
    final_ref, rest = (rest[0], rest[1:]) if final else (None, rest)
    o_ref, wgb_ref, wub_ref, wdb_ref, xn_ref = rest
    _ffn_body(x_ref, g_ref, wg_ref, wu_ref, wd_ref, o_ref, xn_ref, nj, bf16_copies=(wgb_ref, wub_ref, wdb_ref),
              final_ref=final_ref)


def _ffn_kernel(x_ref, g_ref, wg_ref, wu_ref, wd_ref, *rest, nj, final):
    final_ref, rest = (rest[0], rest[1:]) if final else (None, rest)
    o_ref, xn_ref = rest
    _ffn_body(x_ref, g_ref, wg_ref, wu_ref, wd_ref, o_ref, xn_ref, nj, final_ref=final_ref)


def _ffn_kmean_kernel(pages_ref, x_ref, g_ref, wg_ref, wu_ref, wd_ref, ck_ref, o_ref, km_ref, xn_ref, pbuf, sem,
                      *, nj, l, group, n_blocks):
    s = pl.program_id(0) * nj + pl.program_id(1)
    n_steps = pl.num_programs(0) * nj

    def block_of(step, g):
        return jnp.minimum(step * group + g, n_blocks - 1)

    def page_copy(step, g, p):
        slot = step % 2
        page = pages_ref[block_of(step, g) * PAGES_PER_BLOCK + p]
        k = g * PAGES_PER_BLOCK + p
        return pltpu.make_async_copy(ck_ref.at[l, page], pbuf.at[slot, k], sem.at[slot, k])

    def start_all(step):
        for g in range(group):
            for p in range(PAGES_PER_BLOCK):
                page_copy(step, g, p).start()

    @pl.when(s == 0)
    def _():
        start_all(s)

    @pl.when(s + 1 < n_steps)
    def _():
        start_all(s + 1)

    def block_means():
        slot = s % 2
        for g in range(group):
            for p in range(PAGES_PER_BLOCK):
                page_copy(s, g, p).wait()
        for g in range(group):
            tot = jnp.zeros((ATT_HEADS, HEAD_DIM), F32)
            for p in range(PAGES_PER_BLOCK):
                tot += jnp.sum(pbuf[slot, g * PAGES_PER_BLOCK + p], axis=0)
            km_ref[block_of(s, g)] = tot * (1.0 / MOBA_BLOCK)

    _ffn_body(x_ref, g_ref, wg_ref, wu_ref, wd_ref, o_ref, xn_ref, nj, side_job=block_means)


def _ffn(x, ln, wg, wu, wd, l, tm, kmean_of=None, cast=False, final_gain=None):
    m, d = x.shape
    final = final_gain is not None
    extra_in = [pl.BlockSpec((1, d), lambda i, j: (0, 0))] if final else []
    extra_arg = [final_gain] if final else []
    f = wg.shape[-1]
    tf = 512 if f % 512 == 0 else f
    nj = f // tf
    grid = (m // tm, nj)
    x_in = pl.BlockSpec((tm, d), lambda i, j, *_: (i, 0))
    ln_in = pl.BlockSpec((None, 1, d), lambda i, j, *_: (l, 0, 0))
    w_col = pl.BlockSpec((d, tf), lambda i, j, *_: (0, j))
    w_row = pl.BlockSpec((tf, d), lambda i, j, *_: (j, 0))
    in_specs = [x_in, ln_in, w_col, w_col, w_row]
    x_out = pl.BlockSpec((tm, d), lambda i, j, *_: (i, 0))
    if cast:
        return pl.pallas_call(
            functools.partial(_ffn_cast_kernel, nj=nj, final=final),
            out_shape=(jax.ShapeDtypeStruct((m, d), F32), jax.ShapeDtypeStruct((d, f), BF16),
                       jax.ShapeDtypeStruct((d, f), BF16), jax.ShapeDtypeStruct((f, d), BF16)),
            grid=grid,
            in_specs=[x_in, ln_in,
                      pl.BlockSpec((None, d, tf), lambda i, j: (l, 0, j)),
                      pl.BlockSpec((None, d, tf), lambda i, j: (l, 0, j)),
                      pl.BlockSpec((None, tf, d), lambda i, j: (l, j, 0))] + extra_in,
            out_specs=(x_out, w_col, w_col, w_row),
            scratch_shapes=[pltpu.VMEM((tm, d), BF16)],
            compiler_params=_cparams(("arbitrary", "arbitrary")),
            name="ffn_cast",
        )(x, ln, wg, wu, wd, *extra_arg)
    if kmean_of is None:
        return pl.pallas_call(
            functools.partial(_ffn_kernel, nj=nj, final=final),
            out_shape=jax.ShapeDtypeStruct((m, d), F32),
            grid=grid, in_specs=in_specs + extra_in, out_specs=x_out,
            scratch_shapes=[pltpu.VMEM((tm, d), BF16)],
            compiler_params=_cparams(("parallel", "arbitrary")),
            name="ffn",
        )(x, ln, wg, wu, wd, *extra_arg)
    assert not final
    cache_k, page_table = kmean_of
    n_blocks = page_table.size // PAGES_PER_BLOCK
    group = -(-n_blocks // (grid[0] * grid[1]))
    n_buf = group * PAGES_PER_BLOCK
    return pl.pallas_call(
        functools.partial(_ffn_kmean_kernel, nj=nj, l=l, group=group, n_blocks=n_blocks),
        out_shape=(jax.ShapeDtypeStruct((m, d), F32),
                   jax.ShapeDtypeStruct((n_blocks, ATT_HEADS, HEAD_DIM), F32)),
        grid_spec=pltpu.PrefetchScalarGridSpec(
            num_scalar_prefetch=1, grid=grid,
            in_specs=in_specs + [pl.BlockSpec(memory_space=pl.ANY)],
            out_specs=(x_out, pl.BlockSpec((n_blocks, ATT_HEADS, HEAD_DIM), lambda i, j, pg: (0, 0, 0))),
            scratch_shapes=[pltpu.VMEM((tm, d), BF16),
                            pltpu.VMEM((2, n_buf, PAGE_SIZE, ATT_HEADS, HEAD_DIM), F32),
                            pltpu.SemaphoreType.DMA((2, n_buf))]),
        compiler_params=_cparams(("arbitrary", "arbitrary")),
        name="ffn_kmean",
    )(page_table.reshape(-1), x, ln, wg, wu, wd, cache_k)


def _inproj_kernel(x_ref, g_ref, w_ref, wgate_ref, u_ref, gate_ref, *rest):
    wb_ref, xn_ref = rest if len(rest) == 2 else (None, rest[0])

    @pl.when(pl.program_id(1) == 0)
    def _():
        xn = _rms(x_ref[...], g_ref[...]).astype(BF16)
        xn_ref[...] = xn
        gate_ref[...] = _dot(xn, wgate_ref[...])

    w = w_ref[...]
    if wb_ref is not None:
        w = w.astype(BF16)
        wb_ref[...] = w
    u_ref[...] = _dot(xn_ref[...], w)


def _inproj(x, ln, w, wgate, l, tm, cast=False):
    m, d = x.shape
    tn = 1024
    w_out = pl.BlockSpec((d, tn), lambda i, j: (0, j))
    out_shape = [jax.ShapeDtypeStruct((m, N_MAIN), F32), jax.ShapeDtypeStruct((m, LANES), F32)]
    out_specs = [pl.BlockSpec((tm, tn), lambda i, j: (i, j)), pl.BlockSpec((tm, LANES), lambda i, j: (i, 0))]
    if cast:
        out_shape.append(jax.ShapeDtypeStruct((d, N_MAIN), BF16))
        out_specs.append(w_out)
    return pl.pallas_call(
        _inproj_kernel,
        out_shape=tuple(out_shape),
        grid=(m // tm, N_MAIN // tn),
        in_specs=[
            pl.BlockSpec((tm, d), lambda i, j: (i, 0)),
            pl.BlockSpec((None, 1, d), lambda i, j: (l, 0, 0)),
            pl.BlockSpec((None, d, tn), lambda i, j: (l, 0, j)) if w.ndim == 3 else w_out,
            pl.BlockSpec((None, d, LANES), lambda i, j: (l, 0, 0)),
        ],
        out_specs=tuple(out_specs),
        scratch_shapes=[pltpu.VMEM((tm, d), BF16)],
        compiler_params=_cparams(("arbitrary", "arbitrary") if cast else ("parallel", "arbitrary")),
        name="inproj_cast" if cast else "inproj",
    )(x, ln, w, wgate)


def _outproj_kernel(x_ref, a_ref, b_ref, c_ref, w_ref, o_ref):
    acc = _dot(a_ref[...], w_ref[0:HG_WIDTH, :])
    acc += _dot(b_ref[...], w_ref[HG_WIDTH:HG_WIDTH + ATT_WIDTH, :])
    acc += _dot(c_ref[...], w_ref[HG_WIDTH + ATT_WIDTH:, :])
    o_ref[...] = x_ref[...] + acc


def _outproj_cast_kernel(x_ref, a_ref, b_ref, c_ref, w_ref, o_ref, wb_ref):
    w = w_ref[...].astype(BF16)
    wb_ref[...] = w
    acc = _dot(a_ref[...], w[0:HG_WIDTH, :])
    acc += _dot(b_ref[...], w[HG_WIDTH:HG_WIDTH + ATT_WIDTH, :])
    acc += _dot(c_ref[...], w[HG_WIDTH + ATT_WIDTH:, :])
    o_ref[...] = x_ref[...] + acc


def _outproj(x, o_hg, o_att, o_ml, w, l, tm, cast=False):
    m, d = x.shape
    dm = w.shape[-2]
    if cast:
        tn = min(512, d)
        row = lambda width: pl.BlockSpec((tm, width), lambda i, n: (i, 0))
        return pl.pallas_call(
            _outproj_cast_kernel,
            out_shape=(jax.ShapeDtypeStruct((m, d), F32), jax.ShapeDtypeStruct((dm, d), BF16)),
            grid=(m // tm, d // tn),
            in_specs=[pl.BlockSpec((tm, tn), lambda i, n: (i, n)), row(HG_WIDTH), row(ATT_WIDTH), row(ML_WIDTH),
                      pl.BlockSpec((None, dm, tn), lambda i, n: (l, 0, n))],
            out_specs=(pl.BlockSpec((tm, tn), lambda i, n: (i, n)), pl.BlockSpec((dm, tn), lambda i, n: (0, n))),
            compiler_params=_cparams(("arbitrary", "arbitrary")),
            name="outproj_cast",
        )(x, o_hg, o_att, o_ml, w)
    return pl.pallas_call(
        _outproj_kernel,
        out_shape=jax.ShapeDtypeStruct((m, d), F32),
        grid=(m // tm,),
        in_specs=[
            pl.BlockSpec((tm, d), lambda i: (i, 0)),
            pl.BlockSpec((tm, HG_WIDTH), lambda i: (i, 0)),
            pl.BlockSpec((tm, ATT_WIDTH), lambda i: (i, 0)),
            pl.BlockSpec((tm, ML_WIDTH), lambda i: (i, 0)),
            pl.BlockSpec((dm, d), lambda i: (0, 0)),
        ],
        out_specs=pl.BlockSpec((tm, d), lambda i: (i, 0)),
        compiler_params=_cparams(("parallel",)),
        name="outproj",
    )(x, o_hg, o_att, o_ml, w)


def _hgrn_kernel(uq_ref, uf_ref, ui_ref, ug_ref, llb_ref, l1m_ref, oml_ref, norm_ref, s0_ref,
                 o_ref, sout_ref, st_ref, oacc_ref, qd_ref, kd_ref, dl_ref, *, tt, t_valid, t_total):
    c = HG_CHUNK
    ti = pl.program_id(1)
    nt = pl.num_programs(1)

    @pl.when(ti == 0)
    def _():
        for h in range(HG_HEADS):
            st_ref[h] = s0_ref[0, h].T

    rr = min(tt, LANES)
    nc = rr // c
    row = lax.broadcasted_iota(jnp.int32, (rr, rr), 0)
    col = lax.broadcasted_iota(jnp.int32, (rr, rr), 1)
    same_chunk = (row // c) == (col // c)
    tril_bd = jnp.where(same_chunk, jnp.where(col <= row, 1.0, 0.0), 0.0)
    ones_bd = jnp.where(same_chunk, 1.0, 0.0)
    s_idx = lax.broadcasted_iota(jnp.int32, (nc, c, HEAD_DIM), 1)
    sel_r = lax.broadcasted_iota(jnp.int32, (rr, c * rr), 0)
    sel_c = lax.broadcasted_iota(jnp.int32, (rr, c * rr), 1)
    sel_big = jnp.where((sel_c // rr) == (sel_r % c),
                        jnp.where(((sel_c % rr) // c) == (sel_r // c), 1.0, 0.0), 0.0).astype(BF16)
    ones = jnp.ones((HEAD_DIM, HEAD_DIM), BF16)
    llb = llb_ref[...]
    l1m = l1m_ref[...]
    oml = oml_ref[...]

    def intra(si, carry):
        r0 = pl.multiple_of(si * rr, rr)
        uf = uf_ref[pl.ds(r0, rr), :]
        b = l1m + _log_sigmoid(uf)
        lf = jnp.maximum(llb, b) + jnp.log1p(jnp.exp(-jnp.abs(llb - b)))
        kk = oml * jax.nn.sigmoid(-uf)
        if t_valid < t_total:
            valid = (ti * tt + r0 + lax.broadcasted_iota(jnp.int32, (rr, HG_WIDTH), 0)) < t_valid
            lf = jnp.where(valid, lf, 0.0)
            kk = jnp.where(valid, kk, 0.0)
        cum = jnp.dot(tril_bd, lf, preferred_element_type=F32, precision=_HI)
        tot = jnp.dot(ones_bd, lf, preferred_element_type=F32, precision=_HI)
        uq = uq_ref[pl.ds(r0, rr), :]
        ui = ui_ref[pl.ds(r0, rr), :]
        qd_ref[pl.ds(r0, rr), :] = (uq * jnp.exp(cum)).astype(BF16)
        kd_ref[pl.ds(r0, rr), :] = (kk * jnp.exp(tot - cum)).astype(BF16)
        dl_ref[pl.ds(r0, rr), :] = jnp.exp(tot)
        for h in range(HG_HEADS):
            hs = slice(h * HEAD_DIM, (h + 1) * HEAD_DIM)
            q3 = uq[:, hs].reshape(nc, c, HEAD_DIM)
            k3 = kk[:, hs].reshape(nc, c, HEAD_DIM)
            cm3 = cum[:, hs].reshape(nc, c, HEAD_DIM)
            ws = []
            for t in range(c):
                ns = SUBLANES if t < SUBLANES else c
                k3t, cm3t = k3[:, :ns, :], cm3[:, :ns, :]
                dec = jnp.exp(jnp.broadcast_to(cm3[:, t:t + 1, :], cm3t.shape) - cm3t)
                w = (jnp.broadcast_to(q3[:, t:t + 1, :], k3t.shape) * k3t) * dec
                w = jnp.where(s_idx[:, :ns, :] <= t, w, 0.0)
                if ns < c:
                    w = jnp.concatenate([w, jnp.zeros((nc, c - ns, HEAD_DIM), F32)], axis=1)
                ws.append(w.reshape(rr, HEAD_DIM))
            w_all = jnp.concatenate(ws, axis=0).astype(BF16)
            r = _dot(w_all, ones)
            vt = jnp.concatenate([ui[:, hs]] * c, axis=0)
            oacc_ref[pl.ds(r0, rr), hs] = _dot(sel_big, (r * vt).astype(BF16))
        return carry

    lax.fori_loop(0, tt // rr, intra, 0)

    def inter(ci, carry):
        r0 = pl.multiple_of(ci * c, c)
        for h in range(HG_HEADS):
            hs = slice(h * HEAD_DIM, (h + 1) * HEAD_DIM)
            st = st_ref[h]
            oacc_ref[pl.ds(r0, c), hs] += _dot_nt(qd_ref[pl.ds(r0, c), hs], st.astype(BF16))
            st_ref[h] = st * dl_ref[pl.ds(r0, 1), hs] + _dot_tn(ui_ref[pl.ds(r0, c), hs].astype(BF16),
                                                                 kd_ref[pl.ds(r0, c), hs])
        return carry

    lax.fori_loop(0, tt // c, inter, 0, unroll=min(4, tt // c))

    for h in range(HG_HEADS):
        hs = slice(h * HEAD_DIM, (h + 1) * HEAD_DIM)
        g = ug_ref[:, hs]
        y = _rms(oacc_ref[:, hs], norm_ref[:, hs]) * (g * jax.nn.sigmoid(g))
        o_ref[:, hs] = y.astype(o_ref.dtype)

    @pl.when(ti == nt - 1)
    def _():
        for h in range(HG_HEADS):
            sout_ref[0, h] = st_ref[h].T


def _hgrn(u, llb, l1m, oml, norm, s0, l, n_seq, t_total, t_valid):
    tt = min(t_total, 512)
    nt = t_total // tt
    cb = HG_WIDTH
    ublock = lambda k: pl.BlockSpec((tt, cb), lambda b, t: (b * nt + t, k))
    par = pl.BlockSpec((None, 1, cb), lambda b, t: (l, 0, 0))
    sblock = pl.BlockSpec((1, HG_HEADS, HEAD_DIM, HEAD_DIM), lambda b, t: (b, 0, 0, 0))
    return pl.pallas_call(
        functools.partial(_hgrn_kernel, tt=tt, t_valid=t_valid, t_total=t_total),
        out_shape=(jax.ShapeDtypeStruct((n_seq * t_total, cb), BF16),
                   jax.ShapeDtypeStruct((n_seq, HG_HEADS, HEAD_DIM, HEAD_DIM), F32)),
        grid=(n_seq, nt),
        in_specs=[ublock(OFF_HQ // cb), ublock(OFF_HF // cb), ublock(OFF_HI // cb), ublock(OFF_HG // cb),
                  par, par, par, par, sblock],
        out_specs=(pl.BlockSpec((tt, cb), lambda b, t: (b * nt + t, 0)), sblock),
        scratch_shapes=[pltpu.VMEM((HG_HEADS, HEAD_DIM, HEAD_DIM), F32), pltpu.VMEM((tt, cb), F32),
                        pltpu.VMEM((tt, cb), BF16), pltpu.VMEM((tt, cb), BF16), pltpu.VMEM((tt, cb), F32)],
        compiler_params=_cparams(("parallel", "arbitrary")),
        name="hgrn",
    )(u, u, u, u, llb, l1m, oml, norm, s0)


def _mlstm_kernel(qk_ref, v_ref, og_ref, gate_ref, cw_ref, cb_ref, gb_ref, norm_ref, c0_ref, n0_ref, m0_ref,
                  conv0_ref, o_ref, cout_ref, nout_ref, mout_ref, c_ref, n_ref, m_ref, carry_ref,
                  *, ll, t_valid, t_total):
    ti = pl.program_id(1)
    nt = pl.num_programs(1)

    @pl.when(ti == 0)
    def _():
        c_ref[...] = c0_ref[0]
        n_ref[...] = n0_ref[0]
        m_ref[...] = m0_ref[0]
        carry_ref[...] = conv0_ref[0]

    x = qk_ref[...]
    xe = jnp.concatenate([carry_ref[...], x], axis=0)
    cw = cw_ref[...]
    y = cb_ref[...] + cw[3:4, :] * x
    for j in range(1, CONV_W):
        y += cw[3 - j:4 - j, :] * xe[SUBLANES - j:SUBLANES - j + ll, :]
    carry_ref[...] = x[ll - SUBLANES:, :]
    qk = y * jax.nn.sigmoid(y)

    g = gate_ref[...] + gb_ref[...]
    lf = _log_sigmoid(g)
    ipre = g
    if t_valid < t_total:
        valid = (ti * ll + lax.broadcasted_iota(jnp.int32, (ll, LANES), 0)) < t_valid
        lf = jnp.where(valid, lf, 0.0)
        ipre = jnp.where(valid, ipre, NEG_BIG)
    row = lax.broadcasted_iota(jnp.int32, (ll, ll), 0)
    col = lax.broadcasted_iota(jnp.int32, (ll, ll), 1)
    causal = col <= row
    cum = jnp.dot(causal.astype(F32), lf, preferred_element_type=F32, precision=_HI)
    lane = lax.broadcasted_iota(jnp.int32, (ll, LANES), 1)
    a_t = jnp.where(lane < ML_HEADS, ipre, cum).T

    for h in range(ML_HEADS):
        hs = slice(h * HEAD_DIM, (h + 1) * HEAD_DIM)
        q = qk[:, hs]
        k = qk[:, ML_WIDTH + h * HEAD_DIM:ML_WIDTH + (h + 1) * HEAD_DIM] * HEAD_DIM ** -0.5
        v = v_ref[:, hs]
        col_cum = cum[:, ML_HEADS + h:ML_HEADS + h + 1]
        col_i = ipre[:, h:h + 1]
        row_cum = a_t[ML_HEADS + h:ML_HEADS + h + 1, :]
        row_i = a_t[h:h + 1, :]
        m_prev = m_ref[h:h + 1, 0:1]
        log_d = jnp.where(causal, col_cum - row_cum + row_i, -jnp.inf)
        m_inter = col_cum + m_prev
        m_t = jnp.maximum(m_inter, jnp.max(log_d, axis=-1, keepdims=True))
        w_inter = jnp.exp(m_inter - m_t)
        qb, kb, vb = q.astype(BF16), k.astype(BF16), v.astype(BF16)
        s = _dot_nt(qb, kb) * jnp.exp(log_d - m_t)
        c_old = c_ref[h]
        n_old = n_ref[h:h + 1, :]
        num = w_inter * _dot(qb, c_old.astype(BF16)) + _dot(s.astype(BF16), vb)
        den = w_inter * jnp.sum(q * n_old, axis=-1, keepdims=True) + jnp.sum(s, axis=-1, keepdims=True)
        hh = num / jnp.maximum(jnp.abs(den), jnp.exp(-m_t))
        m_new = m_t[ll - 1:ll, :]
        cum_last = col_cum[ll - 1:ll, :]
        w_k = jnp.exp(cum_last - col_cum + col_i - m_new)
        decay = jnp.exp(cum_last + m_prev - m_new)
        kw = k * w_k
        c_ref[h] = decay * c_old + _dot_tn(kw.astype(BF16), vb)
        n_ref[h:h + 1, :] = decay * n_old + jnp.sum(kw, axis=0, keepdims=True)
        m_ref[h:h + 1, :] = jnp.broadcast_to(m_new, (1, LANES))
        og = og_ref[:, hs]
        o_ref[:, hs] = (_rms(hh, norm_ref[:, hs]) * jax.nn.sigmoid(og)).astype(o_ref.dtype)

    @pl.when(ti == nt - 1)
    def _():
        cout_ref[0] = c_ref[...]
        nout_ref[0] = n_ref[...]
        mout_ref[0] = m_ref[...]


def _mlstm(u, gates, cw, cb, gb, norm, c0, n0, m0, conv0, l, n_seq, t_total, t_valid, ll, col0=0):
    nt = t_total // ll
    ublock = lambda w, k: pl.BlockSpec((ll, w), lambda b, t: (b * nt + t, k - col0 // w))
    par = lambda r, w: pl.BlockSpec((None, r, w), lambda b, t: (l, 0, 0))
    cblock = pl.BlockSpec((1, ML_HEADS, HEAD_DIM, HEAD_DIM), lambda b, t: (b, 0, 0, 0))
    vblock = pl.BlockSpec((1, SUBLANES, LANES), lambda b, t: (b, 0, 0))
    return pl.pallas_call(
        functools.partial(_mlstm_kernel, ll=ll, t_valid=t_valid, t_total=t_total),
        out_shape=(jax.ShapeDtypeStruct((n_seq * t_total, ML_WIDTH), BF16),
                   jax.ShapeDtypeStruct((n_seq, ML_HEADS, HEAD_DIM, HEAD_DIM), F32),
                   jax.ShapeDtypeStruct((n_seq, SUBLANES, LANES), F32),
                   jax.ShapeDtypeStruct((n_seq, SUBLANES, LANES), F32)),
        grid=(n_seq, nt),
        in_specs=[ublock(2 * ML_WIDTH, OFF_MQK // (2 * ML_WIDTH)), ublock(ML_WIDTH, OFF_MV // ML_WIDTH),
                  ublock(ML_WIDTH, OFF_MO // ML_WIDTH),
                  pl.BlockSpec((ll, LANES), lambda b, t: (b * nt + t, 0)),
                  par(CONV_W, 2 * ML_WIDTH), par(1, 2 * ML_WIDTH), par(1, LANES), par(1, ML_WIDTH),
                  cblock, vblock, vblock,
                  pl.BlockSpec((1, SUBLANES, 2 * ML_WIDTH), lambda b, t: (b, 0, 0))],
        out_specs=(pl.BlockSpec((ll, ML_WIDTH), lambda b, t: (b * nt + t, 0)), cblock, vblock, vblock),
        scratch_shapes=[pltpu.VMEM((ML_HEADS, HEAD_DIM, HEAD_DIM), F32), pltpu.VMEM((SUBLANES, LANES), F32),
                        pltpu.VMEM((SUBLANES, LANES), F32), pltpu.VMEM((SUBLANES, 2 * ML_WIDTH), F32)],
        compiler_params=_cparams(("parallel", "arbitrary")),
        name="mlstm",
    )(u, u, u, gates, cw, cb, gb, norm, c0, n0, m0, conv0)


def _t5_bucket_np(rel):
    rel = np.asarray(rel, np.int64)
    max_exact = N_BUCKETS // 2
    relf = np.maximum(rel, 1).astype(np.float64)
    large = max_exact + (np.log(relf / max_exact) / math.log(MAX_DISTANCE / max_exact)
                         * (N_BUCKETS - max_exact)).astype(np.int64)
    return np.where(rel < max_exact, rel, np.minimum(large, N_BUCKETS - 1)).astype(np.int32)


def _t5_thresholds(max_rel):
    buckets = _t5_bucket_np(np.arange(max_rel + 1))
    out = []
    for b in range(1, N_BUCKETS):
        hit = np.nonzero(buckets >= b)[0]
        out.append(int(hit[0]) if hit.size else None)
    return out


def _t5_bias_kernel(tab_ref, o_ref, *, rel_fn, thresholds):
    rel = rel_fn(o_ref.shape[1:])
    acc = [jnp.full(o_ref.shape[1:], tab_ref[h, 0], F32) for h in range(ATT_HEADS)]
    for b, th in enumerate(thresholds, start=1):
        if th is None:
            continue
        reached = rel >= th
        for h in range(ATT_HEADS):
            acc[h] = jnp.where(reached, tab_ref[h, b], acc[h])
    for h in range(ATT_HEADS):
        o_ref[h] = acc[h]


def _t5_bias_prompt(bias_tab, t_total):
    blk = MOBA_BLOCK
    nb = t_total // blk

    def rel_fn(shape):
        return (pl.program_id(0) * blk + lax.broadcasted_iota(jnp.int32, shape, 1)
                - lax.broadcasted_iota(jnp.int32, shape, 0))

    return pl.pallas_call(
        functools.partial(_t5_bias_kernel, rel_fn=rel_fn, thresholds=_t5_thresholds(t_total)),
        out_shape=jax.ShapeDtypeStruct((ATT_HEADS, blk, nb * blk), F32),
        grid=(nb,),
        in_specs=[pl.BlockSpec(memory_space=pltpu.SMEM)],
        out_specs=pl.BlockSpec((ATT_HEADS, blk, blk), lambda d: (0, 0, d)),
        compiler_params=_cparams(("parallel",)),
        name="t5_bias_prompt",
    )(bias_tab)


def _t5_bias_sample(bias_tab, past_len):
    n_pages = past_len // PAGE_SIZE

    def rel_fn(shape):
        return past_len - (lax.broadcasted_iota(jnp.int32, shape, 0) * PAGE_SIZE
                           + lax.broadcasted_iota(jnp.int32, shape, 1))

    return pl.pallas_call(
        functools.partial(_t5_bias_kernel, rel_fn=rel_fn, thresholds=_t5_thresholds(past_len)),
        out_shape=jax.ShapeDtypeStruct((ATT_HEADS, n_pages, PAGE_SIZE), F32),
        grid=(1,),
        in_specs=[pl.BlockSpec(memory_space=pltpu.SMEM)],
        out_specs=pl.BlockSpec((ATT_HEADS, n_pages, PAGE_SIZE), lambda i: (0, 0, 0)),
        compiler_params=_cparams(("arbitrary",)),
        name="t5_bias_sample",
    )(bias_tab)


def _moba_prefill_kernel(phys_ref, q_ref, k_ref, v_ref, bias_ref, ck_ref, cv_ref, o_ref, kg_ref, vg_ref,
                         vt_ref, m_ref, l_ref, acc_ref, sem, *, nb, l, per_step, n_slices):
    blk = MOBA_BLOCK
    t_total = nb * blk
    scale = HEAD_DIM ** -0.5
    step = pl.program_id(0) * pl.num_programs(1) + pl.program_id(1)

    def slice_copy(which, k):
        idx = jnp.minimum(step * per_step + k, n_slices - 1)
        head = (idx // N_SEL_PAGES) % ATT_HEADS
        src, dst = ((ck_ref, kg_ref), (cv_ref, vg_ref))[which]
        return pltpu.make_async_copy(src.at[l, phys_ref[idx], :, head, :], dst.at[k], sem.at[which, k])

    for k in range(per_step):
        slice_copy(0, k).start(priority=k % 2)
        slice_copy(1, k).start(priority=(k + 1) % 2)

    q = q_ref[...]
    k = k_ref[...]
    qb = q.astype(BF16)
    kb = k.astype(BF16)
    vt_ref[...] = v_ref[...].T.astype(BF16)

    kmean = jnp.concatenate(
        [jnp.mean(k[n * blk:(n + 1) * blk, :], axis=0, keepdims=True) for n in range(nb)]
        + [jnp.zeros((SUBLANES - nb % SUBLANES, HEAD_DIM), F32)] * (nb % SUBLANES != 0), axis=0)
    nrow = kmean.shape[0]
    gate = _dot_nt(kmean, q, precision=_HI)
    brow = lax.broadcasted_iota(jnp.int32, (nrow, t_total), 0)
    qblk = lax.broadcasted_iota(jnp.int32, (nrow, t_total), 1) // blk
    qblk_row = qblk[0:1, :]
    chosen = []
    for n in range(nb - 1):
        gn = gate[n:n + 1, :]
        ahead = jnp.where(brow < n, jnp.where(gate >= gn, 1.0, 0.0), jnp.where(gate > gn, 1.0, 0.0))
        rank = jnp.sum(jnp.where(brow < qblk, ahead, 0.0), axis=0, keepdims=True)
        chosen.append(jnp.where(qblk_row > n, jnp.where(rank < MOBA_TOPK, 1.0, 0.0), 0.0))

    krow = lax.broadcasted_iota(jnp.int32, (blk, blk), 0)
    qcol = lax.broadcasted_iota(jnp.int32, (blk, blk), 1)
    for i in range(nb):
        rs = slice(i * blk, (i + 1) * blk)
        lg = _dot_nt(kb[rs], qb[rs]) * scale + bias_ref[:, 0:blk]
        lg = jnp.where(krow <= qcol, lg, -jnp.inf)
        m0 = jnp.max(lg, axis=0, keepdims=True)
        p = jnp.exp(lg - m0)
        m_ref[:, rs] = m0
        l_ref[:, rs] = jnp.sum(p, axis=0, keepdims=True)
        acc_ref[:, rs] = _dot(vt_ref[:, rs], p.astype(BF16))

    for n in range(nb - 1):
        ks = slice(n * blk, (n + 1) * blk)
        qs = slice((n + 1) * blk, t_total)
        nq = t_total - (n + 1) * blk
        lg = _dot_nt(kb[ks], qb[qs]) * scale + bias_ref[:, blk:blk + nq]
        lg = jnp.where(chosen[n][:, qs] > 0.0, lg, -jnp.inf)
        m_old = m_ref[:, qs]
        m_new = jnp.maximum(m_old, jnp.max(lg, axis=0, keepdims=True))
        alpha = jnp.exp(m_old - m_new)
        p = jnp.exp(lg - m_new)
        m_ref[:, qs] = m_new
        l_ref[:, qs] = alpha * l_ref[:, qs] + jnp.sum(p, axis=0, keepdims=True)
        acc_ref[:, qs] = alpha * acc_ref[:, qs] + _dot(vt_ref[:, ks], p.astype(BF16))

    o_ref[...] = (acc_ref[...] / l_ref[...]).T.astype(o_ref.dtype)

    for k in range(per_step):
        slice_copy(0, k).wait()
        slice_copy(1, k).wait()


def _moba_prefill(u, bias, n_seq, t_total, cache_k, cache_v, phys, l):
    blk = MOBA_BLOCK
    nb = t_total // blk
    hd = HEAD_DIM
    n_slices = phys.size
    n_steps = ATT_HEADS * n_seq
    per_step = -(-n_slices // n_steps)
    tok = lambda off: pl.BlockSpec((t_total, hd), lambda h, b, ph: (b, off // hd + h))
    gathered = pl.BlockSpec((per_step, PAGE_SIZE, hd), lambda h, b, ph: (h * n_seq + b, 0, 0))
    return pl.pallas_call(
        functools.partial(_moba_prefill_kernel, nb=nb, l=l, per_step=per_step, n_slices=n_slices),
        out_shape=(jax.ShapeDtypeStruct((n_seq * t_total, ATT_WIDTH), BF16),
                   jax.ShapeDtypeStruct((n_steps * per_step, PAGE_SIZE, hd), F32),
                   jax.ShapeDtypeStruct((n_steps * per_step, PAGE_SIZE, hd), F32)),
        grid_spec=pltpu.PrefetchScalarGridSpec(
            num_scalar_prefetch=1, grid=(ATT_HEADS, n_seq),
            in_specs=[tok(OFF_AQ), tok(OFF_AK), tok(OFF_AV),
                      pl.BlockSpec((None, blk, nb * blk), lambda h, b, ph: (h, 0, 0)),
                      pl.BlockSpec(memory_space=pl.ANY), pl.BlockSpec(memory_space=pl.ANY)],
            out_specs=(pl.BlockSpec((t_total, hd), lambda h, b, ph: (b, h)), gathered, gathered),
            scratch_shapes=[pltpu.VMEM((hd, t_total), BF16), pltpu.VMEM((1, t_total), F32),
                            pltpu.VMEM((1, t_total), F32), pltpu.VMEM((hd, t_total), F32),
                            pltpu.SemaphoreType.DMA((2, per_step))]),
        compiler_params=_cparams(("arbitrary", "arbitrary")),
        name="moba_prefill",
    )(phys, u, u, u, bias, cache_k, cache_v)


def _moba_select_kernel(km_ref, q_ref, o_ref, *, nblk):
    g = jnp.sum(km_ref[0] * q_ref[0][None], axis=-1)
    idx = lax.broadcasted_iota(jnp.int32, (nblk, ATT_HEADS), 0).astype(F32)
    o_ref[...] = jnp.zeros_like(o_ref)
    for j in range(MOBA_TOPK):
        mx = jnp.max(g, axis=0, keepdims=True)
        first = jnp.min(jnp.where(g == mx, idx, float(nblk)), axis=0, keepdims=True)
        o_ref[0, j:j + 1, :] = first.astype(jnp.int32)
        g = jnp.where(idx == first, -jnp.inf, g)


def _moba_select(kmean, q3):
    n_seq, nblk = kmean.shape[:2]
    return pl.pallas_call(
        functools.partial(_moba_select_kernel, nblk=nblk),
        out_shape=jax.ShapeDtypeStruct((n_seq, SUBLANES, ATT_HEADS), jnp.int32),
        grid=(n_seq,),
        in_specs=[pl.BlockSpec((1, nblk, ATT_HEADS, HEAD_DIM), lambda b: (b, 0, 0, 0)),
                  pl.BlockSpec((1, ATT_HEADS, HEAD_DIM), lambda b: (b, 0, 0))],
        out_specs=pl.BlockSpec((1, SUBLANES, ATT_HEADS), lambda b: (b, 0, 0)),
        compiler_params=_cparams(("parallel",)),
        name="moba_select",
    )(kmean, q3)


def _moba_decode_kernel(logi_ref, q_ref, kn_ref, vn_ref, bias_ref, bias0_ref, kg_ref, vg_ref, o_ref):
    b = pl.program_id(0)
    scale = HEAD_DIM ** -0.5

    for h in range(ATT_HEADS):
        q = q_ref[0, h:h + 1, :]
        q8 = jnp.broadcast_to(q, (SUBLANES, HEAD_DIM)).astype(BF16)
        logits = []
        for j in range(N_SEL_PAGES):
            lp = logi_ref[(b * ATT_HEADS + h) * N_SEL_PAGES + j]
            logits.append(_dot_nt(q8, kg_ref[h * N_SEL_PAGES + j].astype(BF16)) * scale
                          + bias_ref[h, pl.ds(lp, 1), :])
        self_logit = jnp.sum(q * kn_ref[0, h:h + 1, :], axis=-1, keepdims=True) * scale + bias0_ref[h:h + 1, 0:1]
        m = self_logit
        for lg in logits:
            m = jnp.maximum(m, jnp.max(lg, axis=-1, keepdims=True))
        p_self = jnp.exp(self_logit - m)
        den = p_self
        acc = p_self * vn_ref[0, h:h + 1, :]
        for j in range(N_SEL_PAGES):
            p = jnp.exp(logits[j] - m)
            den += jnp.sum(p, axis=-1, keepdims=True)
            acc += _dot(p.astype(BF16), vg_ref[h * N_SEL_PAGES + j].astype(BF16))
        o_ref[0, h:h + 1, :] = (acc / den)[0:1, :]


def _moba_decode(kg, vg, logi, q3, k3, v3, bias_s, bias0):
    n_seq = q3.shape[0]
    n_slots = ATT_HEADS * N_SEL_PAGES
    tok = pl.BlockSpec((1, ATT_HEADS, HEAD_DIM), lambda b, lg: (b, 0, 0))
    gathered = pl.BlockSpec((n_slots, PAGE_SIZE, HEAD_DIM), lambda b, lg: (b, 0, 0))
    return pl.pallas_call(
        _moba_decode_kernel,
        out_shape=jax.ShapeDtypeStruct((n_seq, ATT_HEADS, HEAD_DIM), F32),
        grid_spec=pltpu.PrefetchScalarGridSpec(
            num_scalar_prefetch=1, grid=(n_seq,),
            in_specs=[tok, tok, tok,
                      pl.BlockSpec(bias_s.shape, lambda b, lg: (0, 0, 0)),
                      pl.BlockSpec(bias0.shape, lambda b, lg: (0, 0)),
                      gathered, gathered],
            out_specs=tok),
        compiler_params=_cparams(("parallel",)),
        name="moba_decode",
    )(logi, q3, k3, v3, bias_s, bias0, kg, vg)


def _pad_rows(a, rows):
    n, w = a.shape
    return jnp.pad(a[:, None, :], ((0, 0), (0, rows - 1), (0, 0))).reshape(n * rows, w)


def _vec_state(a):
    if a.ndim == 2:
        a = jnp.broadcast_to(a[:, :, None], a.shape + (LANES,))
    return jnp.pad(a, ((0, 0), (0, SUBLANES - a.shape[1]), (0, 0)))


def kernel(x_prompt, x_sample, cache_k, cache_v, page_table, state_hgrn, state_mlstm_c, state_mlstm_n,
           state_mlstm_m, state_mlstm_conv, ln_ffn1, w_ffn1_gate, w_ffn1_up, w_ffn1_down, ln_mix, w_in, w_out,
           hgrn_lb_logits, hgrn_out_norm, rel_bias, mlstm_conv_w, mlstm_conv_b, mlstm_gate_bias,
           mlstm_out_norm, ln_ffn2, w_ffn2_gate, w_ffn2_up, w_ffn2_down, ln_final):
    depth = w_in.shape[0]
    bp, tp, d = x_prompt.shape
    bs, ts, _ = x_sample.shape
    assert ts == 1 and tp % MOBA_BLOCK == 0
    n_pages = page_table.shape[1]
    past_len = n_pages * PAGE_SIZE
    assert past_len % MOBA_BLOCK == 0 and past_len // MOBA_BLOCK >= MOBA_TOPK

    w_in_b = w_in.astype(BF16)
    w_gate_b = jnp.pad(w_in[:, :, N_MAIN:], ((0, 0), (0, 0), (0, LANES - N_GATE))).astype(BF16)
    row3 = lambda a: a.reshape(depth, 1, -1)
    ln1, lnm, ln2 = row3(ln_ffn1), row3(ln_mix), row3(ln_ffn2)
    lb = jnp.cumsum(jax.nn.softmax(hgrn_lb_logits.astype(F32), axis=0), axis=0)
    lb = lb - lb[0]
    llb, l1m, oml = row3(jnp.log(lb)), row3(jnp.log1p(-lb)), row3(1.0 - lb)
    hnorm, mnorm = row3(hgrn_out_norm), row3(mlstm_out_norm)
    conv_b = row3(mlstm_conv_b)
    gate_b = row3(jnp.pad(mlstm_gate_bias, ((0, 0), (0, LANES - N_GATE))))
    bias_tab = rel_bias.T.astype(F32)
    bias_p = _t5_bias_prompt(bias_tab, tp)
    bias_s = _t5_bias_sample(bias_tab, past_len)
    bias0 = jnp.broadcast_to(bias_tab[:, 0:1], (ATT_HEADS, LANES))

    zeros = lambda *s: jnp.zeros(s, F32)
    tm_f = 512
    tm_p = 1024 if (bp * tp) % 1024 == 0 else 512
    tm_o = 512
    ll_p = 256

    xp = x_prompt.reshape(bp * tp, d)
    xs = x_sample.reshape(bs, d)
    outs = {k: [] for k in ("ks", "vs", "hgp", "hgs", "cp", "cs", "np", "ns", "mp", "ms", "cvp", "cvs")}
    u_all = []

    for l in range(depth):
        xs, wg1, wu1, wd1 = _ffn(xs, ln1, w_ffn1_gate, w_ffn1_up, w_ffn1_down, l, bs, cast=True)
        xp, kmean = _ffn(xp, ln1, wg1, wu1, wd1, l, tm_f, kmean_of=(cache_k, page_table))
        us, gs = _inproj(xs, lnm, w_in_b, w_gate_b, l, bs)
        u, gates = _inproj(xp, lnm, w_in_b, w_gate_b, l, tm_p)

        heads = lambda a: a.reshape(bs, ATT_HEADS, HEAD_DIM)
        q3, k3, v3 = heads(us[:, OFF_AQ:OFF_AK]), heads(us[:, OFF_AK:OFF_AV]), heads(us[:, OFF_AV:OFF_MQK])
        sel = _moba_select(kmean.reshape(bs, -1, ATT_HEADS, HEAD_DIM), q3)[:, :MOBA_TOPK, :]
        sel = jnp.transpose(sel, (0, 2, 1))
        logi = (sel[..., None] * PAGES_PER_BLOCK
                + jnp.arange(PAGES_PER_BLOCK, dtype=jnp.int32)).reshape(bs, ATT_HEADS, N_SEL_PAGES)
        phys = jnp.take_along_axis(page_table[:, None, :], logi, axis=2)

        o_hg, hg_s = _hgrn(u, llb, l1m, oml, hnorm, zeros(bp, HG_HEADS, HEAD_DIM, HEAD_DIM), l, bp, tp, tp)
        o_att, kg, vg = _moba_prefill(u, bias_p, bp, tp, cache_k, cache_v, phys.reshape(-1), l)
        o_ml, c_s, n_s, m_s = _mlstm(u, gates, mlstm_conv_w, conv_b, gate_b, mnorm,
                                     zeros(bp, ML_HEADS, HEAD_DIM, HEAD_DIM), zeros(bp, SUBLANES, LANES),
                                     zeros(bp, SUBLANES, LANES), zeros(bp, SUBLANES, 2 * ML_WIDTH),
                                     l, bp, tp, tp, ll_p)
        u3 = u.reshape(bp, tp, N_MAIN)
        u_all.append(u3)
        outs["hgp"].append(hg_s)
        outs["cp"].append(c_s)
        outs["np"].append(n_s[:, :ML_HEADS])
        outs["mp"].append(m_s[:, :ML_HEADS, 0])
        outs["cvp"].append(u3[:, tp - (CONV_W - 1):, OFF_MQK:OFF_MV])

        o_hg_s, hg_ss = _hgrn(_pad_rows(us[:, :OFF_AQ], HG_CHUNK), llb, l1m, oml, hnorm, state_hgrn[l],
                              l, bs, HG_CHUNK, 1)
        conv0 = jnp.pad(state_mlstm_conv[l], ((0, 0), (SUBLANES - (CONV_W - 1), 0), (0, 0)))
        o_ml_s, c_ss, n_ss, m_ss = _mlstm(_pad_rows(us[:, OFF_MQK:], SAMPLE_PAD), _pad_rows(gs, SAMPLE_PAD),
                                          mlstm_conv_w, conv_b, gate_b, mnorm, state_mlstm_c[l],
                                          _vec_state(state_mlstm_n[l]), _vec_state(state_mlstm_m[l]), conv0,
                                          l, bs, SAMPLE_PAD, 1, SAMPLE_PAD, col0=OFF_MQK)
        o_att_s = _moba_decode(kg, vg, logi.reshape(-1), q3, k3, v3, bias_s, bias0).reshape(bs, ATT_WIDTH)
        o_att_s = o_att_s.astype(BF16)
        first = lambda a, rows: a.reshape(bs, rows, -1)[:, 0]

        xs, w_out_b = _outproj(xs, first(o_hg_s, HG_CHUNK), o_att_s, first(o_ml_s, SAMPLE_PAD), w_out, l, bs,
                               cast=True)
        xp = _outproj(xp, o_hg, o_att, o_ml, w_out_b, l, tm_o)
        closing = ln_final.reshape(1, d) if l == depth - 1 else None
        xs, wg2, wu2, wd2 = _ffn(xs, ln2, w_ffn2_gate, w_ffn2_up, w_ffn2_down, l, bs, cast=True,
                                 final_gain=closing)
        xp = _ffn(xp, ln2, wg2, wu2, wd2, l, tm_f, final_gain=closing)
        outs["ks"].append(k3.reshape(bs, 1, ATT_HEADS, HEAD_DIM))
        outs["vs"].append(v3.reshape(bs, 1, ATT_HEADS, HEAD_DIM))
        outs["hgs"].append(hg_ss)
        outs["cs"].append(c_ss)
        outs["ns"].append(n_ss[:, :ML_HEADS])
        outs["ms"].append(m_ss[:, :ML_HEADS, 0])
        outs["cvs"].append(jnp.concatenate([state_mlstm_conv[l][:, 1:], us[:, None, OFF_MQK:OFF_MV]], axis=1))

    y_prompt = xp.reshape(bp, tp, d)
    y_sample = xs.reshape(bs, 1, d)
    st = {k: jnp.stack(v) for k, v in outs.items()}
    heads_p = lambda lo, hi: jnp.stack([u3[:, :, lo:hi] for u3 in u_all]).reshape(depth, bp, tp, ATT_HEADS, HEAD_DIM)
    st["kp"], st["vp"] = heads_p(OFF_AK, OFF_AV), heads_p(OFF_AV, OFF_MQK)
    return (y_prompt, y_sample, st["kp"], st["vp"], st["ks"], st["vs"], st["hgp"], st["hgs"], st["cp"], st["cs"],
            st["np"], st["ns"], st["mp"], st["ms"], st["cvp"], st["cvs"])
```

```python
import functools
import math

import jax
import jax.numpy as jnp
import numpy as np
from jax import lax
from jax.experimental import pallas as pl
from jax.experimental.pallas import tpu as pltpu

F32 = jnp.float32
BF16 = jnp.bfloat16

HEAD_DIM = 128
HG_HEADS = 4
ATT_HEADS = 8
ML_HEADS = 4
HG_WIDTH = HG_HEADS * HEAD_DIM
ATT_WIDTH = ATT_HEADS * HEAD_DIM
ML_WIDTH = ML_HEADS * HEAD_DIM
N_MAIN = 4 * HG_WIDTH + 3 * ATT_WIDTH + 4 * ML_WIDTH
N_GATE = 2 * ML_HEADS
OFF_HQ, OFF_HF, OFF_HI, OFF_HG = 0, HG_WIDTH, 2 * HG_WIDTH, 3 * HG_WIDTH
OFF_AQ = 4 * HG_WIDTH
OFF_AK = OFF_AQ + ATT_WIDTH
OFF_AV = OFF_AK + ATT_WIDTH
OFF_MQK = OFF_AV + ATT_WIDTH
OFF_MV = OFF_MQK + 2 * ML_WIDTH
OFF_MO = OFF_MV + ML_WIDTH

PAGE_SIZE = 128
MOBA_BLOCK = 256
MOBA_TOPK = 3
PAGES_PER_BLOCK = MOBA_BLOCK // PAGE_SIZE
N_SEL_PAGES = MOBA_TOPK * PAGES_PER_BLOCK
N_BUCKETS = 32
MAX_DISTANCE = 4096
HG_CHUNK = 16
CONV_W = 4
EPS = 1e-6
NEG_BIG = -1e30
LANES = 128
SUBLANES = 8
VMEM_LIMIT = 48 * 1024 * 1024
SAMPLE_PAD = 128

_HI = lax.Precision.HIGHEST


def _cparams(sem):
    return pltpu.CompilerParams(dimension_semantics=sem, vmem_limit_bytes=VMEM_LIMIT)


def _rms(x, g):
    return x * lax.rsqrt(jnp.mean(x * x, axis=-1, keepdims=True) + EPS) * g


def _log_sigmoid(x):
    return jnp.minimum(x, 0.0) - jnp.log1p(jnp.exp(-jnp.abs(x)))


def _dot(a, b):
    return jnp.dot(a, b, preferred_element_type=F32)


def _dot_nt(a, b, precision=None):
    return lax.dot_general(a, b, (((1,), (1,)), ((), ())), preferred_element_type=F32, precision=precision)


def _dot_tn(a, b):
    return lax.dot_general(a, b, (((0,), (0,)), ((), ())), preferred_element_type=F32)


def _ffn_body(x_ref, g_ref, wg_ref, wu_ref, wd_ref, o_ref, xn_ref, nj, side_job=None, bf16_copies=None,
              final_ref=None):
    j = pl.program_id(1)

    @pl.when(j == 0)
    def _():
        xn_ref[...] = _rms(x_ref[...], g_ref[...]).astype(BF16)
        o_ref[...] = jnp.zeros_like(o_ref)

    if side_job is not None:
        side_job()
    wg, wu, wd = wg_ref[...], wu_ref[...], wd_ref[...]
    if bf16_copies is not None:
        wg, wu, wd = wg.astype(BF16), wu.astype(BF16), wd.astype(BF16)
        for ref, w in zip(bf16_copies, (wg, wu, wd)):
            ref[...] = w
    xn = xn_ref[...]
    a = _dot(xn, wg)
    b = _dot(xn, wu)
    h = (a * jax.nn.sigmoid(a)) * b
    o_ref[...] += _dot(h.astype(BF16), wd)

    @pl.when(j == nj - 1)
    def _():
        y = x_ref[...] + 0.5 * o_ref[...]
        o_ref[...] = y if final_ref is None else _rms(y, final_ref[...])


def _ffn_cast_kernel(x_ref, g_ref, wg_ref, wu_ref, wd_ref, *rest, nj, final):
    final_ref, rest = (rest[0], rest[1:]) if final else (None, rest)
    o_ref, wgb_ref, wub_ref, wdb_ref, xn_ref = rest
    _ffn_body(x_ref, g_ref, wg_ref, wu_ref, wd_ref, o_ref, xn_ref, nj, bf16_copies=(wgb_ref, wub_ref, wdb_ref),
              final_ref=final_ref)


def _ffn_kernel(x_ref, g_ref, wg_ref, wu_ref, wd_ref, *rest, nj, final):
    final_ref, rest = (rest[0], rest[1:]) if final else (None, rest)
    o_ref, xn_ref = rest
    _ffn_body(x_ref, g_ref, wg_ref, wu_ref, wd_ref, o_ref, xn_ref, nj, final_ref=final_ref)


def _ffn_kmean_kernel(pages_ref, x_ref, g_ref, wg_ref, wu_ref, wd_ref, ck_ref, o_ref, km_ref, xn_ref, pbuf, sem,
                      *, nj, l, group, n_blocks):
    s = pl.program_id(0) * nj + pl.program_id(1)
    n_steps = pl.num_programs(0) * nj

    def block_of(step, g):
        return jnp.minimum(step * group + g, n_blocks - 1)

    def page_copy(step, g, p):
        slot = step % 2
        page = pages_ref[block_of(step, g) * PAGES_PER_BLOCK + p]
        k = g * PAGES_PER_BLOCK + p
        return pltpu.make_async_copy(ck_ref.at[l, page], pbuf.at[slot, k], sem.at[slot, k])

    def start_all(step):
        for g in range(group):
            for p in range(PAGES_PER_BLOCK):
                page_copy(step, g, p).start(priority=1)

    @pl.when(s == 0)
    def _():
        start_all(s)

    @pl.when(s + 1 < n_steps)
    def _():
        start_all(s + 1)

    def block_means():
        slot = s % 2
        for g in range(group):
            for p in range(PAGES_PER_BLOCK):
                page_copy(s, g, p).wait()
        for g in range(group):
            tot = jnp.zeros((ATT_HEADS, HEAD_DIM), F32)
            for p in range(PAGES_PER_BLOCK):
                tot += jnp.sum(pbuf[slot, g * PAGES_PER_BLOCK + p], axis=0)
            km_ref[block_of(s, g)] = tot * (1.0 / MOBA_BLOCK)

    _ffn_body(x_ref, g_ref, wg_ref, wu_ref, wd_ref, o_ref, xn_ref, nj, side_job=block_means)


def _ffn(x, ln, wg, wu, wd, l, tm, kmean_of=None, cast=False, final_gain=None):
    m, d = x.shape
    final = final_gain is not None
    extra_in = [pl.BlockSpec((1, d), lambda i, j: (0, 0))] if final else []
    extra_arg = [final_gain] if final else []
    f = wg.shape[-1]
    tf = 512 if f % 512 == 0 else f
    nj = f // tf
    grid = (m // tm, nj)
    x_in = pl.BlockSpec((tm, d), lambda i, j, *_: (i, 0))
    ln_in = pl.BlockSpec((None, 1, d), lambda i, j, *_: (l, 0, 0))
    w_col = pl.BlockSpec((d, tf), lambda i, j, *_: (0, j))
    w_row = pl.BlockSpec((tf, d), lambda i, j, *_: (j, 0))
    in_specs = [x_in, ln_in, w_col, w_col, w_row]
    x_out = pl.BlockSpec((tm, d), lambda i, j, *_: (i, 0))
    if cast:
        return pl.pallas_call(
            functools.partial(_ffn_cast_kernel, nj=nj, final=final),
            out_shape=(jax.ShapeDtypeStruct((m, d), F32), jax.ShapeDtypeStruct((d, f), BF16),
                       jax.ShapeDtypeStruct((d, f), BF16), jax.ShapeDtypeStruct((f, d), BF16)),
            grid=grid,
            in_specs=[x_in, ln_in,
                      pl.BlockSpec((None, d, tf), lambda i, j: (l, 0, j)),
                      pl.BlockSpec((None, d, tf), lambda i, j: (l, 0, j)),
                      pl.BlockSpec((None, tf, d), lambda i, j: (l, j, 0))] + extra_in,
            out_specs=(x_out, w_col, w_col, w_row),
            scratch_shapes=[pltpu.VMEM((tm, d), BF16)],
            compiler_params=_cparams(("arbitrary", "arbitrary")),
            name="ffn_cast",
        )(x, ln, wg, wu, wd, *extra_arg)
    if kmean_of is None:
        return pl.pallas_call(
            functools.partial(_ffn_kernel, nj=nj, final=final),
            out_shape=jax.ShapeDtypeStruct((m, d), F32),
            grid=grid, in_specs=in_specs + extra_in, out_specs=x_out,
            scratch_shapes=[pltpu.VMEM((tm, d), BF16)],
            compiler_params=_cparams(("parallel", "arbitrary")),
            name="ffn",
        )(x, ln, wg, wu, wd, *extra_arg)
    assert not final
    cache_k, page_table = kmean_of
    n_blocks = page_table.size // PAGES_PER_BLOCK
    group = -(-n_blocks // (grid[0] * grid[1]))
    n_buf = group * PAGES_PER_BLOCK
    return pl.pallas_call(
        functools.partial(_ffn_kmean_kernel, nj=nj, l=l, group=group, n_blocks=n_blocks),
        out_shape=(jax.ShapeDtypeStruct((m, d), F32),
                   jax.ShapeDtypeStruct((n_blocks, ATT_HEADS, HEAD_DIM), F32)),
        grid_spec=pltpu.PrefetchScalarGridSpec(
            num_scalar_prefetch=1, grid=grid,
            in_specs=in_specs + [pl.BlockSpec(memory_space=pl.ANY)],
            out_specs=(x_out, pl.BlockSpec((n_blocks, ATT_HEADS, HEAD_DIM), lambda i, j, pg: (0, 0, 0))),
            scratch_shapes=[pltpu.VMEM((tm, d), BF16),
                            pltpu.VMEM((2, n_buf, PAGE_SIZE, ATT_HEADS, HEAD_DIM), F32),
                            pltpu.SemaphoreType.DMA((2, n_buf))]),
        compiler_params=_cparams(("arbitrary", "arbitrary")),
        name="ffn_kmean",
    )(page_table.reshape(-1), x, ln, wg, wu, wd, cache_k)


def _inproj_kernel(x_ref, g_ref, w_ref, wgate_ref, u_ref, gate_ref, *rest):
    wb_ref, xn_ref = rest if len(rest) == 2 else (None, rest[0])

    @pl.when(pl.program_id(1) == 0)
    def _():
        xn = _rms(x_ref[...], g_ref[...]).astype(BF16)
        xn_ref[...] = xn
        gate_ref[...] = _dot(xn, wgate_ref[...])

    w = w_ref[...]
    if wb_ref is not None:
        w = w.astype(BF16)
        wb_ref[...] = w
    u_ref[...] = _dot(xn_ref[...], w)


def _inproj(x, ln, w, wgate, l, tm, cast=False):
    m, d = x.shape
    tn = 1024
    w_out = pl.BlockSpec((d, tn), lambda i, j: (0, j))
    out_shape = [jax.ShapeDtypeStruct((m, N_MAIN), F32), jax.ShapeDtypeStruct((m, LANES), F32)]
    out_specs = [pl.BlockSpec((tm, tn), lambda i, j: (i, j)), pl.BlockSpec((tm, LANES), lambda i, j: (i, 0))]
    if cast:
        out_shape.append(jax.ShapeDtypeStruct((d, N_MAIN), BF16))
        out_specs.append(w_out)
    return pl.pallas_call(
        _inproj_kernel,
        out_shape=tuple(out_shape),
        grid=(m // tm, N_MAIN // tn),
        in_specs=[
            pl.BlockSpec((tm, d), lambda i, j: (i, 0)),
            pl.BlockSpec((None, 1, d), lambda i, j: (l, 0, 0)),
            pl.BlockSpec((None, d, tn), lambda i, j: (l, 0, j)) if w.ndim == 3 else w_out,
            pl.BlockSpec((None, d, LANES), lambda i, j: (l, 0, 0)),
        ],
        out_specs=tuple(out_specs),
        scratch_shapes=[pltpu.VMEM((tm, d), BF16)],
        compiler_params=_cparams(("arbitrary", "arbitrary") if cast else ("parallel", "arbitrary")),
        name="inproj_cast" if cast else "inproj",
    )(x, ln, w, wgate)


def _outproj_kernel(x_ref, a_ref, b_ref, c_ref, w_ref, o_ref):
    acc = _dot(a_ref[...], w_ref[0:HG_WIDTH, :])
    acc += _dot(b_ref[...], w_ref[HG_WIDTH:HG_WIDTH + ATT_WIDTH, :])
    acc += _dot(c_ref[...], w_ref[HG_WIDTH + ATT_WIDTH:, :])
    o_ref[...] = x_ref[...] + acc


def _outproj_cast_kernel(x_ref, a_ref, b_ref, c_ref, w_ref, o_ref, wb_ref):
    w = w_ref[...].astype(BF16)
    wb_ref[...] = w
    acc = _dot(a_ref[...], w[0:HG_WIDTH, :])
    acc += _dot(b_ref[...], w[HG_WIDTH:HG_WIDTH + ATT_WIDTH, :])
    acc += _dot(c_ref[...], w[HG_WIDTH + ATT_WIDTH:, :])
    o_ref[...] = x_ref[...] + acc


def _outproj(x, o_hg, o_att, o_ml, w, l, tm, cast=False):
    m, d = x.shape
    dm = w.shape[-2]
    if cast:
        tn = min(512, d)
        row = lambda width: pl.BlockSpec((tm, width), lambda i, n: (i, 0))
        return pl.pallas_call(
            _outproj_cast_kernel,
            out_shape=(jax.ShapeDtypeStruct((m, d), F32), jax.ShapeDtypeStruct((dm, d), BF16)),
            grid=(m // tm, d // tn),
            in_specs=[pl.BlockSpec((tm, tn), lambda i, n: (i, n)), row(HG_WIDTH), row(ATT_WIDTH), row(ML_WIDTH),
                      pl.BlockSpec((None, dm, tn), lambda i, n: (l, 0, n))],
            out_specs=(pl.BlockSpec((tm, tn), lambda i, n: (i, n)), pl.BlockSpec((dm, tn), lambda i, n: (0, n))),
            compiler_params=_cparams(("arbitrary", "arbitrary")),
            name="outproj_cast",
        )(x, o_hg, o_att, o_ml, w)
    return pl.pallas_call(
        _outproj_kernel,
        out_shape=jax.ShapeDtypeStruct((m, d), F32),
        grid=(m // tm,),
        in_specs=[
            pl.BlockSpec((tm, d), lambda i: (i, 0)),
            pl.BlockSpec((tm, HG_WIDTH), lambda i: (i, 0)),
            pl.BlockSpec((tm, ATT_WIDTH), lambda i: (i, 0)),
            pl.BlockSpec((tm, ML_WIDTH), lambda i: (i, 0)),
            pl.BlockSpec((dm, d), lambda i: (0, 0)),
        ],
        out_specs=pl.BlockSpec((tm, d), lambda i: (i, 0)),
        compiler_params=_cparams(("parallel",)),
        name="outproj",
    )(x, o_hg, o_att, o_ml, w)


def _hgrn_kernel(uq_ref, uf_ref, ui_ref, ug_ref, llb_ref, l1m_ref, oml_ref, norm_ref, s0_ref,
                 o_ref, sout_ref, st_ref, oacc_ref, qd_ref, kd_ref, dl_ref, *, tt, t_valid, t_total):
    c = HG_CHUNK
    ti = pl.program_id(1)
    nt = pl.num_programs(1)

    @pl.when(ti == 0)
    def _():
        for h in range(HG_HEADS):
            st_ref[h] = s0_ref[0, h].T

    rr = min(tt, LANES)
    nc = rr // c
    row = lax.broadcasted_iota(jnp.int32, (rr, rr), 0)
    col = lax.broadcasted_iota(jnp.int32, (rr, rr), 1)
    same_chunk = (row // c) == (col // c)
    tril_bd = jnp.where(same_chunk, jnp.where(col <= row, 1.0, 0.0), 0.0)
    ones_bd = jnp.where(same_chunk, 1.0, 0.0)
    s_idx = lax.broadcasted_iota(jnp.int32, (nc, c, HEAD_DIM), 1)
    sel_r = lax.broadcasted_iota(jnp.int32, (rr, c * rr), 0)
    sel_c = lax.broadcasted_iota(jnp.int32, (rr, c * rr), 1)
    sel_big = jnp.where((sel_c // rr) == (sel_r % c),
                        jnp.where(((sel_c % rr) // c) == (sel_r // c), 1.0, 0.0), 0.0).astype(BF16)
    ones = jnp.ones((HEAD_DIM, HEAD_DIM), BF16)
    llb = llb_ref[...]
    l1m = l1m_ref[...]
    oml = oml_ref[...]

    def intra(si, carry):
        r0 = pl.multiple_of(si * rr, rr)
        uf = uf_ref[pl.ds(r0, rr), :]
        b = l1m + _log_sigmoid(uf)
        lf = jnp.maximum(llb, b) + jnp.log1p(jnp.exp(-jnp.abs(llb - b)))
        kk = oml * jax.nn.sigmoid(-uf)
        if t_valid < t_total:
            valid = (ti * tt + r0 + lax.broadcasted_iota(jnp.int32, (rr, HG_WIDTH), 0)) < t_valid
            lf = jnp.where(valid, lf, 0.0)
            kk = jnp.where(valid, kk, 0.0)
        cum = jnp.dot(tril_bd, lf, preferred_element_type=F32, precision=_HI)
        tot = jnp.dot(ones_bd, lf, preferred_element_type=F32, precision=_HI)
        uq = uq_ref[pl.ds(r0, rr), :]
        ui = ui_ref[pl.ds(r0, rr), :]
        qd_ref[pl.ds(r0, rr), :] = (uq * jnp.exp(cum)).astype(BF16)
        kd_ref[pl.ds(r0, rr), :] = (kk * jnp.exp(tot - cum)).astype(BF16)
        dl_ref[pl.ds(r0, rr), :] = jnp.exp(tot)
        for h in range(HG_HEADS):
            hs = slice(h * HEAD_DIM, (h + 1) * HEAD_DIM)
            q3 = uq[:, hs].reshape(nc, c, HEAD_DIM)
            k3 = kk[:, hs].reshape(nc, c, HEAD_DIM)
            cm3 = cum[:, hs].reshape(nc, c, HEAD_DIM)
            ws = []
            for t in range(c):
                ns = SUBLANES if t < SUBLANES else c
                k3t, cm3t = k3[:, :ns, :], cm3[:, :ns, :]
                dec = jnp.exp(jnp.broadcast_to(cm3[:, t:t + 1, :], cm3t.shape) - cm3t)
                w = (jnp.broadcast_to(q3[:, t:t + 1, :], k3t.shape) * k3t) * dec
                w = jnp.where(s_idx[:, :ns, :] <= t, w, 0.0)
                if ns < c:
                    w = jnp.concatenate([w, jnp.zeros((nc, c - ns, HEAD_DIM), F32)], axis=1)
                ws.append(w.reshape(rr, HEAD_DIM))
            w_all = jnp.concatenate(ws, axis=0).astype(BF16)
            r = _dot(w_all, ones)
            vt = jnp.concatenate([ui[:, hs]] * c, axis=0)
            oacc_ref[pl.ds(r0, rr), hs] = _dot(sel_big, (r * vt).astype(BF16))
        return carry

    lax.fori_loop(0, tt // rr, intra, 0)

    def inter(ci, carry):
        r0 = pl.multiple_of(ci * c, c)
        for h in range(HG_HEADS):
            hs = slice(h * HEAD_DIM, (h + 1) * HEAD_DIM)
            st = st_ref[h]
            oacc_ref[pl.ds(r0, c), hs] += _dot_nt(qd_ref[pl.ds(r0, c), hs], st.astype(BF16))
            st_ref[h] = st * dl_ref[pl.ds(r0, 1), hs] + _dot_tn(ui_ref[pl.ds(r0, c), hs].astype(BF16),
                                                                 kd_ref[pl.ds(r0, c), hs])
        return carry

    lax.fori_loop(0, tt // c, inter, 0, unroll=min(4, tt // c))

    for h in range(HG_HEADS):
        hs = slice(h * HEAD_DIM, (h + 1) * HEAD_DIM)
        g = ug_ref[:, hs]
        y = _rms(oacc_ref[:, hs], norm_ref[:, hs]) * (g * jax.nn.sigmoid(g))
        o_ref[:, hs] = y.astype(o_ref.dtype)

    @pl.when(ti == nt - 1)
    def _():
        for h in range(HG_HEADS):
            sout_ref[0, h] = st_ref[h].T


def _hgrn(u, llb, l1m, oml, norm, s0, l, n_seq, t_total, t_valid):
    tt = min(t_total, 512)
    nt = t_total // tt
    cb = HG_WIDTH
    ublock = lambda k: pl.BlockSpec((tt, cb), lambda b, t: (b * nt + t, k))
    par = pl.BlockSpec((None, 1, cb), lambda b, t: (l, 0, 0))
    sblock = pl.BlockSpec((1, HG_HEADS, HEAD_DIM, HEAD_DIM), lambda b, t: (b, 0, 0, 0))
    return pl.pallas_call(
        functools.partial(_hgrn_kernel, tt=tt, t_valid=t_valid, t_total=t_total),
        out_shape=(jax.ShapeDtypeStruct((n_seq * t_total, cb), BF16),
                   jax.ShapeDtypeStruct((n_seq, HG_HEADS, HEAD_DIM, HEAD_DIM), F32)),
        grid=(n_seq, nt),
        in_specs=[ublock(OFF_HQ // cb), ublock(OFF_HF // cb), ublock(OFF_HI // cb), ublock(OFF_HG // cb),
                  par, par, par, par, sblock],
        out_specs=(pl.BlockSpec((tt, cb), lambda b, t: (b * nt + t, 0)), sblock),
        scratch_shapes=[pltpu.VMEM((HG_HEADS, HEAD_DIM, HEAD_DIM), F32), pltpu.VMEM((tt, cb), F32),
                        pltpu.VMEM((tt, cb), BF16), pltpu.VMEM((tt, cb), BF16), pltpu.VMEM((tt, cb), F32)],
        compiler_params=_cparams(("parallel", "arbitrary")),
        name="hgrn",
    )(u, u, u, u, llb, l1m, oml, norm, s0)


def _mlstm_kernel(qk_ref, v_ref, og_ref, gate_ref, cw_ref, cb_ref, gb_ref, norm_ref, c0_ref, n0_ref, m0_ref,
                  conv0_ref, o_ref, cout_ref, nout_ref, mout_ref, c_ref, n_ref, m_ref, carry_ref,
                  *, ll, t_valid, t_total):
    ti = pl.program_id(1)
    nt = pl.num_programs(1)

    @pl.when(ti == 0)
    def _():
        c_ref[...] = c0_ref[0]
        n_ref[...] = n0_ref[0]
        m_ref[...] = m0_ref[0]
        carry_ref[...] = conv0_ref[0]

    x = qk_ref[...]
    xe = jnp.concatenate([carry_ref[...], x], axis=0)
    cw = cw_ref[...]
    y = cb_ref[...] + cw[3:4, :] * x
    for j in range(1, CONV_W):
        y += cw[3 - j:4 - j, :] * xe[SUBLANES - j:SUBLANES - j + ll, :]
    carry_ref[...] = x[ll - SUBLANES:, :]
    qk = y * jax.nn.sigmoid(y)

    g = gate_ref[...] + gb_ref[...]
    lf = _log_sigmoid(g)
    ipre = g
    if t_valid < t_total:
        valid = (ti * ll + lax.broadcasted_iota(jnp.int32, (ll, LANES), 0)) < t_valid
        lf = jnp.where(valid, lf, 0.0)
        ipre = jnp.where(valid, ipre, NEG_BIG)
    row = lax.broadcasted_iota(jnp.int32, (ll, ll), 0)
    col = lax.broadcasted_iota(jnp.int32, (ll, ll), 1)
    causal = col <= row
    cum = jnp.dot(causal.astype(F32), lf, preferred_element_type=F32, precision=_HI)
    lane = lax.broadcasted_iota(jnp.int32, (ll, LANES), 1)
    a_t = jnp.where(lane < ML_HEADS, ipre, cum).T

    for h in range(ML_HEADS):
        hs = slice(h * HEAD_DIM, (h + 1) * HEAD_DIM)
        q = qk[:, hs]
        k = qk[:, ML_WIDTH + h * HEAD_DIM:ML_WIDTH + (h + 1) * HEAD_DIM] * HEAD_DIM ** -0.5
        v = v_ref[:, hs]
        col_cum = cum[:, ML_HEADS + h:ML_HEADS + h + 1]
        col_i = ipre[:, h:h + 1]
        row_cum = a_t[ML_HEADS + h:ML_HEADS + h + 1, :]
        row_i = a_t[h:h + 1, :]
        m_prev = m_ref[h:h + 1, 0:1]
        log_d = jnp.where(causal, col_cum - row_cum + row_i, -jnp.inf)
        m_inter = col_cum + m_prev
        m_t = jnp.maximum(m_inter, jnp.max(log_d, axis=-1, keepdims=True))
        w_inter = jnp.exp(m_inter - m_t)
        qb, kb, vb = q.astype(BF16), k.astype(BF16), v.astype(BF16)
        s = _dot_nt(qb, kb) * jnp.exp(log_d - m_t)
        c_old = c_ref[h]
        n_old = n_ref[h:h + 1, :]
        num = w_inter * _dot(qb, c_old.astype(BF16)) + _dot(s.astype(BF16), vb)
        den = w_inter * jnp.sum(q * n_old, axis=-1, keepdims=True) + jnp.sum(s, axis=-1, keepdims=True)
        hh = num / jnp.maximum(jnp.abs(den), jnp.exp(-m_t))
        m_new = m_t[ll - 1:ll, :]
        cum_last = col_cum[ll - 1:ll, :]
        w_k = jnp.exp(cum_last - col_cum + col_i - m_new)
        decay = jnp.exp(cum_last + m_prev - m_new)
        kw = k * w_k
        c_ref[h] = decay * c_old + _dot_tn(kw.astype(BF16), vb)
        n_ref[h:h + 1, :] = decay * n_old + jnp.sum(kw, axis=0, keepdims=True)
        m_ref[h:h + 1, :] = jnp.broadcast_to(m_new, (1, LANES))
        og = og_ref[:, hs]
        o_ref[:, hs] = (_rms(hh, norm_ref[:, hs]) * jax.nn.sigmoid(og)).astype(o_ref.dtype)

    @pl.when(ti == nt - 1)
    def _():
        cout_ref[0] = c_ref[...]
        nout_ref[0] = n_ref[...]
        mout_ref[0] = m_ref[...]


def _mlstm(u, gates, cw, cb, gb, norm, c0, n0, m0, conv0, l, n_seq, t_total, t_valid, ll, col0=0):
    nt = t_total // ll
    ublock = lambda w, k: pl.BlockSpec((ll, w), lambda b, t: (b * nt + t, k - col0 // w))
    par = lambda r, w: pl.BlockSpec((None, r, w), lambda b, t: (l, 0, 0))
    cblock = pl.BlockSpec((1, ML_HEADS, HEAD_DIM, HEAD_DIM), lambda b, t: (b, 0, 0, 0))
    vblock = pl.BlockSpec((1, SUBLANES, LANES), lambda b, t: (b, 0, 0))
    return pl.pallas_call(
        functools.partial(_mlstm_kernel, ll=ll, t_valid=t_valid, t_total=t_total),
        out_shape=(jax.ShapeDtypeStruct((n_seq * t_total, ML_WIDTH), BF16),
                   jax.ShapeDtypeStruct((n_seq, ML_HEADS, HEAD_DIM, HEAD_DIM), F32),
                   jax.ShapeDtypeStruct((n_seq, SUBLANES, LANES), F32),
                   jax.ShapeDtypeStruct((n_seq, SUBLANES, LANES), F32)),
        grid=(n_seq, nt),
        in_specs=[ublock(2 * ML_WIDTH, OFF_MQK // (2 * ML_WIDTH)), ublock(ML_WIDTH, OFF_MV // ML_WIDTH),
                  ublock(ML_WIDTH, OFF_MO // ML_WIDTH),
                  pl.BlockSpec((ll, LANES), lambda b, t: (b * nt + t, 0)),
                  par(CONV_W, 2 * ML_WIDTH), par(1, 2 * ML_WIDTH), par(1, LANES), par(1, ML_WIDTH),
                  cblock, vblock, vblock,
                  pl.BlockSpec((1, SUBLANES, 2 * ML_WIDTH), lambda b, t: (b, 0, 0))],
        out_specs=(pl.BlockSpec((ll, ML_WIDTH), lambda b, t: (b * nt + t, 0)), cblock, vblock, vblock),
        scratch_shapes=[pltpu.VMEM((ML_HEADS, HEAD_DIM, HEAD_DIM), F32), pltpu.VMEM((SUBLANES, LANES), F32),
                        pltpu.VMEM((SUBLANES, LANES), F32), pltpu.VMEM((SUBLANES, 2 * ML_WIDTH), F32)],
        compiler_params=_cparams(("parallel", "arbitrary")),
        name="mlstm",
    )(u, u, u, gates, cw, cb, gb, norm, c0, n0, m0, conv0)


def _t5_bucket_np(rel):
    rel = np.asarray(rel, np.int64)
    max_exact = N_BUCKETS // 2
    relf = np.maximum(rel, 1).astype(np.float64)
    large = max_exact + (np.log(relf / max_exact) / math.log(MAX_DISTANCE / max_exact)
                         * (N_BUCKETS - max_exact)).astype(np.int64)
    return np.where(rel < max_exact, rel, np.minimum(large, N_BUCKETS - 1)).astype(np.int32)


def _t5_thresholds(max_rel):
    buckets = _t5_bucket_np(np.arange(max_rel + 1))
    out = []
    for b in range(1, N_BUCKETS):
        hit = np.nonzero(buckets >= b)[0]
        out.append(int(hit[0]) if hit.size else None)
    return out


def _t5_bias_kernel(tab_ref, o_ref, *, rel_fn, thresholds):
    rel = rel_fn(o_ref.shape[1:])
    acc = [jnp.full(o_ref.shape[1:], tab_ref[h, 0], F32) for h in range(ATT_HEADS)]
    for b, th in enumerate(thresholds, start=1):
        if th is None:
            continue
        reached = rel >= th
        for h in range(ATT_HEADS):
            acc[h] = jnp.where(reached, tab_ref[h, b], acc[h])
    for h in range(ATT_HEADS):
        o_ref[h] = acc[h]


def _t5_bias_prompt(bias_tab, t_total):
    blk = MOBA_BLOCK
    nb = t_total // blk

    def rel_fn(shape):
        return (pl.program_id(0) * blk + lax.broadcasted_iota(jnp.int32, shape, 1)
                - lax.broadcasted_iota(jnp.int32, shape, 0))

    return pl.pallas_call(
        functools.partial(_t5_bias_kernel, rel_fn=rel_fn, thresholds=_t5_thresholds(t_total)),
        out_shape=jax.ShapeDtypeStruct((ATT_HEADS, blk, nb * blk), F32),
        grid=(nb,),
        in_specs=[pl.BlockSpec(memory_space=pltpu.SMEM)],
        out_specs=pl.BlockSpec((ATT_HEADS, blk, blk), lambda d: (0, 0, d)),
        compiler_params=_cparams(("parallel",)),
        name="t5_bias_prompt",
    )(bias_tab)


def _t5_bias_sample(bias_tab, past_len):
    n_pages = past_len // PAGE_SIZE

    def rel_fn(shape):
        return past_len - (lax.broadcasted_iota(jnp.int32, shape, 0) * PAGE_SIZE
                           + lax.broadcasted_iota(jnp.int32, shape, 1))

    return pl.pallas_call(
        functools.partial(_t5_bias_kernel, rel_fn=rel_fn, thresholds=_t5_thresholds(past_len)),
        out_shape=jax.ShapeDtypeStruct((ATT_HEADS, n_pages, PAGE_SIZE), F32),
        grid=(1,),
        in_specs=[pl.BlockSpec(memory_space=pltpu.SMEM)],
        out_specs=pl.BlockSpec((ATT_HEADS, n_pages, PAGE_SIZE), lambda i: (0, 0, 0)),
        compiler_params=_cparams(("arbitrary",)),
        name="t5_bias_sample",
    )(bias_tab)


def _moba_prefill_kernel(phys_ref, q_ref, k_ref, v_ref, bias_ref, ck_ref, cv_ref, o_ref, kg_ref, vg_ref,
                         vt_ref, m_ref, l_ref, acc_ref, sem, *, nb, l, per_step, n_slices):
    blk = MOBA_BLOCK
    t_total = nb * blk
    scale = HEAD_DIM ** -0.5
    step = pl.program_id(0) * pl.num_programs(1) + pl.program_id(1)

    def slice_copy(which, k):
        idx = jnp.minimum(step * per_step + k, n_slices - 1)
        head = (idx // N_SEL_PAGES) % ATT_HEADS
        src, dst = ((ck_ref, kg_ref), (cv_ref, vg_ref))[which]
        return pltpu.make_async_copy(src.at[l, phys_ref[idx], :, head, :], dst.at[k], sem.at[which, k])

    for k in range(per_step):
        slice_copy(0, k).start(priority=k % 2)
        slice_copy(1, k).start(priority=(k + 1) % 2)

    q = q_ref[...]
    k = k_ref[...]
    qb = q.astype(BF16)
    kb = k.astype(BF16)
    vt_ref[...] = v_ref[...].T.astype(BF16)

    kmean = jnp.concatenate(
        [jnp.mean(k[n * blk:(n + 1) * blk, :], axis=0, keepdims=True) for n in range(nb)]
        + [jnp.zeros((SUBLANES - nb % SUBLANES, HEAD_DIM), F32)] * (nb % SUBLANES != 0), axis=0)
    nrow = kmean.shape[0]
    gate = _dot_nt(kmean, q, precision=_HI)
    brow = lax.broadcasted_iota(jnp.int32, (nrow, t_total), 0)
    qblk = lax.broadcasted_iota(jnp.int32, (nrow, t_total), 1) // blk
    qblk_row = qblk[0:1, :]
    chosen = []
    for n in range(nb - 1):
        gn = gate[n:n + 1, :]
        ahead = jnp.where(brow < n, jnp.where(gate >= gn, 1.0, 0.0), jnp.where(gate > gn, 1.0, 0.0))
        rank = jnp.sum(jnp.where(brow < qblk, ahead, 0.0), axis=0, keepdims=True)
        chosen.append(jnp.where(qblk_row > n, jnp.where(rank < MOBA_TOPK, 1.0, 0.0), 0.0))

    krow = lax.broadcasted_iota(jnp.int32, (blk, blk), 0)
    qcol = lax.broadcasted_iota(jnp.int32, (blk, blk), 1)
    for i in range(nb):
        rs = slice(i * blk, (i + 1) * blk)
        lg = _dot_nt(kb[rs], qb[rs]) * scale + bias_ref[:, 0:blk]
        lg = jnp.where(krow <= qcol, lg, -jnp.inf)
        m0 = jnp.max(lg, axis=0, keepdims=True)
        p = jnp.exp(lg - m0)
        m_ref[:, rs] = m0
        l_ref[:, rs] = jnp.sum(p, axis=0, keepdims=True)
        acc_ref[:, rs] = _dot(vt_ref[:, rs], p.astype(BF16))

    for n in range(nb - 1):
        ks = slice(n * blk, (n + 1) * blk)
        qs = slice((n + 1) * blk, t_total)
        nq = t_total - (n + 1) * blk
        lg = _dot_nt(kb[ks], qb[qs]) * scale + bias_ref[:, blk:blk + nq]
        lg = jnp.where(chosen[n][:, qs] > 0.0, lg, -jnp.inf)
        m_old = m_ref[:, qs]
        m_new = jnp.maximum(m_old, jnp.max(lg, axis=0, keepdims=True))
        alpha = jnp.exp(m_old - m_new)
        p = jnp.exp(lg - m_new)
        m_ref[:, qs] = m_new
        l_ref[:, qs] = alpha * l_ref[:, qs] + jnp.sum(p, axis=0, keepdims=True)
        acc_ref[:, qs] = alpha * acc_ref[:, qs] + _dot(vt_ref[:, ks], p.astype(BF16))

    o_ref[...] = (acc_ref[...] / l_ref[...]).T.astype(o_ref.dtype)

    for k in range(per_step):
        slice_copy(0, k).wait()
        slice_copy(1, k).wait()


def _moba_prefill(u, bias, n_seq, t_total, cache_k, cache_v, phys, l):
    blk = MOBA_BLOCK
    nb = t_total // blk
    hd = HEAD_DIM
    n_slices = phys.size
    n_steps = ATT_HEADS * n_seq
    per_step = -(-n_slices // n_steps)
    tok = lambda off: pl.BlockSpec((t_total, hd), lambda h, b, ph: (b, off // hd + h))
    gathered = pl.BlockSpec((per_step, PAGE_SIZE, hd), lambda h, b, ph: (h * n_seq + b, 0, 0))
    return pl.pallas_call(
        functools.partial(_moba_prefill_kernel, nb=nb, l=l, per_step=per_step, n_slices=n_slices),
        out_shape=(jax.ShapeDtypeStruct((n_seq * t_total, ATT_WIDTH), BF16),
                   jax.ShapeDtypeStruct((n_steps * per_step, PAGE_SIZE, hd), F32),
                   jax.ShapeDtypeStruct((n_steps * per_step, PAGE_SIZE, hd), F32)),
        grid_spec=pltpu.PrefetchScalarGridSpec(
            num_scalar_prefetch=1, grid=(ATT_HEADS, n_seq),
            in_specs=[tok(OFF_AQ), tok(OFF_AK), tok(OFF_AV),
                      pl.BlockSpec((None, blk, nb * blk), lambda h, b, ph: (h, 0, 0)),
                      pl.BlockSpec(memory_space=pl.ANY), pl.BlockSpec(memory_space=pl.ANY)],
            out_specs=(pl.BlockSpec((t_total, hd), lambda h, b, ph: (b, h)), gathered, gathered),
            scratch_shapes=[pltpu.VMEM((hd, t_total), BF16), pltpu.VMEM((1, t_total), F32),
                            pltpu.VMEM((1, t_total), F32), pltpu.VMEM((hd, t_total), F32),
                            pltpu.SemaphoreType.DMA((2, per_step))]),
        compiler_params=_cparams(("arbitrary", "arbitrary")),
        name="moba_prefill",
    )(phys, u, u, u, bias, cache_k, cache_v)


def _moba_select_kernel(km_ref, q_ref, o_ref, *, nblk):
    g = jnp.sum(km_ref[0] * q_ref[0][None], axis=-1)
    idx = lax.broadcasted_iota(jnp.int32, (nblk, ATT_HEADS), 0).astype(F32)
    o_ref[...] = jnp.zeros_like(o_ref)
    for j in range(MOBA_TOPK):
        mx = jnp.max(g, axis=0, keepdims=True)
        first = jnp.min(jnp.where(g == mx, idx, float(nblk)), axis=0, keepdims=True)
        o_ref[0, j:j + 1, :] = first.astype(jnp.int32)
        g = jnp.where(idx == first, -jnp.inf, g)


def _moba_select(kmean, q3):
    n_seq, nblk = kmean.shape[:2]
    return pl.pallas_call(
        functools.partial(_moba_select_kernel, nblk=nblk),
        out_shape=jax.ShapeDtypeStruct((n_seq, SUBLANES, ATT_HEADS), jnp.int32),
        grid=(n_seq,),
        in_specs=[pl.BlockSpec((1, nblk, ATT_HEADS, HEAD_DIM), lambda b: (b, 0, 0, 0)),
                  pl.BlockSpec((1, ATT_HEADS, HEAD_DIM), lambda b: (b, 0, 0))],
        out_specs=pl.BlockSpec((1, SUBLANES, ATT_HEADS), lambda b: (b, 0, 0)),
        compiler_params=_cparams(("parallel",)),
        name="moba_select",
    )(kmean, q3)


def _moba_decode_kernel(logi_ref, q_ref, kn_ref, vn_ref, bias_ref, bias0_ref, kg_ref, vg_ref, o_ref):
    b = pl.program_id(0)
    scale = HEAD_DIM ** -0.5

    for h in range(ATT_HEADS):
        q = q_ref[0, h:h + 1, :]
        q8 = jnp.broadcast_to(q, (SUBLANES, HEAD_DIM)).astype(BF16)
        logits = []
        for j in range(N_SEL_PAGES):
            lp = logi_ref[(b * ATT_HEADS + h) * N_SEL_PAGES + j]
            logits.append(_dot_nt(q8, kg_ref[h * N_SEL_PAGES + j].astype(BF16)) * scale
                          + bias_ref[h, pl.ds(lp, 1), :])
        self_logit = jnp.sum(q * kn_ref[0, h:h + 1, :], axis=-1, keepdims=True) * scale + bias0_ref[h:h + 1, 0:1]
        m = self_logit
        for lg in logits:
            m = jnp.maximum(m, jnp.max(lg, axis=-1, keepdims=True))
        p_self = jnp.exp(self_logit - m)
        den = p_self
        acc = p_self * vn_ref[0, h:h + 1, :]
        for j in range(N_SEL_PAGES):
            p = jnp.exp(logits[j] - m)
            den += jnp.sum(p, axis=-1, keepdims=True)
            acc += _dot(p.astype(BF16), vg_ref[h * N_SEL_PAGES + j].astype(BF16))
        o_ref[0, h:h + 1, :] = (acc / den)[0:1, :]


def _moba_decode(kg, vg, logi, q3, k3, v3, bias_s, bias0):
    n_seq = q3.shape[0]
    n_slots = ATT_HEADS * N_SEL_PAGES
    tok = pl.BlockSpec((1, ATT_HEADS, HEAD_DIM), lambda b, lg: (b, 0, 0))
    gathered = pl.BlockSpec((n_slots, PAGE_SIZE, HEAD_DIM), lambda b, lg: (b, 0, 0))
    return pl.pallas_call(
        _moba_decode_kernel,
        out_shape=jax.ShapeDtypeStruct((n_seq, ATT_HEADS, HEAD_DIM), F32),
        grid_spec=pltpu.PrefetchScalarGridSpec(
            num_scalar_prefetch=1, grid=(n_seq,),
            in_specs=[tok, tok, tok,
                      pl.BlockSpec(bias_s.shape, lambda b, lg: (0, 0, 0)),
                      pl.BlockSpec(bias0.shape, lambda b, lg: (0, 0)),
                      gathered, gathered],
            out_specs=tok),
        compiler_params=_cparams(("parallel",)),
        name="moba_decode",
    )(logi, q3, k3, v3, bias_s, bias0, kg, vg)


def _pad_rows(a, rows):
    n, w = a.shape
    return jnp.pad(a[:, None, :], ((0, 0), (0, rows - 1), (0, 0))).reshape(n * rows, w)


def _vec_state(a):
    if a.ndim == 2:
        a = jnp.broadcast_to(a[:, :, None], a.shape + (LANES,))
    return jnp.pad(a, ((0, 0), (0, SUBLANES - a.shape[1]), (0, 0)))


def kernel(x_prompt, x_sample, cache_k, cache_v, page_table, state_hgrn, state_mlstm_c, state_mlstm_n,
           state_mlstm_m, state_mlstm_conv, ln_ffn1, w_ffn1_gate, w_ffn1_up, w_ffn1_down, ln_mix, w_in, w_out,
           hgrn_lb_logits, hgrn_out_norm, rel_bias, mlstm_conv_w, mlstm_conv_b, mlstm_gate_bias,
           mlstm_out_norm, ln_ffn2, w_ffn2_gate, w_ffn2_up, w_ffn2_down, ln_final):
    depth = w_in.shape[0]
    bp, tp, d = x_prompt.shape
    bs, ts, _ = x_sample.shape
    assert ts == 1 and tp % MOBA_BLOCK == 0
    n_pages = page_table.shape[1]
    past_len = n_pages * PAGE_SIZE
    assert past_len % MOBA_BLOCK == 0 and past_len // MOBA_BLOCK >= MOBA_TOPK

    w_in_b = w_in.astype(BF16)
    w_gate_b = jnp.pad(w_in[:, :, N_MAIN:], ((0, 0), (0, 0), (0, LANES - N_GATE))).astype(BF16)
    row3 = lambda a: a.reshape(depth, 1, -1)
    ln1, lnm, ln2 = row3(ln_ffn1), row3(ln_mix), row3(ln_ffn2)
    lb = jnp.cumsum(jax.nn.softmax(hgrn_lb_logits.astype(F32), axis=0), axis=0)
    lb = lb - lb[0]
    llb, l1m, oml = row3(jnp.log(lb)), row3(jnp.log1p(-lb)), row3(1.0 - lb)
    hnorm, mnorm = row3(hgrn_out_norm), row3(mlstm_out_norm)
    conv_b = row3(mlstm_conv_b)
    gate_b = row3(jnp.pad(mlstm_gate_bias, ((0, 0), (0, LANES - N_GATE))))
    bias_tab = rel_bias.T.astype(F32)
    bias_p = _t5_bias_prompt(bias_tab, tp)
    bias_s = _t5_bias_sample(bias_tab, past_len)
    bias0 = jnp.broadcast_to(bias_tab[:, 0:1], (ATT_HEADS, LANES))

    zeros = lambda *s: jnp.zeros(s, F32)
    tm_f = 512
    tm_p = 1024 if (bp * tp) % 1024 == 0 else 512
    tm_o = 512
    ll_p = 256

    xp = x_prompt.reshape(bp * tp, d)
    xs = x_sample.reshape(bs, d)
    outs = {k: [] for k in ("ks", "vs", "hgp", "hgs", "cp", "cs", "np", "ns", "mp", "ms", "cvp", "cvs")}
    u_all = []

    for l in range(depth):
        xs, wg1, wu1, wd1 = _ffn(xs, ln1, w_ffn1_gate, w_ffn1_up, w_ffn1_down, l, bs, cast=True)
        xp, kmean = _ffn(xp, ln1, wg1, wu1, wd1, l, tm_f, kmean_of=(cache_k, page_table))
        us, gs = _inproj(xs, lnm, w_in_b, w_gate_b, l, bs)
        u, gates = _inproj(xp, lnm, w_in_b, w_gate_b, l, tm_p)

        heads = lambda a: a.reshape(bs, ATT_HEADS, HEAD_DIM)
        q3, k3, v3 = heads(us[:, OFF_AQ:OFF_AK]), heads(us[:, OFF_AK:OFF_AV]), heads(us[:, OFF_AV:OFF_MQK])
        sel = _moba_select(kmean.reshape(bs, -1, ATT_HEADS, HEAD_DIM), q3)[:, :MOBA_TOPK, :]
        sel = jnp.transpose(sel, (0, 2, 1))
        logi = (sel[..., None] * PAGES_PER_BLOCK
                + jnp.arange(PAGES_PER_BLOCK, dtype=jnp.int32)).reshape(bs, ATT_HEADS, N_SEL_PAGES)
        phys = jnp.take_along_axis(page_table[:, None, :], logi, axis=2)

        o_hg, hg_s = _hgrn(u, llb, l1m, oml, hnorm, zeros(bp, HG_HEADS, HEAD_DIM, HEAD_DIM), l, bp, tp, tp)
        o_att, kg, vg = _moba_prefill(u, bias_p, bp, tp, cache_k, cache_v, phys.reshape(-1), l)
        o_ml, c_s, n_s, m_s = _mlstm(u, gates, mlstm_conv_w, conv_b, gate_b, mnorm,
                                     zeros(bp, ML_HEADS, HEAD_DIM, HEAD_DIM), zeros(bp, SUBLANES, LANES),
                                     zeros(bp, SUBLANES, LANES), zeros(bp, SUBLANES, 2 * ML_WIDTH),
                                     l, bp, tp, tp, ll_p)
        u3 = u.reshape(bp, tp, N_MAIN)
        u_all.append(u3)
        outs["hgp"].append(hg_s)
        outs["cp"].append(c_s)
        outs["np"].append(n_s[:, :ML_HEADS])
        outs["mp"].append(m_s[:, :ML_HEADS, 0])
        outs["cvp"].append(u3[:, tp - (CONV_W - 1):, OFF_MQK:OFF_MV])

        o_hg_s, hg_ss = _hgrn(_pad_rows(us[:, :OFF_AQ], HG_CHUNK), llb, l1m, oml, hnorm, state_hgrn[l],
                              l, bs, HG_CHUNK, 1)
        conv0 = jnp.pad(state_mlstm_conv[l], ((0, 0), (SUBLANES - (CONV_W - 1), 0), (0, 0)))
        o_ml_s, c_ss, n_ss, m_ss = _mlstm(_pad_rows(us[:, OFF_MQK:], SAMPLE_PAD), _pad_rows(gs, SAMPLE_PAD),
                                          mlstm_conv_w, conv_b, gate_b, mnorm, state_mlstm_c[l],
                                          _vec_state(state_mlstm_n[l]), _vec_state(state_mlstm_m[l]), conv0,
                                          l, bs, SAMPLE_PAD, 1, SAMPLE_PAD, col0=OFF_MQK)
        o_att_s = _moba_decode(kg, vg, logi.reshape(-1), q3, k3, v3, bias_s, bias0).reshape(bs, ATT_WIDTH)
        o_att_s = o_att_s.astype(BF16)
        first = lambda a, rows: a.reshape(bs, rows, -1)[:, 0]

        xs, w_out_b = _outproj(xs, first(o_hg_s, HG_CHUNK), o_att_s, first(o_ml_s, SAMPLE_PAD), w_out, l, bs,
                               cast=True)
        xp = _outproj(xp, o_hg, o_att, o_ml, w_out_b, l, tm_o)
        closing = ln_final.reshape(1, d) if l == depth - 1 else None
        xs, wg2, wu2, wd2 = _ffn(xs, ln2, w_ffn2_gate, w_ffn2_up, w_ffn2_down, l, bs, cast=True,
                                 final_gain=closing)
        xp = _ffn(xp, ln2, wg2, wu2, wd2, l, tm_f, final_gain=closing)
        outs["ks"].append(k3.reshape(bs, 1, ATT_HEADS, HEAD_DIM))
        outs["vs"].append(v3.reshape(bs, 1, ATT_HEADS, HEAD_DIM))
        outs["hgs"].append(hg_ss)
        outs["cs"].append(c_ss)
        outs["ns"].append(n_ss[:, :ML_HEADS])
        outs["ms"].append(m_ss[:, :ML_HEADS, 0])
        outs["cvs"].append(jnp.concatenate([state_mlstm_conv[l][:, 1:], us[:, None, OFF_MQK:OFF_MV]], axis=1))

    y_prompt = xp.reshape(bp, tp, d)
    y_sample = xs.reshape(bs, 1, d)
    st = {k: jnp.stack(v) for k, v in outs.items()}
    heads_p = lambda lo, hi: jnp.stack([u3[:, :, lo:hi] for u3 in u_all]).reshape(depth, bp, tp, ATT_HEADS, HEAD_DIM)
    st["kp"], st["vp"] = heads_p(OFF_AK, OFF_AV), heads_p(OFF_AV, OFF_MQK)
    return (y_prompt, y_sample, st["kp"], st["vp"], st["ks"], st["vs"], st["hgp"], st["hgs"], st["cp"], st["cs"],
            st["np"], st["ns"], st["mp"], st["ms"], st["cvp"], st["cvs"])
```

```python
import functools
import math

import jax
import jax.numpy as jnp
import numpy as np
from jax import lax
from jax.experimental import pallas as pl
from jax.experimental.pallas import tpu as pltpu

F32 = jnp.float32
BF16 = jnp.bfloat16

HEAD_DIM = 128
HG_HEADS = 4
ATT_HEADS = 8
ML_HEADS = 4
HG_WIDTH = HG_HEADS * HEAD_DIM
ATT_WIDTH = ATT_HEADS * HEAD_DIM
ML_WIDTH = ML_HEADS * HEAD_DIM
N_MAIN = 4 * HG_WIDTH + 3 * ATT_WIDTH + 4 * ML_WIDTH
N_GATE = 2 * ML_HEADS
OFF_HQ, OFF_HF, OFF_HI, OFF_HG = 0, HG_WIDTH, 2 * HG_WIDTH, 3 * HG_WIDTH
OFF_AQ = 4 * HG_WIDTH
OFF_AK = OFF_AQ + ATT_WIDTH
OFF_AV = OFF_AK + ATT_WIDTH
OFF_MQK = OFF_AV + ATT_WIDTH
OFF_MV = OFF_MQK + 2 * ML_WIDTH
OFF_MO = OFF_MV + ML_WIDTH

PAGE_SIZE = 128
MOBA_BLOCK = 256
MOBA_TOPK = 3
PAGES_PER_BLOCK = MOBA_BLOCK // PAGE_SIZE
N_SEL_PAGES = MOBA_TOPK * PAGES_PER_BLOCK
N_BUCKETS = 32
MAX_DISTANCE = 4096
HG_CHUNK = 16
CONV_W = 4
EPS = 1e-6
NEG_BIG = -1e30
LANES = 128
SUBLANES = 8
VMEM_LIMIT = 48 * 1024 * 1024
SAMPLE_PAD = 128

_HI = lax.Precision.HIGHEST


def _cparams(sem):
    return pltpu.CompilerParams(dimension_semantics=sem, vmem_limit_bytes=VMEM_LIMIT)


def _rms(x, g):
    return x * lax.rsqrt(jnp.mean(x * x, axis=-1, keepdims=True) + EPS) * g


def _log_sigmoid(x):
    return jnp.minimum(x, 0.0) - jnp.log1p(jnp.exp(-jnp.abs(x)))


def _dot(a, b):
    return jnp.dot(a, b, preferred_element_type=F32)


def _dot_nt(a, b, precision=None):
    return lax.dot_general(a, b, (((1,), (1,)), ((), ())), preferred_element_type=F32, precision=precision)


def _dot_tn(a, b):
    return lax.dot_general(a, b, (((0,), (0,)), ((), ())), preferred_element_type=F32)


def _ffn_body(x_ref, g_ref, wg_ref, wu_ref, wd_ref, o_ref, xn_ref, nj, side_job=None, bf16_copies=None,
              final_ref=None):
    j = pl.program_id(1)

    @pl.when(j == 0)
    def _():
        xn_ref[...] = _rms(x_ref[...], g_ref[...]).astype(BF16)
        o_ref[...] = jnp.zeros_like(o_ref)

    if side_job is not None:
        side_job()
    wg, wu, wd = wg_ref[...], wu_ref[...], wd_ref[...]
    if bf16_copies is not None:
        wg, wu, wd = wg.astype(BF16), wu.astype(BF16), wd.astype(BF16)
        for ref, w in zip(bf16_copies, (wg, wu, wd)):
            ref[...] = w
    xn = xn_ref[...]
    a = _dot(xn, wg)
    b = _dot(xn, wu)
    h = (a * jax.nn.sigmoid(a)) * b
    o_ref[...] += _dot(h.astype(BF16), wd)

    @pl.when(j == nj - 1)
    def _():
        y = x_ref[...] + 0.5 * o_ref[...]
        o_ref[...] = y if final_ref is None else _rms(y, final_ref[...])


def _ffn_cast_kernel(x_ref, g_ref, wg_ref, wu_ref, wd_ref, *rest, nj, final):
    final_ref, rest = (rest[0], rest[1:]) if final else (None, rest)
    o_ref, wgb_ref, wub_ref, wdb_ref, xn_ref = rest
    _ffn_body(x_ref, g_ref, wg_ref, wu_ref, wd_ref, o_ref, xn_ref, nj, bf16_copies=(wgb_ref, wub_ref, wdb_ref),
              final_ref=final_ref)


def _ffn_kernel(x_ref, g_ref, wg_ref, wu_ref, wd_ref, *rest, nj, final):
    final_ref, rest = (rest[0], rest[1:]) if final else (None, rest)
    o_ref, xn_ref = rest
    _ffn_body(x_ref, g_ref, wg_ref, wu_ref, wd_ref, o_ref, xn_ref, nj, final_ref=final_ref)


def _ffn_kmean_kernel(pages_ref, x_ref, g_ref, wg_ref, wu_ref, wd_ref, ck_ref, o_ref, km_ref, xn_ref, pbuf, sem,
                      *, nj, l, group, n_blocks):
    s = pl.program_id(0) * nj + pl.program_id(1)
    n_steps = pl.num_programs(0) * nj

    def block_of(step, g):
        return jnp.minimum(step * group + g, n_blocks - 1)

    def page_copy(step, g, p):
        slot = step % 2
        page = pages_ref[block_of(step, g) * PAGES_PER_BLOCK + p]
        k = g * PAGES_PER_BLOCK + p
        return pltpu.make_async_copy(ck_ref.at[l, page], pbuf.at[slot, k], sem.at[slot, k])

    def start_all(step):
        for g in range(group):
            for p in range(PAGES_PER_BLOCK):
                page_copy(step, g, p).start()

    @pl.when(s == 0)
    def _():
        start_all(s)

    @pl.when(s + 1 < n_steps)
    def _():
        start_all(s + 1)

    def block_means():
        slot = s % 2
        for g in range(group):
            for p in range(PAGES_PER_BLOCK):
                page_copy(s, g, p).wait()
        for g in range(group):
            tot = jnp.zeros((ATT_HEADS, HEAD_DIM), F32)
            for p in range(PAGES_PER_BLOCK):
                tot += jnp.sum(pbuf[slot, g * PAGES_PER_BLOCK + p], axis=0)
            km_ref[block_of(s, g)] = tot * (1.0 / MOBA_BLOCK)

    _ffn_body(x_ref, g_ref, wg_ref, wu_ref, wd_ref, o_ref, xn_ref, nj, side_job=block_means)


def _ffn(x, ln, wg, wu, wd, l, tm, kmean_of=None, cast=False, final_gain=None):
    m, d = x.shape
    final = final_gain is not None
    extra_in = [pl.BlockSpec((1, d), lambda i, j: (0, 0))] if final else []
    extra_arg = [final_gain] if final else []
    f = wg.shape[-1]
    tf = 512 if f % 512 == 0 else f
    nj = f // tf
    grid = (m // tm, nj)
    x_in = pl.BlockSpec((tm, d), lambda i, j, *_: (i, 0))
    ln_in = pl.BlockSpec((None, 1, d), lambda i, j, *_: (l, 0, 0))
    w_col = pl.BlockSpec((d, tf), lambda i, j, *_: (0, j))
    w_row = pl.BlockSpec((tf, d), lambda i, j, *_: (j, 0))
    in_specs = [x_in, ln_in, w_col, w_col, w_row]
    x_out = pl.BlockSpec((tm, d), lambda i, j, *_: (i, 0))
    if cast:
        return pl.pallas_call(
            functools.partial(_ffn_cast_kernel, nj=nj, final=final),
            out_shape=(jax.ShapeDtypeStruct((m, d), F32), jax.ShapeDtypeStruct((d, f), BF16),
                       jax.ShapeDtypeStruct((d, f), BF16), jax.ShapeDtypeStruct((f, d), BF16)),
            grid=grid,
            in_specs=[x_in, ln_in,
                      pl.BlockSpec((None, d, tf), lambda i, j: (l, 0, j)),
                      pl.BlockSpec((None, d, tf), lambda i, j: (l, 0, j)),
                      pl.BlockSpec((None, tf, d), lambda i, j: (l, j, 0))] + extra_in,
            out_specs=(x_out, w_col, w_col, w_row),
            scratch_shapes=[pltpu.VMEM((tm, d), BF16)],
            compiler_params=_cparams(("arbitrary", "arbitrary")),
            name="ffn_cast",
        )(x, ln, wg, wu, wd, *extra_arg)
    if kmean_of is None:
        return pl.pallas_call(
            functools.partial(_ffn_kernel, nj=nj, final=final),
            out_shape=jax.ShapeDtypeStruct((m, d), F32),
            grid=grid, in_specs=in_specs + extra_in, out_specs=x_out,
            scratch_shapes=[pltpu.VMEM((tm, d), BF16)],
            compiler_params=_cparams(("parallel", "arbitrary")),
            name="ffn",
        )(x, ln, wg, wu, wd, *extra_arg)
    assert not final
    cache_k, page_table = kmean_of
    n_blocks = page_table.size // PAGES_PER_BLOCK
    group = -(-n_blocks // (grid[0] * grid[1]))
    n_buf = group * PAGES_PER_BLOCK
    return pl.pallas_call(
        functools.partial(_ffn_kmean_kernel, nj=nj, l=l, group=group, n_blocks=n_blocks),
        out_shape=(jax.ShapeDtypeStruct((m, d), F32),
                   jax.ShapeDtypeStruct((n_blocks, ATT_HEADS, HEAD_DIM), F32)),
        grid_spec=pltpu.PrefetchScalarGridSpec(
            num_scalar_prefetch=1, grid=grid,
            in_specs=in_specs + [pl.BlockSpec(memory_space=pl.ANY)],
            out_specs=(x_out, pl.BlockSpec((n_blocks, ATT_HEADS, HEAD_DIM), lambda i, j, pg: (0, 0, 0))),
            scratch_shapes=[pltpu.VMEM((tm, d), BF16),
                            pltpu.VMEM((2, n_buf, PAGE_SIZE, ATT_HEADS, HEAD_DIM), F32),
                            pltpu.SemaphoreType.DMA((2, n_buf))]),
        compiler_params=_cparams(("arbitrary", "arbitrary")),
        name="ffn_kmean",
    )(page_table.reshape(-1), x, ln, wg, wu, wd, cache_k)


def _inproj_kernel(x_ref, g_ref, w_ref, wgate_ref, u_ref, gate_ref, *rest):
    wb_ref, xn_ref = rest if len(rest) == 2 else (None, rest[0])

    @pl.when(pl.program_id(1) == 0)
    def _():
        xn = _rms(x_ref[...], g_ref[...]).astype(BF16)
        xn_ref[...] = xn
        gate_ref[...] = _dot(xn, wgate_ref[...])

    w = w_ref[...]
    if wb_ref is not None:
        w = w.astype(BF16)
        wb_ref[...] = w
    u_ref[...] = _dot(xn_ref[...], w)


def _inproj(x, ln, w, wgate, l, tm, cast=False):
    m, d = x.shape
    tn = 1024
    w_out = pl.BlockSpec((d, tn), lambda i, j: (0, j))
    out_shape = [jax.ShapeDtypeStruct((m, N_MAIN), F32), jax.ShapeDtypeStruct((m, LANES), F32)]
    out_specs = [pl.BlockSpec((tm, tn), lambda i, j: (i, j)), pl.BlockSpec((tm, LANES), lambda i, j: (i, 0))]
    if cast:
        out_shape.append(jax.ShapeDtypeStruct((d, N_MAIN), BF16))
        out_specs.append(w_out)
    return pl.pallas_call(
        _inproj_kernel,
        out_shape=tuple(out_shape),
        grid=(m // tm, N_MAIN // tn),
        in_specs=[
            pl.BlockSpec((tm, d), lambda i, j: (i, 0)),
            pl.BlockSpec((None, 1, d), lambda i, j: (l, 0, 0)),
            pl.BlockSpec((None, d, tn), lambda i, j: (l, 0, j)) if w.ndim == 3 else w_out,
            pl.BlockSpec((None, d, LANES), lambda i, j: (l, 0, 0)),
        ],
        out_specs=tuple(out_specs),
        scratch_shapes=[pltpu.VMEM((tm, d), BF16)],
        compiler_params=_cparams(("arbitrary", "arbitrary") if cast else ("parallel", "arbitrary")),
        name="inproj_cast" if cast else "inproj",
    )(x, ln, w, wgate)


def _outproj_kernel(x_ref, a_ref, b_ref, c_ref, w_ref, o_ref):
    acc = _dot(a_ref[...], w_ref[0:HG_WIDTH, :])
    acc += _dot(b_ref[...], w_ref[HG_WIDTH:HG_WIDTH + ATT_WIDTH, :])
    acc += _dot(c_ref[...], w_ref[HG_WIDTH + ATT_WIDTH:, :])
    o_ref[...] = x_ref[...] + acc


def _outproj_cast_kernel(x_ref, a_ref, b_ref, c_ref, w_ref, o_ref, wb_ref):
    w = w_ref[...].astype(BF16)
    wb_ref[...] = w
    acc = _dot(a_ref[...], w[0:HG_WIDTH, :])
    acc += _dot(b_ref[...], w[HG_WIDTH:HG_WIDTH + ATT_WIDTH, :])
    acc += _dot(c_ref[...], w[HG_WIDTH + ATT_WIDTH:, :])
    o_ref[...] = x_ref[...] + acc


def _outproj(x, o_hg, o_att, o_ml, w, l, tm, cast=False):
    m, d = x.shape
    dm = w.shape[-2]
    if cast:
        tn = min(512, d)
        row = lambda width: pl.BlockSpec((tm, width), lambda i, n: (i, 0))
        return pl.pallas_call(
            _outproj_cast_kernel,
            out_shape=(jax.ShapeDtypeStruct((m, d), F32), jax.ShapeDtypeStruct((dm, d), BF16)),
            grid=(m // tm, d // tn),
            in_specs=[pl.BlockSpec((tm, tn), lambda i, n: (i, n)), row(HG_WIDTH), row(ATT_WIDTH), row(ML_WIDTH),
                      pl.BlockSpec((None, dm, tn), lambda i, n: (l, 0, n))],
            out_specs=(pl.BlockSpec((tm, tn), lambda i, n: (i, n)), pl.BlockSpec((dm, tn), lambda i, n: (0, n))),
            compiler_params=_cparams(("arbitrary", "arbitrary")),
            name="outproj_cast",
        )(x, o_hg, o_att, o_ml, w)
    return pl.pallas_call(
        _outproj_kernel,
        out_shape=jax.ShapeDtypeStruct((m, d), F32),
        grid=(m // tm,),
        in_specs=[
            pl.BlockSpec((tm, d), lambda i: (i, 0)),
            pl.BlockSpec((tm, HG_WIDTH), lambda i: (i, 0)),
            pl.BlockSpec((tm, ATT_WIDTH), lambda i: (i, 0)),
            pl.BlockSpec((tm, ML_WIDTH), lambda i: (i, 0)),
            pl.BlockSpec((dm, d), lambda i: (0, 0)),
        ],
        out_specs=pl.BlockSpec((tm, d), lambda i: (i, 0)),
        compiler_params=_cparams(("parallel",)),
        name="outproj",
    )(x, o_hg, o_att, o_ml, w)


def _hgrn_kernel(uq_ref, uf_ref, ui_ref, ug_ref, llb_ref, l1m_ref, oml_ref, norm_ref, s0_ref,
                 o_ref, sout_ref, st_ref, oacc_ref, qd_ref, kd_ref, dl_ref, *, tt, t_valid, t_total):
    c = HG_CHUNK
    ti = pl.program_id(1)
    nt = pl.num_programs(1)

    @pl.when(ti == 0)
    def _():
        for h in range(HG_HEADS):
            st_ref[h] = s0_ref[0, h].T

    rr = min(tt, LANES)
    nc = rr // c
    row = lax.broadcasted_iota(jnp.int32, (rr, rr), 0)
    col = lax.broadcasted_iota(jnp.int32, (rr, rr), 1)
    same_chunk = (row // c) == (col // c)
    tril_bd = jnp.where(same_chunk, jnp.where(col <= row, 1.0, 0.0), 0.0)
    ones_bd = jnp.where(same_chunk, 1.0, 0.0)
    s_idx = lax.broadcasted_iota(jnp.int32, (nc, c, HEAD_DIM), 1)
    sel_r = lax.broadcasted_iota(jnp.int32, (rr, c * rr), 0)
    sel_c = lax.broadcasted_iota(jnp.int32, (rr, c * rr), 1)
    sel_big = jnp.where((sel_c // rr) == (sel_r % c),
                        jnp.where(((sel_c % rr) // c) == (sel_r // c), 1.0, 0.0), 0.0).astype(BF16)
    ones = jnp.ones((HEAD_DIM, HEAD_DIM), BF16)
    llb = llb_ref[...]
    l1m = l1m_ref[...]
    oml = oml_ref[...]

    def intra(si, carry):
        r0 = pl.multiple_of(si * rr, rr)
        uf = uf_ref[pl.ds(r0, rr), :]
        b = l1m + _log_sigmoid(uf)
        lf = jnp.maximum(llb, b) + jnp.log1p(jnp.exp(-jnp.abs(llb - b)))
        kk = oml * jax.nn.sigmoid(-uf)
        if t_valid < t_total:
            valid = (ti * tt + r0 + lax.broadcasted_iota(jnp.int32, (rr, HG_WIDTH), 0)) < t_valid
            lf = jnp.where(valid, lf, 0.0)
            kk = jnp.where(valid, kk, 0.0)
        cum = jnp.dot(tril_bd, lf, preferred_element_type=F32, precision=_HI)
        tot = jnp.dot(ones_bd, lf, preferred_element_type=F32, precision=_HI)
        uq = uq_ref[pl.ds(r0, rr), :]
        ui = ui_ref[pl.ds(r0, rr), :]
        qd_ref[pl.ds(r0, rr), :] = (uq * jnp.exp(cum)).astype(BF16)
        kd_ref[pl.ds(r0, rr), :] = (kk * jnp.exp(tot - cum)).astype(BF16)
        dl_ref[pl.ds(r0, rr), :] = jnp.exp(tot)
        for h in range(HG_HEADS):
            hs = slice(h * HEAD_DIM, (h + 1) * HEAD_DIM)
            q3 = uq[:, hs].reshape(nc, c, HEAD_DIM)
            k3 = kk[:, hs].reshape(nc, c, HEAD_DIM)
            cm3 = cum[:, hs].reshape(nc, c, HEAD_DIM)
            ws = []
            for t in range(c):
                ns = SUBLANES if t < SUBLANES else c
                k3t, cm3t = k3[:, :ns, :], cm3[:, :ns, :]
                dec = jnp.exp(jnp.broadcast_to(cm3[:, t:t + 1, :], cm3t.shape) - cm3t)
                w = (jnp.broadcast_to(q3[:, t:t + 1, :], k3t.shape) * k3t) * dec
                w = jnp.where(s_idx[:, :ns, :] <= t, w, 0.0)
                if ns < c:
                    w = jnp.concatenate([w, jnp.zeros((nc, c - ns, HEAD_DIM), F32)], axis=1)
                ws.append(w.reshape(rr, HEAD_DIM))
            w_all = jnp.concatenate(ws, axis=0).astype(BF16)
            r = _dot(w_all, ones)
            vt = jnp.concatenate([ui[:, hs]] * c, axis=0)
            oacc_ref[pl.ds(r0, rr), hs] = _dot(sel_big, (r * vt).astype(BF16))
        return carry

    lax.fori_loop(0, tt // rr, intra, 0)

    def inter(ci, carry):
        r0 = pl.multiple_of(ci * c, c)
        for h in range(HG_HEADS):
            hs = slice(h * HEAD_DIM, (h + 1) * HEAD_DIM)
            st = st_ref[h]
            oacc_ref[pl.ds(r0, c), hs] += _dot_nt(qd_ref[pl.ds(r0, c), hs], st.astype(BF16))
            st_ref[h] = st * dl_ref[pl.ds(r0, 1), hs] + _dot_tn(ui_ref[pl.ds(r0, c), hs].astype(BF16),
                                                                 kd_ref[pl.ds(r0, c), hs])
        return carry

    lax.fori_loop(0, tt // c, inter, 0, unroll=min(16, tt // c))

    for h in range(HG_HEADS):
        hs = slice(h * HEAD_DIM, (h + 1) * HEAD_DIM)
        g = ug_ref[:, hs]
        y = _rms(oacc_ref[:, hs], norm_ref[:, hs]) * (g * jax.nn.sigmoid(g))
        o_ref[:, hs] = y.astype(o_ref.dtype)

    @pl.when(ti == nt - 1)
    def _():
        for h in range(HG_HEADS):
            sout_ref[0, h] = st_ref[h].T


def _hgrn(u, llb, l1m, oml, norm, s0, l, n_seq, t_total, t_valid):
    tt = min(t_total, 512)
    nt = t_total // tt
    cb = HG_WIDTH
    ublock = lambda k: pl.BlockSpec((tt, cb), lambda b, t: (b * nt + t, k))
    par = pl.BlockSpec((None, 1, cb), lambda b, t: (l, 0, 0))
    sblock = pl.BlockSpec((1, HG_HEADS, HEAD_DIM, HEAD_DIM), lambda b, t: (b, 0, 0, 0))
    return pl.pallas_call(
        functools.partial(_hgrn_kernel, tt=tt, t_valid=t_valid, t_total=t_total),
        out_shape=(jax.ShapeDtypeStruct((n_seq * t_total, cb), BF16),
                   jax.ShapeDtypeStruct((n_seq, HG_HEADS, HEAD_DIM, HEAD_DIM), F32)),
        grid=(n_seq, nt),
        in_specs=[ublock(OFF_HQ // cb), ublock(OFF_HF // cb), ublock(OFF_HI // cb), ublock(OFF_HG // cb),
                  par, par, par, par, sblock],
        out_specs=(pl.BlockSpec((tt, cb), lambda b, t: (b * nt + t, 0)), sblock),
        scratch_shapes=[pltpu.VMEM((HG_HEADS, HEAD_DIM, HEAD_DIM), F32), pltpu.VMEM((tt, cb), F32),
                        pltpu.VMEM((tt, cb), BF16), pltpu.VMEM((tt, cb), BF16), pltpu.VMEM((tt, cb), F32)],
        compiler_params=_cparams(("parallel", "arbitrary")),
        name="hgrn",
    )(u, u, u, u, llb, l1m, oml, norm, s0)


def _mlstm_kernel(qk_ref, v_ref, og_ref, gate_ref, cw_ref, cb_ref, gb_ref, norm_ref, c0_ref, n0_ref, m0_ref,
                  conv0_ref, o_ref, cout_ref, nout_ref, mout_ref, c_ref, n_ref, m_ref, carry_ref,
                  *, ll, t_valid, t_total):
    ti = pl.program_id(1)
    nt = pl.num_programs(1)

    @pl.when(ti == 0)
    def _():
        c_ref[...] = c0_ref[0]
        n_ref[...] = n0_ref[0]
        m_ref[...] = m0_ref[0]
        carry_ref[...] = conv0_ref[0]

    x = qk_ref[...]
    xe = jnp.concatenate([carry_ref[...], x], axis=0)
    cw = cw_ref[...]
    y = cb_ref[...] + cw[3:4, :] * x
    for j in range(1, CONV_W):
        y += cw[3 - j:4 - j, :] * xe[SUBLANES - j:SUBLANES - j + ll, :]
    carry_ref[...] = x[ll - SUBLANES:, :]
    qk = y * jax.nn.sigmoid(y)

    g = gate_ref[...] + gb_ref[...]
    lf = _log_sigmoid(g)
    ipre = g
    if t_valid < t_total:
        valid = (ti * ll + lax.broadcasted_iota(jnp.int32, (ll, LANES), 0)) < t_valid
        lf = jnp.where(valid, lf, 0.0)
        ipre = jnp.where(valid, ipre, NEG_BIG)
    row = lax.broadcasted_iota(jnp.int32, (ll, ll), 0)
    col = lax.broadcasted_iota(jnp.int32, (ll, ll), 1)
    causal = col <= row
    cum = jnp.dot(causal.astype(F32), lf, preferred_element_type=F32, precision=_HI)
    lane = lax.broadcasted_iota(jnp.int32, (ll, LANES), 1)
    a_t = jnp.where(lane < ML_HEADS, ipre, cum).T

    for h in range(ML_HEADS):
        hs = slice(h * HEAD_DIM, (h + 1) * HEAD_DIM)
        q = qk[:, hs]
        k = qk[:, ML_WIDTH + h * HEAD_DIM:ML_WIDTH + (h + 1) * HEAD_DIM] * HEAD_DIM ** -0.5
        v = v_ref[:, hs]
        col_cum = cum[:, ML_HEADS + h:ML_HEADS + h + 1]
        col_i = ipre[:, h:h + 1]
        row_cum = a_t[ML_HEADS + h:ML_HEADS + h + 1, :]
        row_i = a_t[h:h + 1, :]
        m_prev = m_ref[h:h + 1, 0:1]
        log_d = jnp.where(causal, col_cum - row_cum + row_i, -jnp.inf)
        m_inter = col_cum + m_prev
        m_t = jnp.maximum(m_inter, jnp.max(log_d, axis=-1, keepdims=True))
        w_inter = jnp.exp(m_inter - m_t)
        qb, kb, vb = q.astype(BF16), k.astype(BF16), v.astype(BF16)
        s = _dot_nt(qb, kb) * jnp.exp(log_d - m_t)
        c_old = c_ref[h]
        n_old = n_ref[h:h + 1, :]
        num = w_inter * _dot(qb, c_old.astype(BF16)) + _dot(s.astype(BF16), vb)
        den = w_inter * jnp.sum(q * n_old, axis=-1, keepdims=True) + jnp.sum(s, axis=-1, keepdims=True)
        hh = num / jnp.maximum(jnp.abs(den), jnp.exp(-m_t))
        m_new = m_t[ll - 1:ll, :]
        cum_last = col_cum[ll - 1:ll, :]
        w_k = jnp.exp(cum_last - col_cum + col_i - m_new)
        decay = jnp.exp(cum_last + m_prev - m_new)
        kw = k * w_k
        c_ref[h] = decay * c_old + _dot_tn(kw.astype(BF16), vb)
        n_ref[h:h + 1, :] = decay * n_old + jnp.sum(kw, axis=0, keepdims=True)
        m_ref[h:h + 1, :] = jnp.broadcast_to(m_new, (1, LANES))
        og = og_ref[:, hs]
        o_ref[:, hs] = (_rms(hh, norm_ref[:, hs]) * jax.nn.sigmoid(og)).astype(o_ref.dtype)

    @pl.when(ti == nt - 1)
    def _():
        cout_ref[0] = c_ref[...]
        nout_ref[0] = n_ref[...]
        mout_ref[0] = m_ref[...]


def _mlstm(u, gates, cw, cb, gb, norm, c0, n0, m0, conv0, l, n_seq, t_total, t_valid, ll, col0=0):
    nt = t_total // ll
    ublock = lambda w, k: pl.BlockSpec((ll, w), lambda b, t: (b * nt + t, k - col0 // w))
    par = lambda r, w: pl.BlockSpec((None, r, w), lambda b, t: (l, 0, 0))
    cblock = pl.BlockSpec((1, ML_HEADS, HEAD_DIM, HEAD_DIM), lambda b, t: (b, 0, 0, 0))
    vblock = pl.BlockSpec((1, SUBLANES, LANES), lambda b, t: (b, 0, 0))
    return pl.pallas_call(
        functools.partial(_mlstm_kernel, ll=ll, t_valid=t_valid, t_total=t_total),
        out_shape=(jax.ShapeDtypeStruct((n_seq * t_total, ML_WIDTH), BF16),
                   jax.ShapeDtypeStruct((n_seq, ML_HEADS, HEAD_DIM, HEAD_DIM), F32),
                   jax.ShapeDtypeStruct((n_seq, SUBLANES, LANES), F32),
                   jax.ShapeDtypeStruct((n_seq, SUBLANES, LANES), F32)),
        grid=(n_seq, nt),
        in_specs=[ublock(2 * ML_WIDTH, OFF_MQK // (2 * ML_WIDTH)), ublock(ML_WIDTH, OFF_MV // ML_WIDTH),
                  ublock(ML_WIDTH, OFF_MO // ML_WIDTH),
                  pl.BlockSpec((ll, LANES), lambda b, t: (b * nt + t, 0)),
                  par(CONV_W, 2 * ML_WIDTH), par(1, 2 * ML_WIDTH), par(1, LANES), par(1, ML_WIDTH),
                  cblock, vblock, vblock,
                  pl.BlockSpec((1, SUBLANES, 2 * ML_WIDTH), lambda b, t: (b, 0, 0))],
        out_specs=(pl.BlockSpec((ll, ML_WIDTH), lambda b, t: (b * nt + t, 0)), cblock, vblock, vblock),
        scratch_shapes=[pltpu.VMEM((ML_HEADS, HEAD_DIM, HEAD_DIM), F32), pltpu.VMEM((SUBLANES, LANES), F32),
                        pltpu.VMEM((SUBLANES, LANES), F32), pltpu.VMEM((SUBLANES, 2 * ML_WIDTH), F32)],
        compiler_params=_cparams(("parallel", "arbitrary")),
        name="mlstm",
    )(u, u, u, gates, cw, cb, gb, norm, c0, n0, m0, conv0)


def _t5_bucket_np(rel):
    rel = np.asarray(rel, np.int64)
    max_exact = N_BUCKETS // 2
    relf = np.maximum(rel, 1).astype(np.float64)
    large = max_exact + (np.log(relf / max_exact) / math.log(MAX_DISTANCE / max_exact)
                         * (N_BUCKETS - max_exact)).astype(np.int64)
    return np.where(rel < max_exact, rel, np.minimum(large, N_BUCKETS - 1)).astype(np.int32)


def _t5_thresholds(max_rel):
    buckets = _t5_bucket_np(np.arange(max_rel + 1))
    out = []
    for b in range(1, N_BUCKETS):
        hit = np.nonzero(buckets >= b)[0]
        out.append(int(hit[0]) if hit.size else None)
    return out


def _t5_bias_kernel(tab_ref, o_ref, *, rel_fn, thresholds):
    rel = rel_fn(o_ref.shape[1:])
    acc = [jnp.full(o_ref.shape[1:], tab_ref[h, 0], F32) for h in range(ATT_HEADS)]
    for b, th in enumerate(thresholds, start=1):
        if th is None:
            continue
        reached = rel >= th
        for h in range(ATT_HEADS):
            acc[h] = jnp.where(reached, tab_ref[h, b], acc[h])
    for h in range(ATT_HEADS):
        o_ref[h] = acc[h]


def _t5_bias_prompt(bias_tab, t_total):
    blk = MOBA_BLOCK
    nb = t_total // blk

    def rel_fn(shape):
        return (pl.program_id(0) * blk + lax.broadcasted_iota(jnp.int32, shape, 1)
                - lax.broadcasted_iota(jnp.int32, shape, 0))

    return pl.pallas_call(
        functools.partial(_t5_bias_kernel, rel_fn=rel_fn, thresholds=_t5_thresholds(t_total)),
        out_shape=jax.ShapeDtypeStruct((ATT_HEADS, blk, nb * blk), F32),
        grid=(nb,),
        in_specs=[pl.BlockSpec(memory_space=pltpu.SMEM)],
        out_specs=pl.BlockSpec((ATT_HEADS, blk, blk), lambda d: (0, 0, d)),
        compiler_params=_cparams(("parallel",)),
        name="t5_bias_prompt",
    )(bias_tab)


def _t5_bias_sample(bias_tab, past_len):
    n_pages = past_len // PAGE_SIZE

    def rel_fn(shape):
        return past_len - (lax.broadcasted_iota(jnp.int32, shape, 0) * PAGE_SIZE
                           + lax.broadcasted_iota(jnp.int32, shape, 1))

    return pl.pallas_call(
        functools.partial(_t5_bias_kernel, rel_fn=rel_fn, thresholds=_t5_thresholds(past_len)),
        out_shape=jax.ShapeDtypeStruct((ATT_HEADS, n_pages, PAGE_SIZE), F32),
        grid=(1,),
        in_specs=[pl.BlockSpec(memory_space=pltpu.SMEM)],
        out_specs=pl.BlockSpec((ATT_HEADS, n_pages, PAGE_SIZE), lambda i: (0, 0, 0)),
        compiler_params=_cparams(("arbitrary",)),
        name="t5_bias_sample",
    )(bias_tab)


def _moba_prefill_kernel(phys_ref, q_ref, k_ref, v_ref, bias_ref, ck_ref, cv_ref, o_ref, kg_ref, vg_ref,
                         vt_ref, m_ref, l_ref, acc_ref, sem, *, nb, l, per_step, n_slices):
    blk = MOBA_BLOCK
    t_total = nb * blk
    scale = HEAD_DIM ** -0.5
    step = pl.program_id(0) * pl.num_programs(1) + pl.program_id(1)

    def slice_copy(which, k):
        idx = jnp.minimum(step * per_step + k, n_slices - 1)
        head = (idx // N_SEL_PAGES) % ATT_HEADS
        src, dst = ((ck_ref, kg_ref), (cv_ref, vg_ref))[which]
        return pltpu.make_async_copy(src.at[l, phys_ref[idx], :, head, :], dst.at[k], sem.at[which, k])

    for k in range(per_step):
        slice_copy(0, k).start(priority=k % 2)
        slice_copy(1, k).start(priority=(k + 1) % 2)

    q = q_ref[...]
    k = k_ref[...]
    qb = q.astype(BF16)
    kb = k.astype(BF16)
    vt_ref[...] = v_ref[...].T.astype(BF16)

    kmean = jnp.concatenate(
        [jnp.mean(k[n * blk:(n + 1) * blk, :], axis=0, keepdims=True) for n in range(nb)]
        + [jnp.zeros((SUBLANES - nb % SUBLANES, HEAD_DIM), F32)] * (nb % SUBLANES != 0), axis=0)
    nrow = kmean.shape[0]
    gate = _dot_nt(kmean, q, precision=_HI)
    brow = lax.broadcasted_iota(jnp.int32, (nrow, t_total), 0)
    qblk = lax.broadcasted_iota(jnp.int32, (nrow, t_total), 1) // blk
    qblk_row = qblk[0:1, :]
    chosen = []
    for n in range(nb - 1):
        gn = gate[n:n + 1, :]
        ahead = jnp.where(brow < n, jnp.where(gate >= gn, 1.0, 0.0), jnp.where(gate > gn, 1.0, 0.0))
        rank = jnp.sum(jnp.where(brow < qblk, ahead, 0.0), axis=0, keepdims=True)
        chosen.append(jnp.where(qblk_row > n, jnp.where(rank < MOBA_TOPK, 1.0, 0.0), 0.0))

    krow = lax.broadcasted_iota(jnp.int32, (blk, blk), 0)
    qcol = lax.broadcasted_iota(jnp.int32, (blk, blk), 1)
    for i in range(nb):
        rs = slice(i * blk, (i + 1) * blk)
        lg = _dot_nt(kb[rs], qb[rs]) * scale + bias_ref[:, 0:blk]
        lg = jnp.where(krow <= qcol, lg, -jnp.inf)
        m0 = jnp.max(lg, axis=0, keepdims=True)
        p = jnp.exp(lg - m0)
        m_ref[:, rs] = m0
        l_ref[:, rs] = jnp.sum(p, axis=0, keepdims=True)
        acc_ref[:, rs] = _dot(vt_ref[:, rs], p.astype(BF16))

    for n in range(nb - 1):
        ks = slice(n * blk, (n + 1) * blk)
        qs = slice((n + 1) * blk, t_total)
        nq = t_total - (n + 1) * blk
        lg = _dot_nt(kb[ks], qb[qs]) * scale + bias_ref[:, blk:blk + nq]
        lg = jnp.where(chosen[n][:, qs] > 0.0, lg, -jnp.inf)
        m_old = m_ref[:, qs]
        m_new = jnp.maximum(m_old, jnp.max(lg, axis=0, keepdims=True))
        alpha = jnp.exp(m_old - m_new)
        p = jnp.exp(lg - m_new)
        m_ref[:, qs] = m_new
        l_ref[:, qs] = alpha * l_ref[:, qs] + jnp.sum(p, axis=0, keepdims=True)
        acc_ref[:, qs] = alpha * acc_ref[:, qs] + _dot(vt_ref[:, ks], p.astype(BF16))

    o_ref[...] = (acc_ref[...] / l_ref[...]).T.astype(o_ref.dtype)

    for k in range(per_step):
        slice_copy(0, k).wait()
        slice_copy(1, k).wait()


def _moba_prefill(u, bias, n_seq, t_total, cache_k, cache_v, phys, l):
    blk = MOBA_BLOCK
    nb = t_total // blk
    hd = HEAD_DIM
    n_slices = phys.size
    n_steps = ATT_HEADS * n_seq
    per_step = -(-n_slices // n_steps)
    tok = lambda off: pl.BlockSpec((t_total, hd), lambda h, b, ph: (b, off // hd + h))
    gathered = pl.BlockSpec((per_step, PAGE_SIZE, hd), lambda h, b, ph: (h * n_seq + b, 0, 0))
    return pl.pallas_call(
        functools.partial(_moba_prefill_kernel, nb=nb, l=l, per_step=per_step, n_slices=n_slices),
        out_shape=(jax.ShapeDtypeStruct((n_seq * t_total, ATT_WIDTH), BF16),
                   jax.ShapeDtypeStruct((n_steps * per_step, PAGE_SIZE, hd), F32),
                   jax.ShapeDtypeStruct((n_steps * per_step, PAGE_SIZE, hd), F32)),
        grid_spec=pltpu.PrefetchScalarGridSpec(
            num_scalar_prefetch=1, grid=(ATT_HEADS, n_seq),
            in_specs=[tok(OFF_AQ), tok(OFF_AK), tok(OFF_AV),
                      pl.BlockSpec((None, blk, nb * blk), lambda h, b, ph: (h, 0, 0)),
                      pl.BlockSpec(memory_space=pl.ANY), pl.BlockSpec(memory_space=pl.ANY)],
            out_specs=(pl.BlockSpec((t_total, hd), lambda h, b, ph: (b, h)), gathered, gathered),
            scratch_shapes=[pltpu.VMEM((hd, t_total), BF16), pltpu.VMEM((1, t_total), F32),
                            pltpu.VMEM((1, t_total), F32), pltpu.VMEM((hd, t_total), F32),
                            pltpu.SemaphoreType.DMA((2, per_step))]),
        compiler_params=_cparams(("arbitrary", "arbitrary")),
        name="moba_prefill",
    )(phys, u, u, u, bias, cache_k, cache_v)


def _moba_select_kernel(km_ref, q_ref, o_ref, *, nblk):
    g = jnp.sum(km_ref[0] * q_ref[0][None], axis=-1)
    idx = lax.broadcasted_iota(jnp.int32, (nblk, ATT_HEADS), 0).astype(F32)
    o_ref[...] = jnp.zeros_like(o_ref)
    for j in range(MOBA_TOPK):
        mx = jnp.max(g, axis=0, keepdims=True)
        first = jnp.min(jnp.where(g == mx, idx, float(nblk)), axis=0, keepdims=True)
        o_ref[0, j:j + 1, :] = first.astype(jnp.int32)
        g = jnp.where(idx == first, -jnp.inf, g)


def _moba_select(kmean, q3):
    n_seq, nblk = kmean.shape[:2]
    return pl.pallas_call(
        functools.partial(_moba_select_kernel, nblk=nblk),
        out_shape=jax.ShapeDtypeStruct((n_seq, SUBLANES, ATT_HEADS), jnp.int32),
        grid=(n_seq,),
        in_specs=[pl.BlockSpec((1, nblk, ATT_HEADS, HEAD_DIM), lambda b: (b, 0, 0, 0)),
                  pl.BlockSpec((1, ATT_HEADS, HEAD_DIM), lambda b: (b, 0, 0))],
        out_specs=pl.BlockSpec((1, SUBLANES, ATT_HEADS), lambda b: (b, 0, 0)),
        compiler_params=_cparams(("parallel",)),
        name="moba_select",
    )(kmean, q3)


def _moba_decode_kernel(logi_ref, q_ref, kn_ref, vn_ref, bias_ref, bias0_ref, kg_ref, vg_ref, o_ref):
    b = pl.program_id(0)
    scale = HEAD_DIM ** -0.5

    for h in range(ATT_HEADS):
        q = q_ref[0, h:h + 1, :]
        q8 = jnp.broadcast_to(q, (SUBLANES, HEAD_DIM)).astype(BF16)
        logits = []
        for j in range(N_SEL_PAGES):
            lp = logi_ref[(b * ATT_HEADS + h) * N_SEL_PAGES + j]
            logits.append(_dot_nt(q8, kg_ref[h * N_SEL_PAGES + j].astype(BF16)) * scale
                          + bias_ref[h, pl.ds(lp, 1), :])
        self_logit = jnp.sum(q * kn_ref[0, h:h + 1, :], axis=-1, keepdims=True) * scale + bias0_ref[h:h + 1, 0:1]
        m = self_logit
        for lg in logits:
            m = jnp.maximum(m, jnp.max(lg, axis=-1, keepdims=True))
        p_self = jnp.exp(self_logit - m)
        den = p_self
        acc = p_self * vn_ref[0, h:h + 1, :]
        for j in range(N_SEL_PAGES):
            p = jnp.exp(logits[j] - m)
            den += jnp.sum(p, axis=-1, keepdims=True)
            acc += _dot(p.astype(BF16), vg_ref[h * N_SEL_PAGES + j].astype(BF16))
        o_ref[0, h:h + 1, :] = (acc / den)[0:1, :]


def _moba_decode(kg, vg, logi, q3, k3, v3, bias_s, bias0):
    n_seq = q3.shape[0]
    n_slots = ATT_HEADS * N_SEL_PAGES
    tok = pl.BlockSpec((1, ATT_HEADS, HEAD_DIM), lambda b, lg: (b, 0, 0))
    gathered = pl.BlockSpec((n_slots, PAGE_SIZE, HEAD_DIM), lambda b, lg: (b, 0, 0))
    return pl.pallas_call(
        _moba_decode_kernel,
        out_shape=jax.ShapeDtypeStruct((n_seq, ATT_HEADS, HEAD_DIM), F32),
        grid_spec=pltpu.PrefetchScalarGridSpec(
            num_scalar_prefetch=1, grid=(n_seq,),
            in_specs=[tok, tok, tok,
                      pl.BlockSpec(bias_s.shape, lambda b, lg: (0, 0, 0)),
                      pl.BlockSpec(bias0.shape, lambda b, lg: (0, 0)),
                      gathered, gathered],
            out_specs=tok),
        compiler_params=_cparams(("parallel",)),
        name="moba_decode",
    )(logi, q3, k3, v3, bias_s, bias0, kg, vg)


def _pad_rows(a, rows):
    n, w = a.shape
    return jnp.pad(a[:, None, :], ((0, 0), (0, rows - 1), (0, 0))).reshape(n * rows, w)


def _vec_state(a):
    if a.ndim == 2:
        a = jnp.broadcast_to(a[:, :, None], a.shape + (LANES,))
    return jnp.pad(a, ((0, 0), (0, SUBLANES - a.shape[1]), (0, 0)))


def kernel(x_prompt, x_sample, cache_k, cache_v, page_table, state_hgrn, state_mlstm_c, state_mlstm_n,
           state_mlstm_m, state_mlstm_conv, ln_ffn1, w_ffn1_gate, w_ffn1_up, w_ffn1_down, ln_mix, w_in, w_out,
           hgrn_lb_logits, hgrn_out_norm, rel_bias, mlstm_conv_w, mlstm_conv_b, mlstm_gate_bias,
           mlstm_out_norm, ln_ffn2, w_ffn2_gate, w_ffn2_up, w_ffn2_down, ln_final):
    depth = w_in.shape[0]
    bp, tp, d = x_prompt.shape
    bs, ts, _ = x_sample.shape
    assert ts == 1 and tp % MOBA_BLOCK == 0
    n_pages = page_table.shape[1]
    past_len = n_pages * PAGE_SIZE
    assert past_len % MOBA_BLOCK == 0 and past_len // MOBA_BLOCK >= MOBA_TOPK

    w_in_b = w_in.astype(BF16)
    w_gate_b = jnp.pad(w_in[:, :, N_MAIN:], ((0, 0), (0, 0), (0, LANES - N_GATE))).astype(BF16)
    row3 = lambda a: a.reshape(depth, 1, -1)
    ln1, lnm, ln2 = row3(ln_ffn1), row3(ln_mix), row3(ln_ffn2)
    lb = jnp.cumsum(jax.nn.softmax(hgrn_lb_logits.astype(F32), axis=0), axis=0)
    lb = lb - lb[0]
    llb, l1m, oml = row3(jnp.log(lb)), row3(jnp.log1p(-lb)), row3(1.0 - lb)
    hnorm, mnorm = row3(hgrn_out_norm), row3(mlstm_out_norm)
    conv_b = row3(mlstm_conv_b)
    gate_b = row3(jnp.pad(mlstm_gate_bias, ((0, 0), (0, LANES - N_GATE))))
    bias_tab = rel_bias.T.astype(F32)
    bias_p = _t5_bias_prompt(bias_tab, tp)
    bias_s = _t5_bias_sample(bias_tab, past_len)
    bias0 = jnp.broadcast_to(bias_tab[:, 0:1], (ATT_HEADS, LANES))

    zeros = lambda *s: jnp.zeros(s, F32)
    tm_f = 512
    tm_p = 1024 if (bp * tp) % 1024 == 0 else 512
    tm_o = 512
    ll_p = 256

    xp = x_prompt.reshape(bp * tp, d)
    xs = x_sample.reshape(bs, d)
    outs = {k: [] for k in ("ks", "vs", "hgp", "hgs", "cp", "cs", "np", "ns", "mp", "ms", "cvp", "cvs")}
    u_all = []

    for l in range(depth):
        xs, wg1, wu1, wd1 = _ffn(xs, ln1, w_ffn1_gate, w_ffn1_up, w_ffn1_down, l, bs, cast=True)
        xp, kmean = _ffn(xp, ln1, wg1, wu1, wd1, l, tm_f, kmean_of=(cache_k, page_table))
        us, gs = _inproj(xs, lnm, w_in_b, w_gate_b, l, bs)
        u, gates = _inproj(xp, lnm, w_in_b, w_gate_b, l, tm_p)

        heads = lambda a: a.reshape(bs, ATT_HEADS, HEAD_DIM)
        q3, k3, v3 = heads(us[:, OFF_AQ:OFF_AK]), heads(us[:, OFF_AK:OFF_AV]), heads(us[:, OFF_AV:OFF_MQK])
        sel = _moba_select(kmean.reshape(bs, -1, ATT_HEADS, HEAD_DIM), q3)[:, :MOBA_TOPK, :]
        sel = jnp.transpose(sel, (0, 2, 1))
        logi = (sel[..., None] * PAGES_PER_BLOCK
                + jnp.arange(PAGES_PER_BLOCK, dtype=jnp.int32)).reshape(bs, ATT_HEADS, N_SEL_PAGES)
        phys = jnp.take_along_axis(page_table[:, None, :], logi, axis=2)

        o_hg, hg_s = _hgrn(u, llb, l1m, oml, hnorm, zeros(bp, HG_HEADS, HEAD_DIM, HEAD_DIM), l, bp, tp, tp)
        o_att, kg, vg = _moba_prefill(u, bias_p, bp, tp, cache_k, cache_v, phys.reshape(-1), l)
        o_ml, c_s, n_s, m_s = _mlstm(u, gates, mlstm_conv_w, conv_b, gate_b, mnorm,
                                     zeros(bp, ML_HEADS, HEAD_DIM, HEAD_DIM), zeros(bp, SUBLANES, LANES),
                                     zeros(bp, SUBLANES, LANES), zeros(bp, SUBLANES, 2 * ML_WIDTH),
                                     l, bp, tp, tp, ll_p)
        u3 = u.reshape(bp, tp, N_MAIN)
        u_all.append(u3)
        outs["hgp"].append(hg_s)
        outs["cp"].append(c_s)
        outs["np"].append(n_s[:, :ML_HEADS])
        outs["mp"].append(m_s[:, :ML_HEADS, 0])
        outs["cvp"].append(u3[:, tp - (CONV_W - 1):, OFF_MQK:OFF_MV])

        o_hg_s, hg_ss = _hgrn(_pad_rows(us[:, :OFF_AQ], HG_CHUNK), llb, l1m, oml, hnorm, state_hgrn[l],
                              l, bs, HG_CHUNK, 1)
        conv0 = jnp.pad(state_mlstm_conv[l], ((0, 0), (SUBLANES - (CONV_W - 1), 0), (0, 0)))
        o_ml_s, c_ss, n_ss, m_ss = _mlstm(_pad_rows(us[:, OFF_MQK:], SAMPLE_PAD), _pad_rows(gs, SAMPLE_PAD),
                                          mlstm_conv_w, conv_b, gate_b, mnorm, state_mlstm_c[l],
                                          _vec_state(state_mlstm_n[l]), _vec_state(state_mlstm_m[l]), conv0,
                                          l, bs, SAMPLE_PAD, 1, SAMPLE_PAD, col0=OFF_MQK)
        o_att_s = _moba_decode(kg, vg, logi.reshape(-1), q3, k3, v3, bias_s, bias0).reshape(bs, ATT_WIDTH)
        o_att_s = o_att_s.astype(BF16)
        first = lambda a, rows: a.reshape(bs, rows, -1)[:, 0]

        xs, w_out_b = _outproj(xs, first(o_hg_s, HG_CHUNK), o_att_s, first(o_ml_s, SAMPLE_PAD), w_out, l, bs,
                               cast=True)
        xp = _outproj(xp, o_hg, o_att, o_ml, w_out_b, l, tm_o)
        closing = ln_final.reshape(1, d) if l == depth - 1 else None
        xs, wg2, wu2, wd2 = _ffn(xs, ln2, w_ffn2_gate, w_ffn2_up, w_ffn2_down, l, bs, cast=True,
                                 final_gain=closing)
        xp = _ffn(xp, ln2, wg2, wu2, wd2, l, tm_f, final_gain=closing)
        outs["ks"].append(k3.reshape(bs, 1, ATT_HEADS, HEAD_DIM))
        outs["vs"].append(v3.reshape(bs, 1, ATT_HEADS, HEAD_DIM))
        outs["hgs"].append(hg_ss)
        outs["cs"].append(c_ss)
        outs["ns"].append(n_ss[:, :ML_HEADS])
        outs["ms"].append(m_ss[:, :ML_HEADS, 0])
        outs["cvs"].append(jnp.concatenate([state_mlstm_conv[l][:, 1:], us[:, None, OFF_MQK:OFF_MV]], axis=1))

    y_prompt = xp.reshape(bp, tp, d)
    y_sample = xs.reshape(bs, 1, d)
    st = {k: jnp.stack(v) for k, v in outs.items()}
    heads_p = lambda lo, hi: jnp.stack([u3[:, :, lo:hi] for u3 in u_all]).reshape(depth, bp, tp, ATT_HEADS, HEAD_DIM)
    st["kp"], st["vp"] = heads_p(OFF_AK, OFF_AV), heads_p(OFF_AV, OFF_MQK)
    return (y_prompt, y_sample, st["kp"], st["vp"], st["ks"], st["vs"], st["hgp"], st["hgs"], st["cp"], st["cs"],
            st["np"], st["ns"], st["mp"], st["ms"], st["cvp"], st["cvs"])
```
